```python
import math
import jax
import jax.numpy as jnp
from jax import lax
import numpy as np

D_MODEL = 4096
BATCH = 4
SEQ = 2048
DEPTH = 2
DEC_BATCH = 128
DEC_SEQ = 8
PAST_LEN = 16384
PAGE_SIZE = 128

DN_HEADS = D_MODEL // 256
DN_DK = 128
DN_DV = 128
DN_WIDTH = DN_HEADS * DN_DV
DN_QKV = 2 * DN_HEADS * DN_DK + DN_WIDTH
DN_CONV = 4
DN_CHUNK = 64
S5_WIDTH = D_MODEL // 4
S5_GROUP = 16
S5_GROUPS = S5_WIDTH // S5_GROUP
S5_STATE = 64
S5_DT_MIN = 0.01
S5_DT_MAX = 0.1
RET_HEADS = D_MODEL // 512
RET_DK = 128
RET_DV = 128
RET_WIDTH = RET_HEADS * RET_DV
RET_CHUNK = 64
ROPE_BASE = 10000.0
N_BRANCH = 3
IN_SIZES = (DN_QKV, DN_WIDTH, DN_HEADS, DN_HEADS, S5_WIDTH, RET_HEADS * RET_DK,
            RET_HEADS * RET_DK, RET_WIDTH, RET_WIDTH, N_BRANCH * D_MODEL)
IN_COLS = sum(IN_SIZES)
N_EXPERTS = 16
N_EXPERT_GROUPS = 4
EXPERTS_PER_GROUP = N_EXPERTS // N_EXPERT_GROUPS
TOP_K = 2
D_FF = D_MODEL // 2
MOE_BLOCK = 128
DEEPNORM_ALPHA = (2 * DEPTH) ** 0.25
DEEPNORM_BETA = (8 * DEPTH) ** -0.25
LN_EPS = 1e-5
RMS_EPS = 1e-6

kernel_name = 'hybrid_deltanet_s5_retention_moe_step'

F32 = jnp.float32


def _split_points(sizes):
    pts, acc = [], 0
    for s in sizes[:-1]:
        acc += s
        pts.append(acc)
    return pts


def layer_norm(x, g, b):
    xf = x.astype(F32)
    mu = jnp.mean(xf, -1, keepdims=True)
    var = jnp.mean(jnp.square(xf - mu), -1, keepdims=True)
    return ((xf - mu) * lax.rsqrt(var + LN_EPS) * g.astype(F32) + b.astype(F32)).astype(x.dtype)


def rms_norm(x):
    return x * lax.rsqrt(jnp.mean(jnp.square(x), -1, keepdims=True) + RMS_EPS)


def l2norm(x):
    return x * lax.rsqrt(jnp.sum(jnp.square(x), -1, keepdims=True) + RMS_EPS)


def to_chunks(t, c):
    b, l, h = t.shape[:3]
    t = t.reshape((b, l // c, c, h) + t.shape[3:])
    return jnp.moveaxis(t, (1, 3), (0, 2))


def from_chunks(t):
    n, b, h, c = t.shape[:4]
    t = jnp.moveaxis(t, (0, 2), (1, 3))
    return t.reshape((b, n * c, h) + t.shape[4:])


def causal_conv(x, buf, w):
    xp = jnp.concatenate([buf.astype(x.dtype), x], axis=1)
    y = lax.conv_general_dilated(xp, w[:, None, :].astype(xp.dtype), (1,), 'VALID',
                                 dimension_numbers=('NWC', 'WIO', 'NWC'),
                                 feature_group_count=xp.shape[-1])
    return y, xp[:, -(DN_CONV - 1):]


def rotary(t, pos):
    half = t.shape[-1] // 2
    inv = ROPE_BASE ** (-jnp.arange(half, dtype=F32) / half)
    ang = pos.astype(F32)[:, None] * inv[None, :]
    cos = jnp.cos(ang)[None, :, None, :]
    sin = jnp.sin(ang)[None, :, None, :]
    t1, t2 = t[..., :half], t[..., half:]
    return jnp.concatenate([t1 * cos - t2 * sin, t2 * cos + t1 * sin], -1)


def gated_delta_scan(q, k, v, beta, g, s0):
    c = math.gcd(q.shape[1], DN_CHUNK)
    incl = jnp.tril(jnp.ones((c, c), bool))
    strict = jnp.tril(jnp.ones((c, c), bool), -1)

    def step(s, inp):
        qc, kc, vc, bc, gc = inp
        gcum = jnp.cumsum(gc, axis=-1)
        decay = jnp.exp(jnp.where(incl, gcum[..., :, None] - gcum[..., None, :], -jnp.inf))
        kk = jnp.einsum('bhtd,bhsd->bhts', kc, kc)
        m = jnp.where(strict, bc[..., :, None] * decay * kk, 0.0)
        g_in = jnp.exp(gcum)[..., None]
        rhs = bc[..., None] * (vc - g_in * jnp.einsum('bhtd,bhde->bhte', kc, s))
        u = lax.linalg.triangular_solve(m, rhs, left_side=True, lower=True, unit_diagonal=True)
        qk = jnp.einsum('bhtd,bhsd->bhts', qc, kc) * decay
        o = g_in * jnp.einsum('bhtd,bhde->bhte', qc, s) + jnp.einsum('bhts,bhse->bhte', qk, u)
        g_last = gcum[..., -1:]
        s = (jnp.exp(g_last)[..., None] * s
             + jnp.einsum('bhsd,bhse->bhde', kc * jnp.exp(g_last - gcum)[..., None], u))
        return s, o

    s, o = lax.scan(step, s0, tuple(to_chunks(t, c) for t in (q, k, v, beta, g)))
    return from_chunks(o), s


def retention_scan(q, k, v, log_gamma, r0):
    c = math.gcd(q.shape[1], RET_CHUNK)
    pos = jnp.arange(c, dtype=F32)
    rel = pos[:, None] - pos[None, :]
    decay = jnp.where(rel >= 0, jnp.exp(jnp.maximum(rel, 0.0) * log_gamma[:, None, None]), 0.0)
    q_decay = jnp.exp((pos + 1.0) * log_gamma[:, None])[..., None]
    k_decay = jnp.exp((c - 1.0 - pos) * log_gamma[:, None])[..., None]
    chunk_decay = jnp.exp(c * log_gamma)[:, None, None]

    def step(r, inp):
        qc, kc, vc = inp
        scores = jnp.einsum('bhtd,bhsd->bhts', qc, kc) * decay
        o = jnp.einsum('bhts,bhse->bhte', scores, vc) + q_decay * jnp.einsum('bhtd,bhde->bhte', qc, r)
        r = chunk_decay * r + jnp.einsum('bhsd,bhse->bhde', kc * k_decay, vc)
        return r, o

    r, o = lax.scan(step, r0, (to_chunks(q, c), to_chunks(k, c), to_chunks(v, c)))
    return from_chunks(o), r


def _linear_combine(e1, e2):
    a1, b1 = e1
    a2, b2 = e2
    return a1 * a2, a2 * b1 + b2


def s5_branch(u, h0, lam_re, lam_im, log_step, b_re, b_im, c_re, c_im, d_skip, w_glu, b_glu):
    b, l, _ = u.shape
    lam = lax.complex(lam_re.astype(F32), lam_im.astype(F32))
    lam_dt = lam * jnp.exp(log_step.astype(F32))[:, None]
    lam_bar = jnp.exp(lam_dt)
    b_bar = ((lam_bar - 1.0) / lam)[..., None] * lax.complex(b_re.astype(F32), b_im.astype(F32))
    uf = u.astype(F32)
    ug = uf.reshape(b, l, S5_GROUPS, S5_GROUP).astype(jnp.complex64)
    bu = jnp.einsum('gpc,blgc->blgp', b_bar, ug)
    _, h = lax.associative_scan(_linear_combine, (jnp.broadcast_to(lam_bar, bu.shape), bu), axis=1)
    powers = jnp.exp(lam_dt[None] * jnp.arange(1, l + 1, dtype=F32)[:, None, None])
    h0c = lax.complex(h0[..., 0].astype(F32), h0[..., 1].astype(F32))
    h = h + powers[None] * h0c[:, None]
    cc = lax.complex(c_re.astype(F32), c_im.astype(F32))
    y = jnp.real(jnp.einsum('gcp,blgp->blgc', cc, h)).reshape(b, l, S5_WIDTH) + d_skip.astype(F32) * uf
    z = jax.nn.gelu(y)
    z = z * jax.nn.sigmoid(z @ w_glu.astype(F32) + b_glu.astype(F32))
    h_last = h[:, -1]
    return z, jnp.stack([jnp.real(h_last), jnp.imag(h_last)], -1)


def token_mixers(x, pos, st_delta, st_conv, st_s5, st_ret, p, l):
    b, s, _ = x.shape
    proj = x @ p['w_in'][l]
    (dn_qkv, dn_z, dn_b, dn_a, s5_u, rq, rk, rv, rg, gates) = jnp.split(proj, _split_points(IN_SIZES), axis=-1)

    conv_out, new_conv = causal_conv(dn_qkv, st_conv, p['dn_conv_w'][l])
    qkv = jax.nn.silu(conv_out.astype(F32))
    q, k, v = jnp.split(qkv, [DN_HEADS * DN_DK, 2 * DN_HEADS * DN_DK], axis=-1)
    q = l2norm(q.reshape(b, s, DN_HEADS, DN_DK)) * DN_DK ** -0.5
    k = l2norm(k.reshape(b, s, DN_HEADS, DN_DK))
    v = v.reshape(b, s, DN_HEADS, DN_DV)
    beta = jax.nn.sigmoid(dn_b.astype(F32))
    g = -jnp.exp(p['dn_a_log'][l].astype(F32)) * jax.nn.softplus(dn_a.astype(F32) + p['dn_dt_bias'][l].astype(F32))
    o, new_delta = gated_delta_scan(q, k, v, beta, g, st_delta.astype(F32))
    o = rms_norm(o) * p['dn_norm_w'][l].astype(F32) * jax.nn.silu(dn_z.astype(F32).reshape(b, s, DN_HEADS, DN_DV))
    o_dn = o.reshape(b, s, DN_WIDTH).astype(x.dtype)

    o_s5, new_s5 = s5_branch(s5_u, st_s5, p['s5_lam_re'][l], p['s5_lam_im'][l], p['s5_log_step'][l],
                             p['s5_b_re'][l], p['s5_b_im'][l], p['s5_c_re'][l], p['s5_c_im'][l],
                             p['s5_d'][l], p['s5_w_glu'][l], p['s5_b_glu'][l])
    o_s5 = o_s5.astype(x.dtype)

    log_gamma = jnp.log1p(-jnp.exp2(-5.0 - jnp.arange(RET_HEADS, dtype=F32)))
    rq = rotary(rq.astype(F32).reshape(b, s, RET_HEADS, RET_DK), pos)
    rk = rotary(rk.astype(F32).reshape(b, s, RET_HEADS, RET_DK), pos) * RET_DK ** -0.5
    rv = rv.astype(F32).reshape(b, s, RET_HEADS, RET_DV)
    o, new_ret = retention_scan(rq, rk, rv, log_gamma, st_ret.astype(F32))
    o_ret = (rms_norm(o) * jax.nn.silu(rg.astype(F32).reshape(b, s, RET_HEADS, RET_DV)))
    o_ret = o_ret.reshape(b, s, RET_WIDTH).astype(x.dtype)

    g_dn, g_s5, g_ret = jnp.split(jax.nn.sigmoid(gates.astype(F32)), N_BRANCH, axis=-1)
    merged = (g_dn * (o_dn @ p['w_up_dn'][l]).astype(F32)
              + g_s5 * (o_s5 @ p['w_up_s5'][l]).astype(F32)
              + g_ret * (o_ret @ p['w_up_ret'][l]).astype(F32))
    out = merged.astype(x.dtype) @ p['w_o'][l]
    return out, new_delta, new_conv, new_s5, new_ret


def route(xt, router_w, router_b):
    logits = xt.astype(F32) @ router_w.astype(F32) + router_b.astype(F32)
    probs = jax.nn.softmax(logits, axis=-1)
    grouped = probs.reshape(-1, N_EXPERT_GROUPS, EXPERTS_PER_GROUP)
    group_score = lax.top_k(grouped, TOP_K)[0].sum(-1)
    g_sel = jnp.argmax(group_score, axis=-1).astype(jnp.int32)
    in_group = jnp.einsum('tge,tg->te', grouped, jax.nn.one_hot(g_sel, N_EXPERT_GROUPS, dtype=F32))
    top_p, top_i = lax.top_k(in_group, TOP_K)
    gate = top_p / jnp.sum(top_p, -1, keepdims=True)
    return g_sel[:, None] * EXPERTS_PER_GROUP + top_i.astype(jnp.int32), gate


def moe_ffn(x, router_w, router_b, w_gate, w_up, w_down):
    b, s, d = x.shape
    t = b * s
    xt = x.reshape(t, d)
    expert_idx, gate = route(xt, router_w, router_b)
    flat_e = expert_idx.reshape(-1)
    flat_w = gate.reshape(-1)
    flat_tok = jnp.repeat(jnp.arange(t, dtype=jnp.int32), TOP_K)
    n_pairs = t * TOP_K
    order = jnp.argsort(flat_e)
    e_sorted = flat_e[order]
    counts = jnp.zeros((N_EXPERTS,), jnp.int32).at[flat_e].add(1)
    padded = (counts + MOE_BLOCK - 1) // MOE_BLOCK * MOE_BLOCK
    start = jnp.cumsum(counts) - counts
    pstart = jnp.cumsum(padded) - padded
    dest = pstart[e_sorted] + jnp.arange(n_pairs, dtype=jnp.int32) - start[e_sorted]
    n_blocks = -(-(n_pairs + N_EXPERTS * (MOE_BLOCK - 1)) // MOE_BLOCK)
    n_slots = n_blocks * MOE_BLOCK
    slot_tok = jnp.full((n_slots,), t, jnp.int32).at[dest].set(flat_tok[order])
    slot_w = jnp.zeros((n_slots,), F32).at[dest].set(flat_w[order])
    block_exp = jnp.minimum(jnp.searchsorted(jnp.cumsum(padded), jnp.arange(n_blocks, dtype=jnp.int32) * MOE_BLOCK,
                                             side='right'), N_EXPERTS - 1).astype(jnp.int32)
    x_pad = jnp.concatenate([xt, jnp.zeros((1, d), xt.dtype)], axis=0)
    xb = x_pad[slot_tok].reshape(n_blocks, MOE_BLOCK, d)

    def expert_block(args):
        xblk, e = args
        h = jax.nn.silu(xblk @ w_gate[e]) * (xblk @ w_up[e])
        return h @ w_down[e]

    yb = lax.map(expert_block, (xb, block_exp))
    y = jnp.zeros((t + 1, d), F32).at[slot_tok].add(yb.reshape(n_slots, d).astype(F32) * slot_w[:, None])
    return y[:t].astype(x.dtype).reshape(b, s, d)


def run_trunk(x, pos, st_delta, st_conv, st_s5, st_ret, p):
    x = layer_norm(x, p['ln_in_g'], p['ln_in_b'])
    nd, nc, ns, nr = [], [], [], []
    for l in range(DEPTH):
        h, d_new, c_new, s_new, r_new = token_mixers(x, pos, st_delta[l], st_conv[l], st_s5[l], st_ret[l], p, l)
        nd.append(d_new)
        nc.append(c_new)
        ns.append(s_new)
        nr.append(r_new)
        x = layer_norm(DEEPNORM_ALPHA * x + h, p['ln1_g'][l], p['ln1_b'][l])
        f = moe_ffn(x, p['router_w'], p['router_b'], p['w_gate_e'][l], p['w_up_e'][l], p['w_down_e'][l])
        x = layer_norm(DEEPNORM_ALPHA * x + f, p['ln2_g'][l], p['ln2_b'][l])
    return (x, jnp.stack(nd).astype(st_delta.dtype), jnp.stack(nc).astype(st_conv.dtype),
            jnp.stack(ns).astype(st_s5.dtype), jnp.stack(nr).astype(st_ret.dtype))


def setup_inputs(seed: int = 0) -> dict:
    key = jax.random.key(seed)
    ks = iter(jax.random.split(key, 48))

    def nrm(shape, scale):
        return jax.random.normal(next(ks), shape, F32) * scale

    def unif(shape, lo, hi):
        return jax.random.uniform(next(ks), shape, F32, lo, hi)

    dt = jnp.exp(unif((DEPTH, DN_HEADS), math.log(1e-3), math.log(1e-1)))
    return {
        'x_prompt': nrm((BATCH, SEQ, D_MODEL), 1.0),
        'x_sample': nrm((DEC_BATCH, DEC_SEQ, D_MODEL), 1.0),
        'state_delta': nrm((DEPTH, DEC_BATCH, DN_HEADS, DN_DK, DN_DV), DN_DK ** -0.5),
        'state_conv': nrm((DEPTH, DEC_BATCH, DN_CONV - 1, DN_QKV), 1.0),
        'state_s5': nrm((DEPTH, DEC_BATCH, S5_GROUPS, S5_STATE, 2), 0.5),
        'state_ret': nrm((DEPTH, DEC_BATCH, RET_HEADS, RET_DK, RET_DV), 0.5),
        'ln_in_g': 1.0 + nrm((D_MODEL,), 0.02),
        'ln_in_b': nrm((D_MODEL,), 0.02),
        'w_in': nrm((DEPTH, D_MODEL, IN_COLS), D_MODEL ** -0.5),
        'dn_conv_w': nrm((DEPTH, DN_CONV, DN_QKV), DN_CONV ** -0.5),
        'dn_a_log': jnp.log(unif((DEPTH, DN_HEADS), 1.0, 16.0)),
        'dn_dt_bias': dt + jnp.log(-jnp.expm1(-dt)),
        'dn_norm_w': 1.0 + nrm((DEPTH, DN_DV), 0.02),
        's5_lam_re': -0.5 + nrm((DEPTH, S5_GROUPS, S5_STATE), 0.01),
        's5_lam_im': jnp.pi * jnp.arange(S5_STATE, dtype=F32) + nrm((DEPTH, S5_GROUPS, S5_STATE), 0.01),
        's5_log_step': unif((DEPTH, S5_GROUPS), math.log(S5_DT_MIN), math.log(S5_DT_MAX)),
        's5_b_re': nrm((DEPTH, S5_GROUPS, S5_STATE, S5_GROUP), (2 * S5_GROUP) ** -0.5),
        's5_b_im': nrm((DEPTH, S5_GROUPS, S5_STATE, S5_GROUP), (2 * S5_GROUP) ** -0.5),
        's5_c_re': nrm((DEPTH, S5_GROUPS, S5_GROUP, S5_STATE), S5_STATE ** -0.5),
        's5_c_im': nrm((DEPTH, S5_GROUPS, S5_GROUP, S5_STATE), S5_STATE ** -0.5),
        's5_d': nrm((DEPTH, S5_WIDTH), 1.0),
        's5_w_glu': nrm((DEPTH, S5_WIDTH, S5_WIDTH), S5_WIDTH ** -0.5),
        's5_b_glu': nrm((DEPTH, S5_WIDTH), 0.02),
        'w_up_dn': nrm((DEPTH, DN_WIDTH, D_MODEL), DN_WIDTH ** -0.5),
        'w_up_s5': nrm((DEPTH, S5_WIDTH, D_MODEL), S5_WIDTH ** -0.5),
        'w_up_ret': nrm((DEPTH, RET_WIDTH, D_MODEL), RET_WIDTH ** -0.5),
        'w_o': nrm((DEPTH, D_MODEL, D_MODEL), D_MODEL ** -0.5 * DEEPNORM_BETA),
        'ln1_g': 1.0 + nrm((DEPTH, D_MODEL), 0.02),
        'ln1_b': nrm((DEPTH, D_MODEL), 0.02),
        'router_w': nrm((D_MODEL, N_EXPERTS), D_MODEL ** -0.5),
        'router_b': nrm((N_EXPERTS,), 0.01),
        'w_gate_e': nrm((DEPTH, N_EXPERTS, D_MODEL, D_FF), D_MODEL ** -0.5),
        'w_up_e': nrm((DEPTH, N_EXPERTS, D_MODEL, D_FF), D_MODEL ** -0.5),
        'w_down_e': nrm((DEPTH, N_EXPERTS, D_FF, D_MODEL), D_FF ** -0.5 * DEEPNORM_BETA),
        'ln2_g': 1.0 + nrm((DEPTH, D_MODEL), 0.02),
        'ln2_b': nrm((DEPTH, D_MODEL), 0.02),
    }


def reference(x_prompt, x_sample, state_delta, state_conv, state_s5, state_ret,
              ln_in_g, ln_in_b, w_in, dn_conv_w, dn_a_log, dn_dt_bias, dn_norm_w,
              s5_lam_re, s5_lam_im, s5_log_step, s5_b_re, s5_b_im, s5_c_re, s5_c_im,
              s5_d, s5_w_glu, s5_b_glu, w_up_dn, w_up_s5, w_up_ret, w_o, ln1_g, ln1_b,
              router_w, router_b, w_gate_e, w_up_e, w_down_e, ln2_g, ln2_b):
    p = dict(ln_in_g=ln_in_g, ln_in_b=ln_in_b, w_in=w_in, dn_conv_w=dn_conv_w, dn_a_log=dn_a_log,
             dn_dt_bias=dn_dt_bias, dn_norm_w=dn_norm_w, s5_lam_re=s5_lam_re, s5_lam_im=s5_lam_im,
             s5_log_step=s5_log_step, s5_b_re=s5_b_re, s5_b_im=s5_b_im, s5_c_re=s5_c_re, s5_c_im=s5_c_im,
             s5_d=s5_d, s5_w_glu=s5_w_glu, s5_b_glu=s5_b_glu, w_up_dn=w_up_dn, w_up_s5=w_up_s5,
             w_up_ret=w_up_ret, w_o=w_o, ln1_g=ln1_g, ln1_b=ln1_b, router_w=router_w, router_b=router_b,
             w_gate_e=w_gate_e, w_up_e=w_up_e, w_down_e=w_down_e, ln2_g=ln2_g, ln2_b=ln2_b)
    bp, lp = x_prompt.shape[:2]
    ls = x_sample.shape[1]

    def zeros_like_state(st, nb):
        return jnp.zeros((DEPTH, nb) + st.shape[2:], st.dtype)

    y_prompt, dp, cp, sp, rp = run_trunk(
        x_prompt, jnp.arange(lp, dtype=jnp.int32),
        zeros_like_state(state_delta, bp), zeros_like_state(state_conv, bp),
        zeros_like_state(state_s5, bp), zeros_like_state(state_ret, bp), p)
    y_sample, ds, cs, ss, rs = run_trunk(
        x_sample, PAST_LEN + jnp.arange(ls, dtype=jnp.int32),
        state_delta, state_conv, state_s5, state_ret, p)
    return (y_prompt, y_sample, dp, cp, sp, rp, ds, cs, ss, rs)
```

```python
import functools
import math

import jax
import jax.numpy as jnp
from jax import lax
from jax.experimental import pallas as pl
from jax.experimental.pallas import tpu as pltpu

F32 = jnp.float32
BF16 = jnp.bfloat16
HI = lax.Precision.HIGHEST

LANES = 128
SUBLANES = 8
VMEM_LIMIT = 56 * 1024 * 1024

LN_EPS = 1e-5
RMS_EPS = 1e-6
ROPE_BASE = 10000.0
PAST_LEN = 16384
CHUNK = 64
N_EXPERT_GROUPS = 4
TOP_K = 2
S5_GROUP_BLOCK = 8

ROW_TILE = 256
MM_TM = 1024
MM_TN = 256
MOE_BM = 256


def _cparams(sem):
    return pltpu.CompilerParams(dimension_semantics=sem, vmem_limit_bytes=VMEM_LIMIT)


def _bdot(a, b):
    return jnp.dot(a.astype(BF16), b.astype(BF16), preferred_element_type=F32)


def _bdot_nt(a, b):
    return lax.dot_general(a.astype(BF16), b.astype(BF16), (((1,), (1,)), ((), ())),
                           preferred_element_type=F32)


def _bdot_tn(a, b):
    return lax.dot_general(a.astype(BF16), b.astype(BF16), (((0,), (0,)), ((), ())),
                           preferred_element_type=F32)


def _hdot(a, b):
    return jnp.dot(a, b, precision=HI, preferred_element_type=F32)


def _ln_math(x, g, b):
    mu = jnp.mean(x, -1, keepdims=True)
    xc = x - mu
    var = jnp.mean(xc * xc, -1, keepdims=True)
    return xc * lax.rsqrt(var + LN_EPS) * g + b


def _ln_in_kernel(xp_ref, xs_ref, g_ref, b_ref, of_ref, ob_ref, *, n_prompt_tiles):
    i = pl.program_id(0)

    def run(src):
        y = _ln_math(src[...], g_ref[...], b_ref[...])
        of_ref[...] = y
        ob_ref[...] = y.astype(BF16)

    @pl.when(i < n_prompt_tiles)
    def _():
        run(xp_ref)

    @pl.when(i >= n_prompt_tiles)
    def _():
        run(xs_ref)


def _ln_in(xp, xs, g, b):
    tp, d = xp.shape
    ts = xs.shape[0]
    tm = ROW_TILE
    npt, nst = tp // tm, ts // tm
    t = tp + ts
    return pl.pallas_call(
        functools.partial(_ln_in_kernel, n_prompt_tiles=npt),
        grid=(npt + nst,),
        in_specs=[pl.BlockSpec((tm, d), lambda i: (jnp.minimum(i, npt - 1), 0)),
                  pl.BlockSpec((tm, d), lambda i: (jnp.maximum(i - npt, 0), 0)),
                  pl.BlockSpec((1, d), lambda i: (0, 0)),
                  pl.BlockSpec((1, d), lambda i: (0, 0))],
        out_specs=[pl.BlockSpec((tm, d), lambda i: (i, 0)),
                   pl.BlockSpec((tm, d), lambda i: (i, 0))],
        out_shape=[jax.ShapeDtypeStruct((t, d), F32), jax.ShapeDtypeStruct((t, d), BF16)],
        compiler_params=_cparams(("arbitrary",)),
        name="ln_in",
    )(xp, xs, g.reshape(1, d), b.reshape(1, d))


def _ln1_router_kernel(x_ref, y_ref, g_ref, b_ref, rw_ref, rb_ref, of_ref, ob_ref, pr_ref, *, alpha):
    z = _ln_math(alpha * x_ref[...] + y_ref[...], g_ref[...], b_ref[...])
    of_ref[...] = z
    ob_ref[...] = z.astype(BF16)
    logits = _hdot(z, rw_ref[...]) + rb_ref[...]
    m = jnp.max(logits, -1, keepdims=True)
    e = jnp.exp(logits - m)
    pr_ref[...] = e / jnp.sum(e, -1, keepdims=True)


def _ln1_router(x, y, g, b, rw, rb, alpha, layer):
    t, d = x.shape
    tm = ROW_TILE
    ep = rw.shape[1]
    return pl.pallas_call(
        functools.partial(_ln1_router_kernel, alpha=alpha),
        grid=(t // tm,),
        in_specs=[pl.BlockSpec((tm, d), lambda i: (i, 0)),
                  pl.BlockSpec((tm, d), lambda i: (i, 0)),
                  pl.BlockSpec((None, 1, d), lambda i: (layer, 0, 0)),
                  pl.BlockSpec((None, 1, d), lambda i: (layer, 0, 0)),
                  pl.BlockSpec((d, ep), lambda i: (0, 0)),
                  pl.BlockSpec((1, ep), lambda i: (0, 0))],
        out_specs=[pl.BlockSpec((tm, d), lambda i: (i, 0)),
                   pl.BlockSpec((tm, d), lambda i: (i, 0)),
                   pl.BlockSpec((tm, ep), lambda i: (i, 0))],
        out_shape=[jax.ShapeDtypeStruct((t, d), F32), jax.ShapeDtypeStruct((t, d), BF16),
                   jax.ShapeDtypeStruct((t, ep), F32)],
        compiler_params=_cparams(("arbitrary",)),
        name="ln1_router",
    )(x, y, g, b, rw, rb)


def _mm_kernel(a_ref, w_ref, o_ref):
    o_ref[...] = jnp.dot(a_ref[...], w_ref[...].astype(BF16),
                         preferred_element_type=F32).astype(o_ref.dtype)


def _pick(n, pref):
    for c in (pref, 512, 256, 128):
        if c <= pref and n % c == 0:
            return c
    return n


def _matmul(a, w, *, layer=None, tile_major=False, out_dtype=F32, name="mm"):
    m, k = a.shape
    n = w.shape[-1]
    tm = _pick(m, MM_TM)
    tn = _pick(n, MM_TN)
    if w.ndim == 3:
        w_spec = pl.BlockSpec((None, k, tn), lambda i, j: (layer, 0, j))
    else:
        w_spec = pl.BlockSpec((k, tn), lambda i, j: (0, j))
    if tile_major:
        out_spec = pl.BlockSpec((None, tm, tn), lambda i, j: (j, i, 0))
        out_shape = jax.ShapeDtypeStruct((n // tn, m, tn), out_dtype)
    else:
        out_spec = pl.BlockSpec((tm, tn), lambda i, j: (i, j))
        out_shape = jax.ShapeDtypeStruct((m, n), out_dtype)
    return pl.pallas_call(
        _mm_kernel,
        grid=(m // tm, n // tn),
        in_specs=[pl.BlockSpec((tm, k), lambda i, j: (i, 0)), w_spec],
        out_specs=out_spec,
        out_shape=out_shape,
        compiler_params=_cparams(("arbitrary", "arbitrary")),
        name=name,
    )(a, w)


def _upmerge_kernel(odn, os5, oret, gdn, gs5, gret, wdn, ws5, wret, o_ref):
    def branch(o, w, g):
        return jax.nn.sigmoid(g[...]) * jnp.dot(o[...], w[...].astype(BF16), preferred_element_type=F32)

    o_ref[...] = (branch(odn, wdn, gdn) + branch(os5, ws5, gs5) + branch(oret, wret, gret)).astype(o_ref.dtype)


def _upmerge(o_dn, o_s5, o_ret, gates, w_dn, w_s5, w_ret, layer):
    t = o_dn.shape[0]
    d = w_dn.shape[-1]
    tm = _pick(t, MM_TM)
    tn = _pick(d, MM_TN)
    nj = d // tn

    def a_spec(a):
        return pl.BlockSpec((tm, a.shape[1]), lambda i, j: (i, 0))

    def g_spec(off):
        return pl.BlockSpec((tm, tn), lambda i, j: (i, off * nj + j))

    def w_spec(w):
        return pl.BlockSpec((None, w.shape[1], tn), lambda i, j: (layer, 0, j))

    return pl.pallas_call(
        _upmerge_kernel,
        grid=(t // tm, nj),
        in_specs=[a_spec(o_dn), a_spec(o_s5), a_spec(o_ret), g_spec(0), g_spec(1), g_spec(2),
                  w_spec(w_dn), w_spec(w_s5), w_spec(w_ret)],
        out_specs=pl.BlockSpec((tm, tn), lambda i, j: (i, j)),
        out_shape=jax.ShapeDtypeStruct((t, d), BF16),
        compiler_params=_cparams(("arbitrary", "arbitrary")),
        name="upmerge",
    )(o_dn, o_s5, o_ret, gates, gates, gates, w_dn, w_s5, w_ret)


def _chunk_masks(r, c):
    ti = lax.broadcasted_iota(jnp.int32, (r, r), 0)
    si = lax.broadcasted_iota(jnp.int32, (r, r), 1)
    same = (ti // c) == (si // c)
    return same, same & (si <= ti), same & (si < ti)


def _shift_rows_carry(x, prev8, s):
    rolled = pltpu.roll(x, s, 0)
    prev_rolled = pltpu.roll(prev8, s, 0)
    row8 = lax.broadcasted_iota(jnp.int32, prev8.shape, 0)
    first = jnp.where(row8 < s, prev_rolled, rolled[0:SUBLANES])
    return jnp.concatenate([first, rolled[SUBLANES:]], axis=0)


def _shift_rows_seq8(x, bufx, s, hist):
    r = x.shape[0]
    rolled = pltpu.roll(x, s, 0)
    brolled = pltpu.roll(bufx, (r - (hist - s)) % r, 0)
    t8 = lax.broadcasted_iota(jnp.int32, x.shape, 0) % SUBLANES
    return jnp.where(t8 < s, brolled, rolled)


def _causal_conv(x, w_ref, shift):
    n = w_ref.shape[0]
    y = w_ref[n - 1:n, :] * x
    for s in range(1, n):
        y = y + w_ref[n - 1 - s:n - s, :] * shift(s)
    return y


def _silu(x):
    return x * jax.nn.sigmoid(x)


def _l2norm(x):
    return x * lax.rsqrt(jnp.sum(x * x, -1, keepdims=True) + RMS_EPS)


def _rms(x):
    return x * lax.rsqrt(jnp.mean(x * x, -1, keepdims=True) + RMS_EPS)


def _delta_gates(ba_ref, bat_ref, hpr_ref, hpc_ref, rows_ref, nh, c):
    r = ba_ref.shape[0]
    same, incl, _ = _chunk_masks(r, c)
    ti = lax.broadcasted_iota(jnp.int32, (r, r), 0)
    si = lax.broadcasted_iota(jnp.int32, (r, r), 1)
    incl_t = same & (ti <= si)
    ba = ba_ref[...]
    lane = lax.broadcasted_iota(jnp.int32, ba.shape, 1)
    beta_cols = jax.nn.sigmoid(ba)
    g_cols = -jnp.exp(hpr_ref[0:1, :]) * jax.nn.softplus(ba + hpr_ref[1:2, :])
    g_cols = jnp.where((lane >= nh) & (lane < 2 * nh), g_cols, 0.0)
    gcum_cols = _hdot(incl.astype(F32), g_cols)
    gtot_cols = _hdot(same.astype(F32), g_cols)
    bat = bat_ref[...]
    g_rows = -jnp.exp(hpc_ref[:, 0:1]) * jax.nn.softplus(bat + hpc_ref[:, 1:2])
    rows_ref[...] = _hdot(g_rows, incl_t.astype(F32))
    return beta_cols, gcum_cols, gtot_cols


def _pick_col(cols, idx):
    lane = lax.broadcasted_iota(jnp.int32, cols.shape, 1)
    return jnp.sum(jnp.where(lane == idx, cols, 0.0), -1, keepdims=True)


def _delta_prep(q, k, v, beta_c, gc_c, gt_c, gc_r, c, w_s, u0_s, kd_s, eg_s, gin_s):
    r = q.shape[0]
    _, incl, strict = _chunk_masks(r, c)
    dec = jnp.exp(jnp.minimum(gc_c - gc_r, 0.0))
    kk = _bdot_nt(k, k)
    qk = _bdot_nt(q, k)
    m = jnp.where(strict, beta_c * dec * kk, 0.0)
    a = jnp.where(incl, dec * qk, 0.0)
    eye = (lax.broadcasted_iota(jnp.int32, (r, r), 0) == lax.broadcasted_iota(jnp.int32, (r, r), 1)).astype(F32)
    x = eye - m
    pw = m
    for _ in range(int(math.log2(c)) - 1):
        pw = _hdot(pw, pw)
        x = x + _hdot(x, pw)
    gin = jnp.exp(gc_c)
    wu = _hdot(x, jnp.concatenate([beta_c * gin * k, beta_c * v], axis=1))
    dk = k.shape[1]
    w_s[...] = wu[:, :dk]
    u0_s[...] = wu[:, dk:]
    kd_s[...] = k * jnp.exp(gt_c - gc_c)
    eg_s[...] = jnp.broadcast_to(jnp.exp(gt_c), eg_s.shape)
    gin_s[...] = jnp.broadcast_to(gin, gin_s.shape)
    return a


def _delta_unit(r0, c, s, q_s, w_s, u0_s, kd_s, eg_s, gin_s, u_s, qs_s):
    rows = pl.ds(r0, c)
    ws = _bdot(w_s[rows, :], s)
    qs = _bdot(q_s[rows, :], s)
    u = u0_s[rows, :] - ws
    u_s[rows, :] = u
    qs_s[rows, :] = gin_s[rows, :] * qs
    return eg_s[pl.ds(r0, 1), :] * s + _bdot_tn(kd_s[rows, :], u)


def _delta_finish(a, u_s, qs_s, z, nw_ref):
    o = qs_s[...] + _bdot(a, u_s[...])
    return _rms(o) * nw_ref[...] * _silu(z)


def _delta_prompt_kernel(q_ref, k_ref, v_ref, z_ref, ba_ref, bat_ref, hpr_ref, hpc_ref,
                         cwq_ref, cwk_ref, cwv_ref, nw_ref,
                         o_ref, st_ref,
                         s_sc, pq_sc, pk_sc, pv_sc, rows_sc, q_s, w_s, u0_s, kd_s, eg_s, gin_s, u_s, qs_s,
                         *, nh, hb, c, dk):
    p = pl.program_id(1)
    j = pl.program_id(2)
    r = q_ref.shape[0]

    @pl.when(j == 0)
    def _():
        s_sc[...] = jnp.zeros_like(s_sc)
        pq_sc[...] = jnp.zeros_like(pq_sc)
        pk_sc[...] = jnp.zeros_like(pk_sc)
        pv_sc[...] = jnp.zeros_like(pv_sc)

    beta_cols, gcum_cols, gtot_cols = _delta_gates(ba_ref, bat_ref, hpr_ref, hpc_ref, rows_sc, nh, c)

    def conv(x_ref, w_ref, prev_sc):
        x = x_ref[...]
        prev8 = prev_sc[...]
        y = _causal_conv(x, w_ref, lambda s: _shift_rows_carry(x, prev8, s))
        prev_sc[...] = x[r - SUBLANES:, :]
        return _silu(y)

    qa = conv(q_ref, cwq_ref, pq_sc)
    ka = conv(k_ref, cwk_ref, pk_sc)
    va = conv(v_ref, cwv_ref, pv_sc)
    z = z_ref[...]

    for i in range(hb):
        h = p * hb + i
        cols = slice(i * dk, (i + 1) * dk)
        q = _l2norm(qa[:, cols]) * (dk ** -0.5)
        k = _l2norm(ka[:, cols])
        v = va[:, cols]
        beta_c = _pick_col(beta_cols, h)
        gc_c = _pick_col(gcum_cols, nh + h)
        gt_c = _pick_col(gtot_cols, nh + h)
        gc_r = rows_sc[pl.ds(nh + h, 1), :]
        q_s[...] = q
        a = _delta_prep(q, k, v, beta_c, gc_c, gt_c, gc_r, c, w_s, u0_s, kd_s, eg_s, gin_s)

        def unit(n, s):
            return _delta_unit(pl.multiple_of(n * c, c), c, s, q_s, w_s, u0_s, kd_s, eg_s, gin_s, u_s, qs_s)

        s_new = lax.fori_loop(0, r // c, unit, s_sc[i])
        s_sc[i] = s_new
        st_ref[0, i] = s_new
        o_ref[:, cols] = _delta_finish(a, u_s, qs_s, z[:, cols], nw_ref).astype(o_ref.dtype)


def _delta_sample_kernel(q_ref, k_ref, v_ref, z_ref, ba_ref, bat_ref, hpr_ref, hpc_ref,
                         cwq_ref, cwk_ref, cwv_ref, nw_ref, bq_ref, bk_ref, bv_ref, sin_ref,
                         o_ref, st_ref,
                         rows_sc, q_s, w_s, u0_s, kd_s, eg_s, gin_s, u_s, qs_s,
                         *, nh, hb, c, dk, hist):
    p = pl.program_id(1)
    r = q_ref.shape[0]
    beta_cols, gcum_cols, gtot_cols = _delta_gates(ba_ref, bat_ref, hpr_ref, hpc_ref, rows_sc, nh, c)

    def conv(x_ref, w_ref, b_ref):
        x = x_ref[...]
        bufx = b_ref[...]
        return _silu(_causal_conv(x, w_ref, lambda s: _shift_rows_seq8(x, bufx, s, hist)))

    qa = conv(q_ref, cwq_ref, bq_ref)
    ka = conv(k_ref, cwk_ref, bk_ref)
    va = conv(v_ref, cwv_ref, bv_ref)
    z = z_ref[...]

    for i in range(hb):
        h = p * hb + i
        cols = slice(i * dk, (i + 1) * dk)
        q = _l2norm(qa[:, cols]) * (dk ** -0.5)
        k = _l2norm(ka[:, cols])
        v = va[:, cols]
        beta_c = _pick_col(beta_cols, h)
        gc_c = _pick_col(gcum_cols, nh + h)
        gt_c = _pick_col(gtot_cols, nh + h)
        gc_r = rows_sc[pl.ds(nh + h, 1), :]
        q_s[...] = q
        a = _delta_prep(q, k, v, beta_c, gc_c, gt_c, gc_r, c, w_s, u0_s, kd_s, eg_s, gin_s)

        def unit(n, carry):
            s_new = _delta_unit(pl.multiple_of(n * c, c), c, sin_ref[n, i], q_s, w_s, u0_s, kd_s, eg_s,
                                gin_s, u_s, qs_s)
            st_ref[n, i] = s_new
            return carry

        lax.fori_loop(0, r // c, unit, 0)
        o_ref[:, cols] = _delta_finish(a, u_s, qs_s, z[:, cols], nw_ref).astype(o_ref.dtype)


def _delta_scratch(r, dk, dv, nrows):
    return [pltpu.VMEM((nrows, r), F32),
            pltpu.VMEM((r, dk), F32),
            pltpu.VMEM((r, dk), F32),
            pltpu.VMEM((r, dv), F32),
            pltpu.VMEM((r, dk), F32),
            pltpu.VMEM((r, dv), F32),
            pltpu.VMEM((r, dv), F32),
            pltpu.VMEM((r, dv), F32),
            pltpu.VMEM((r, dv), F32)]


def _delta_common_specs(tiles_per_seg, row_map, hp_shapes, layer, tn, r, nh2):
    def seg(n):
        return pl.BlockSpec((None, r, tn), lambda *g: (n * tiles_per_seg + g[1], row_map(*g), 0))

    def cw(n):
        return pl.BlockSpec((None, hp_shapes["taps"], tn), lambda *g: (layer, 0, n * tiles_per_seg + g[1]))

    return [seg(0), seg(1), seg(2), seg(3),
            pl.BlockSpec((r, LANES), lambda *g: (row_map(*g), 0)),
            pl.BlockSpec((nh2, r), lambda *g: (0, row_map(*g))),
            pl.BlockSpec((2, LANES), lambda *g: (0, 0)),
            pl.BlockSpec((nh2, 2), lambda *g: (0, 0)),
            cw(0), cw(1), cw(2),
            pl.BlockSpec((None, 1, hp_shapes["dv"]), lambda *g: (layer, 0, 0))]


def _delta_prompt(p_dn, ba, bat, hpr, hpc, conv_w, norm_w, layer, *, nb, seq, nh, dk, dv, row0):
    tn = p_dn.shape[-1]
    hb = tn // dk
    r = min(ROW_TILE, seq)
    c = math.gcd(seq, CHUNK)
    npair = nh // hb
    nt = seq // r
    t0 = row0 // r
    kern = functools.partial(_delta_prompt_kernel, nh=nh, hb=hb, c=c, dk=dk)
    specs = _delta_common_specs(npair, lambda b, p, j: t0 + b * nt + j, {"taps": conv_w.shape[1], "dv": dv},
                                layer, tn, r, 2 * nh)
    return pl.pallas_call(
        kern,
        grid=(nb, npair, nt),
        in_specs=specs,
        out_specs=[pl.BlockSpec((r, tn), lambda b, p, j: (b * nt + j, p)),
                   pl.BlockSpec((1, hb, dk, dv), lambda b, p, j: (b, p, 0, 0))],
        out_shape=[jax.ShapeDtypeStruct((nb * seq, nh * dv), BF16),
                   jax.ShapeDtypeStruct((nb, nh, dk, dv), F32)],
        scratch_shapes=[pltpu.VMEM((hb, dk, dv), F32),
                        pltpu.VMEM((SUBLANES, tn), F32), pltpu.VMEM((SUBLANES, tn), F32),
                        pltpu.VMEM((SUBLANES, tn), F32)] + _delta_scratch(r, dk, dv, 2 * nh),
        compiler_params=_cparams(("arbitrary", "arbitrary", "arbitrary")),
        name="delta_prompt",
    )(p_dn, p_dn, p_dn, p_dn, ba, bat, hpr, hpc, conv_w, conv_w, conv_w, norm_w)


def _delta_sample(p_dn, ba, bat, hpr, hpc, conv_w, norm_w, bufx, state, layer, *, nb, seq, nh, dk, dv, row0,
                  hist):
    tn = p_dn.shape[-1]
    hb = tn // dk
    c = seq
    r = min(ROW_TILE, nb * seq)
    npair = nh // hb
    nt = nb * seq // r
    spt = r // seq
    t0 = row0 // r
    kern = functools.partial(_delta_sample_kernel, nh=nh, hb=hb, c=c, dk=dk, hist=hist)
    specs = _delta_common_specs(npair, lambda i, p: t0 + i, {"taps": conv_w.shape[1], "dv": dv},
                                layer, tn, r, 2 * nh)

    def buf(n):
        return pl.BlockSpec((r, tn), lambda i, p: (i, n * npair + p))

    specs += [buf(0), buf(1), buf(2),
              pl.BlockSpec((None, spt, hb, dk, dv), lambda i, p: (layer, i, p, 0, 0))]
    return pl.pallas_call(
        kern,
        grid=(nt, npair),
        in_specs=specs,
        out_specs=[pl.BlockSpec((r, tn), lambda i, p: (i, p)),
                   pl.BlockSpec((spt, hb, dk, dv), lambda i, p: (i, p, 0, 0))],
        out_shape=[jax.ShapeDtypeStruct((nb * seq, nh * dv), BF16),
                   jax.ShapeDtypeStruct((nb, nh, dk, dv), F32)],
        scratch_shapes=_delta_scratch(r, dk, dv, 2 * nh),
        compiler_params=_cparams(("arbitrary", "arbitrary")),
        name="delta_sample",
    )(p_dn, p_dn, p_dn, p_dn, ba, bat, hpr, hpc, conv_w, conv_w, conv_w, norm_w, bufx, bufx, bufx, state)


def _rotary(x, cs_ref, sn_ref):
    half = x.shape[1] // 2
    return x * cs_ref[...] + pltpu.roll(x, half, 1) * sn_ref[...]


def _ret_head(q, k, v, g, dec_ref, qd_ref, kd_ref, cd_ref, c, state_in, state_out,
              q_s, kd_s, v_s, qr_s):
    r = q.shape[0]
    scores = _bdot_nt(q, k) * dec_ref[...]
    inner = _bdot(scores, v)
    q_s[...] = q
    kd_s[...] = k * kd_ref[...]
    v_s[...] = v
    cd = cd_ref[0:1, :]

    def unit(n, carry):
        r0 = pl.multiple_of(n * c, c)
        rows = pl.ds(r0, c)
        s = state_in(n, carry)
        qr_s[rows, :] = _bdot(q_s[rows, :], s)
        s_new = cd * s + _bdot_tn(kd_s[rows, :], v_s[rows, :])
        return state_out(n, s_new)

    carry = lax.fori_loop(0, r // c, unit, state_in(None, None))
    o = inner + qd_ref[...] * qr_s[...]
    return _rms(o) * _silu(g), carry


def _ret_prompt_kernel(q_ref, k_ref, v_ref, g_ref, cs_ref, sn_ref, dec_ref, qd_ref, kd_ref, cd_ref,
                       o_ref, st_ref, s_sc, q_s, kd_s, v_s, qr_s, *, hb, c, dk):
    j = pl.program_id(2)

    @pl.when(j == 0)
    def _():
        s_sc[...] = jnp.zeros_like(s_sc)

    for i in range(hb):
        cols = slice(i * dk, (i + 1) * dk)
        q = _rotary(q_ref[:, cols], cs_ref, sn_ref)
        k = _rotary(k_ref[:, cols], cs_ref, sn_ref) * (dk ** -0.5)

        def state_in(n, carry, i=i):
            return s_sc[i] if n is None else carry

        def state_out(n, s_new):
            return s_new

        o, s_fin = _ret_head(q, k, v_ref[:, cols], g_ref[:, cols], dec_ref.at[i], qd_ref.at[i], kd_ref.at[i],
                             cd_ref.at[i], c, state_in, state_out, q_s, kd_s, v_s, qr_s)
        s_sc[i] = s_fin
        st_ref[0, i] = s_fin
        o_ref[:, cols] = o.astype(o_ref.dtype)


def _ret_sample_kernel(q_ref, k_ref, v_ref, g_ref, cs_ref, sn_ref, dec_ref, qd_ref, kd_ref, cd_ref, sin_ref,
                       o_ref, st_ref, q_s, kd_s, v_s, qr_s, *, hb, c, dk):
    for i in range(hb):
        cols = slice(i * dk, (i + 1) * dk)
        q = _rotary(q_ref[:, cols], cs_ref, sn_ref)
        k = _rotary(k_ref[:, cols], cs_ref, sn_ref) * (dk ** -0.5)

        def state_in(n, carry, i=i):
            return 0 if n is None else sin_ref[n, i]

        def state_out(n, s_new, i=i):
            st_ref[n, i] = s_new
            return 0

        o, _ = _ret_head(q, k, v_ref[:, cols], g_ref[:, cols], dec_ref.at[i], qd_ref.at[i], kd_ref.at[i],
                         cd_ref.at[i], c, state_in, state_out, q_s, kd_s, v_s, qr_s)
        o_ref[:, cols] = o.astype(o_ref.dtype)


def _ret_tables(nh, r, c, dk):
    log_gamma = jnp.log1p(-jnp.exp2(-5.0 - jnp.arange(nh, dtype=F32)))
    ti = jnp.arange(r)
    same = (ti[:, None] // c) == (ti[None, :] // c)
    rel = (ti[:, None] - ti[None, :]).astype(F32)
    dec = jnp.where(same & (rel >= 0), jnp.exp(jnp.maximum(rel, 0.0) * log_gamma[:, None, None]), 0.0)
    pos = (ti % c).astype(F32)
    qd = jnp.exp((pos + 1.0) * log_gamma[:, None])
    kd = jnp.exp((c - 1.0 - pos) * log_gamma[:, None])
    cd = jnp.exp(c * log_gamma)
    bc = lambda a: jnp.broadcast_to(a[..., None], a.shape + (dk,))
    return dec, bc(qd), bc(kd), jnp.broadcast_to(cd[:, None, None], (nh, SUBLANES, dk))


def _rope_tables(pos, dk):
    half = dk // 2
    inv = ROPE_BASE ** (-jnp.arange(half, dtype=F32) / half)
    ang = pos.astype(F32)[:, None] * inv[None, :]
    cos, sin = jnp.cos(ang), jnp.sin(ang)
    return jnp.concatenate([cos, cos], -1), jnp.concatenate([-sin, sin], -1)


def _ret_specs(npair, row_map, pos_map, tn, r, hb, dk):
    def seg(n):
        return pl.BlockSpec((None, r, tn), lambda *g: (n * npair + g[1], row_map(*g), 0))

    def tab(shape):
        return pl.BlockSpec((hb,) + shape, lambda *g: (g[1],) + (0,) * len(shape))

    return [seg(0), seg(1), seg(2), seg(3),
            pl.BlockSpec((r, dk), lambda *g: (pos_map(*g), 0)),
            pl.BlockSpec((r, dk), lambda *g: (pos_map(*g), 0)),
            tab((r, r)), tab((r, dk)), tab((r, dk)), tab((SUBLANES, dk))]


def _ret_scratch(r, dk, dv):
    return [pltpu.VMEM((r, dk), F32), pltpu.VMEM((r, dk), F32), pltpu.VMEM((r, dv), F32),
            pltpu.VMEM((r, dv), F32)]


def _ret_prompt(p_ret, cs, sn, *, nb, seq, nh, dk, dv, row0):
    tn = p_ret.shape[-1]
    hb = tn // dk
    r = min(ROW_TILE, seq)
    c = math.gcd(seq, CHUNK)
    npair = nh // hb
    nt = seq // r
    t0 = row0 // r
    dec, qd, kd, cd = _ret_tables(nh, r, c, dk)
    specs = _ret_specs(npair, lambda b, p, j: t0 + b * nt + j, lambda b, p, j: j, tn, r, hb, dk)
    return pl.pallas_call(
        functools.partial(_ret_prompt_kernel, hb=hb, c=c, dk=dk),
        grid=(nb, npair, nt),
        in_specs=specs,
        out_specs=[pl.BlockSpec((r, tn), lambda b, p, j: (b * nt + j, p)),
                   pl.BlockSpec((1, hb, dk, dv), lambda b, p, j: (b, p, 0, 0))],
        out_shape=[jax.ShapeDtypeStruct((nb * seq, nh * dv), BF16),
                   jax.ShapeDtypeStruct((nb, nh, dk, dv), F32)],
        scratch_shapes=[pltpu.VMEM((hb, dk, dv), F32)] + _ret_scratch(r, dk, dv),
        compiler_params=_cparams(("arbitrary", "arbitrary", "arbitrary")),
        name="ret_prompt",
    )(p_ret, p_ret, p_ret, p_ret, cs, sn, dec, qd, kd, cd)


def _ret_sample(p_ret, cs, sn, state, layer, *, nb, seq, nh, dk, dv, row0):
    tn = p_ret.shape[-1]
    hb = tn // dk
    c = seq
    r = min(ROW_TILE, nb * seq)
    npair = nh // hb
    nt = nb * seq // r
    spt = r // seq
    t0 = row0 // r
    dec, qd, kd, cd = _ret_tables(nh, r, c, dk)
    specs = _ret_specs(npair, lambda i, p: t0 + i, lambda i, p: 0, tn, r, hb, dk)
    specs += [pl.BlockSpec((None, spt, hb, dk, dv), lambda i, p: (layer, i, p, 0, 0))]
    return pl.pallas_call(
        functools.partial(_ret_sample_kernel, hb=hb, c=c, dk=dk),
        grid=(nt, npair),
        in_specs=specs,
        out_specs=[pl.BlockSpec((r, tn), lambda i, p: (i, p)),
                   pl.BlockSpec((spt, hb, dk, dv), lambda i, p: (i, p, 0, 0))],
        out_shape=[jax.ShapeDtypeStruct((nb * seq, nh * dv), BF16),
                   jax.ShapeDtypeStruct((nb, nh, dk, dv), F32)],
        scratch_shapes=_ret_scratch(r, dk, dv),
        compiler_params=_cparams(("arbitrary", "arbitrary")),
        name="ret_sample",
    )(p_ret, p_ret, p_ret, p_ret, cs, sn, dec, qd, kd, cd, state)


S5_COL_CHUNK = 1024


def _s5_project_in(u_ref, bblk_ref, bur_s, bui_s):
    nblk = bblk_ref.shape[0]
    half = bblk_ref.shape[2] // 2
    for i in range(nblk):
        res = _hdot(u_ref[:, i * LANES:(i + 1) * LANES], bblk_ref[i])
        bur_s[:, i * half:(i + 1) * half] = res[:, :half]
        bui_s[:, i * half:(i + 1) * half] = res[:, half:]


def _s5_scan_rows(bur_s, bui_s, lam_ref, cols, row_start, nrows, hr, hi):
    lr = lam_ref[0:1, cols]
    li = lam_ref[1:2, cols]

    def step(t, carry):
        hr, hi = carry
        row = pl.ds(row_start + t, 1)
        nr = lr * hr - li * hi + bur_s[row, cols]
        ni = lr * hi + li * hr + bui_s[row, cols]
        bur_s[row, cols] = nr
        bui_s[row, cols] = ni
        return nr, ni

    return lax.fori_loop(0, nrows, step, (hr, hi))


def _s5_project_out(u_ref, bur_s, bui_s, cblk_ref, d_ref, wglu_ref, bglu_ref, o_ref):
    nblk = cblk_ref.shape[0]
    half = cblk_ref.shape[1] // 2
    ys = []
    for i in range(nblk):
        cb = cblk_ref[i]
        y = _bdot(bur_s[:, i * half:(i + 1) * half], cb[:half]) + _bdot(bui_s[:, i * half:(i + 1) * half], cb[half:])
        ys.append(y)
    y = jnp.concatenate(ys, axis=1) + d_ref[...] * u_ref[...]
    z = jax.nn.gelu(y)
    z = z * jax.nn.sigmoid(_bdot(z, wglu_ref[...]) + bglu_ref[...])
    o_ref[...] = z.astype(o_ref.dtype)


def _s5_prompt_kernel(u_ref, bblk_ref, cblk_ref, lam_ref, d_ref, wglu_ref, bglu_ref,
                      o_ref, hr_ref, hi_ref, bur_s, bui_s, hc_s):
    j = pl.program_id(1)
    r = u_ref.shape[0]
    ns = bur_s.shape[1]

    @pl.when(j == 0)
    def _():
        hc_s[...] = jnp.zeros_like(hc_s)

    _s5_project_in(u_ref, bblk_ref, bur_s, bui_s)
    cw = min(S5_COL_CHUNK, ns)
    for cc in range(ns // cw):
        cols = slice(cc * cw, (cc + 1) * cw)
        hr, hi = _s5_scan_rows(bur_s, bui_s, lam_ref, cols, 0, r, hc_s[0:1, cols], hc_s[1:2, cols])
        hc_s[0:1, cols] = hr
        hc_s[1:2, cols] = hi
    hr_ref[0] = hc_s[0:1, :]
    hi_ref[0] = hc_s[1:2, :]
    _s5_project_out(u_ref, bur_s, bui_s, cblk_ref, d_ref, wglu_ref, bglu_ref, o_ref)


def _s5_sample_kernel(u_ref, bblk_ref, cblk_ref, lam_ref, d_ref, wglu_ref, bglu_ref, h0r_ref, h0i_ref,
                      o_ref, hr_ref, hi_ref, bur_s, bui_s, *, seq):
    r = u_ref.shape[0]
    ns = bur_s.shape[1]
    _s5_project_in(u_ref, bblk_ref, bur_s, bui_s)
    cw = min(S5_COL_CHUNK, ns)
    for cc in range(ns // cw):
        cols = slice(cc * cw, (cc + 1) * cw)

        def per_seq(b, carry, cols=cols):
            hr, hi = _s5_scan_rows(bur_s, bui_s, lam_ref, cols, b * seq, seq,
                                   h0r_ref[pl.ds(b, 1), cols], h0i_ref[pl.ds(b, 1), cols])
            hr_ref[pl.ds(b, 1), cols] = hr
            hi_ref[pl.ds(b, 1), cols] = hi
            return carry

        lax.fori_loop(0, r // seq, per_seq, 0)
    _s5_project_out(u_ref, bur_s, bui_s, cblk_ref, d_ref, wglu_ref, bglu_ref, o_ref)


def _s5_params(lam_re, lam_im, log_step, b_re, b_im, c_re, c_im):
    g, p = lam_re.shape
    gs = b_re.shape[-1]
    dt = jnp.exp(log_step.astype(F32))[:, None]
    ar, ai = lam_re.astype(F32), lam_im.astype(F32)
    mag = jnp.exp(ar * dt)
    lbr, lbi = mag * jnp.cos(ai * dt), mag * jnp.sin(ai * dt)
    den = ar * ar + ai * ai
    nr, ni = lbr - 1.0, lbi
    cr = (nr * ar + ni * ai) / den
    ci = (ni * ar - nr * ai) / den
    bbr = cr[..., None] * b_re - ci[..., None] * b_im
    bbi = cr[..., None] * b_im + ci[..., None] * b_re
    gb = S5_GROUP_BLOCK
    nblk = g // gb
    eye = jnp.eye(gb, dtype=F32)
    bb = jnp.stack([bbr, bbi]).reshape(2, nblk, gb, p, gs)
    bblk = jnp.einsum('qigpc,gh->igcqhp', bb, eye).reshape(nblk, gb * gs, 2 * gb * p)
    cc = jnp.stack([c_re.astype(F32), -c_im.astype(F32)]).reshape(2, nblk, gb, gs, p)
    cblk = jnp.einsum('qigcp,gh->iqgphc', cc, eye).reshape(nblk, 2 * gb * p, gb * gs)
    lam = jnp.stack([lbr.reshape(-1), lbi.reshape(-1)])
    lam = jnp.concatenate([lam, jnp.zeros((SUBLANES - 2, g * p), F32)], axis=0)
    return bblk, cblk, lam


def _s5_const_specs(bblk, cblk, lam, w, layer, nmap):
    z = lambda n: (lambda *g: (0,) * n)
    return [pl.BlockSpec(bblk.shape, z(3)), pl.BlockSpec(cblk.shape, z(3)), pl.BlockSpec(lam.shape, z(2)),
            pl.BlockSpec((None, 1, w), lambda *g: (layer, 0, 0)),
            pl.BlockSpec((None, w, w), lambda *g: (layer, 0, 0)),
            pl.BlockSpec((None, 1, w), lambda *g: (layer, 0, 0))]


def _s5_prompt(u, bblk, cblk, lam, d, wglu, bglu, layer, *, nb, seq, row0):
    w = u.shape[1]
    ns = lam.shape[1]
    r = min(ROW_TILE, seq)
    nt = seq // r
    t0 = row0 // r
    return pl.pallas_call(
        _s5_prompt_kernel,
        grid=(nb, nt),
        in_specs=[pl.BlockSpec((r, w), lambda b, j: (t0 + b * nt + j, 0))]
        + _s5_const_specs(bblk, cblk, lam, w, layer, 2),
        out_specs=[pl.BlockSpec((r, w), lambda b, j: (b * nt + j, 0)),
                   pl.BlockSpec((1, 1, ns), lambda b, j: (b, 0, 0)),
                   pl.BlockSpec((1, 1, ns), lambda b, j: (b, 0, 0))],
        out_shape=[jax.ShapeDtypeStruct((nb * seq, w), BF16),
                   jax.ShapeDtypeStruct((nb, 1, ns), F32), jax.ShapeDtypeStruct((nb, 1, ns), F32)],
        scratch_shapes=[pltpu.VMEM((r, ns), F32), pltpu.VMEM((r, ns), F32), pltpu.VMEM((SUBLANES, ns), F32)],
        compiler_params=_cparams(("arbitrary", "arbitrary")),
        name="s5_prompt",
    )(u, bblk, cblk, lam, d, wglu, bglu)


def _s5_sample(u, bblk, cblk, lam, d, wglu, bglu, h0r, h0i, layer, *, nb, seq, row0):
    w = u.shape[1]
    ns = lam.shape[1]
    r = min(ROW_TILE, nb * seq)
    nt = nb * seq // r
    spt = r // seq
    t0 = row0 // r
    return pl.pallas_call(
        functools.partial(_s5_sample_kernel, seq=seq),
        grid=(nt,),
        in_specs=[pl.BlockSpec((r, w), lambda i: (t0 + i, 0))]
        + _s5_const_specs(bblk, cblk, lam, w, layer, 1)
        + [pl.BlockSpec((spt, ns), lambda i: (i, 0)), pl.BlockSpec((spt, ns), lambda i: (i, 0))],
        out_specs=[pl.BlockSpec((r, w), lambda i: (i, 0)),
                   pl.BlockSpec((spt, ns), lambda i: (i, 0)),
                   pl.BlockSpec((spt, ns), lambda i: (i, 0))],
        out_shape=[jax.ShapeDtypeStruct((nb * seq, w), BF16),
                   jax.ShapeDtypeStruct((nb, ns), F32), jax.ShapeDtypeStruct((nb, ns), F32)],
        scratch_shapes=[pltpu.VMEM((r, ns), F32), pltpu.VMEM((r, ns), F32)],
        compiler_params=_cparams(("arbitrary",)),
        name="s5_sample",
    )(u, bblk, cblk, lam, d, wglu, bglu, h0r, h0i)


def _route(probs, n_experts):
    epg = n_experts // N_EXPERT_GROUPS
    grouped = probs.reshape(-1, N_EXPERT_GROUPS, epg)
    group_score = lax.top_k(grouped, TOP_K)[0].sum(-1)
    g_sel = jnp.argmax(group_score, axis=-1).astype(jnp.int32)
    in_group = jnp.einsum('tge,tg->te', grouped, jax.nn.one_hot(g_sel, N_EXPERT_GROUPS, dtype=F32))
    top_p, top_i = lax.top_k(in_group, TOP_K)
    gate = top_p / jnp.sum(top_p, -1, keepdims=True)
    return g_sel[:, None] * epg + top_i.astype(jnp.int32), gate


def _dispatch(expert_idx, n_experts, bm):
    t = expert_idx.shape[0]
    n_pairs = t * TOP_K
    flat_e = expert_idx.reshape(-1)
    flat_tok = jnp.repeat(jnp.arange(t, dtype=jnp.int32), TOP_K)
    order = jnp.argsort(flat_e)
    e_sorted = flat_e[order]
    counts = jnp.zeros((n_experts,), jnp.int32).at[flat_e].add(1)
    padded = (counts + bm - 1) // bm * bm
    start = jnp.cumsum(counts) - counts
    pstart = jnp.cumsum(padded) - padded
    dest = pstart[e_sorted] + jnp.arange(n_pairs, dtype=jnp.int32) - start[e_sorted]
    n_blocks = -(-(n_pairs + n_experts * (bm - 1)) // bm)
    n_slots = n_blocks * bm
    slot_tok = jnp.zeros((n_slots,), jnp.int32).at[dest].set(flat_tok[order])
    pair_slot = jnp.zeros((n_pairs,), jnp.int32).at[order].set(dest)
    block_exp = jnp.minimum(jnp.searchsorted(jnp.cumsum(padded), jnp.arange(n_blocks, dtype=jnp.int32) * bm,
                                             side='right'), n_experts - 1).astype(jnp.int32)
    n_used = (jnp.sum(padded) // bm).astype(jnp.int32).reshape(1)
    return slot_tok, pair_slot, block_exp, n_used, n_blocks


def _row_copy(src_hbm, dst, src_row, dst_row, sem):
    return pltpu.make_async_copy(src_hbm.at[pl.ds(src_row, 1)], dst.at[pl.ds(dst_row, 1)], sem)


def _gather_rows_kernel(tok_ref, x_hbm, o_ref, buf, sem):
    rows = buf.shape[0]
    base = pl.program_id(0) * rows

    def issue(i, c):
        _row_copy(x_hbm, buf, tok_ref[base + i], i, sem).start()
        return c

    lax.fori_loop(0, rows, issue, 0)

    def wait(i, c):
        _row_copy(x_hbm, buf, 0, i, sem).wait()
        return c

    lax.fori_loop(0, rows, wait, 0)
    o_ref[...] = buf[...].astype(o_ref.dtype)


def _gather_rows(x, slot_tok, bm):
    n_slots = slot_tok.shape[0]
    d = x.shape[1]
    rows = min(bm, 128)
    return pl.pallas_call(
        _gather_rows_kernel,
        grid_spec=pltpu.PrefetchScalarGridSpec(
            num_scalar_prefetch=1,
            grid=(n_slots // rows,),
            in_specs=[pl.BlockSpec(memory_space=pl.ANY)],
            out_specs=pl.BlockSpec((rows, d), lambda i, tok: (i, 0)),
            scratch_shapes=[pltpu.VMEM((rows, d), F32), pltpu.SemaphoreType.DMA(())]),
        out_shape=jax.ShapeDtypeStruct((n_slots, d), BF16),
        compiler_params=_cparams(("arbitrary",)),
        name="moe_gather",
    )(slot_tok, x)


def _moe_gate_up_kernel(be_ref, nu_ref, x_ref, wg_ref, wu_ref, o_ref):
    blk = pl.program_id(1)

    @pl.when(blk < nu_ref[0])
    def _():
        x = x_ref[...]
        g = jnp.dot(x, wg_ref[...].astype(BF16), preferred_element_type=F32)
        u = jnp.dot(x, wu_ref[...].astype(BF16), preferred_element_type=F32)
        o_ref[...] = (_silu(g) * u).astype(o_ref.dtype)

    @pl.when(blk >= nu_ref[0])
    def _():
        o_ref[...] = jnp.zeros_like(o_ref)


def _moe_down_kernel(be_ref, nu_ref, h_ref, wd_ref, o_ref):
    blk = pl.program_id(1)

    @pl.when(blk < nu_ref[0])
    def _():
        o_ref[...] = jnp.dot(h_ref[...], wd_ref[...].astype(BF16), preferred_element_type=F32)

    @pl.when(blk >= nu_ref[0])
    def _():
        o_ref[...] = jnp.zeros_like(o_ref)


def _moe_experts(xs, block_exp, n_used, w_gate, w_up, w_down, layer, bm):
    n_slots, d = xs.shape
    ff = w_gate.shape[-1]
    nblk = n_slots // bm
    tn = _pick(ff, MM_TN)
    w_in_spec = pl.BlockSpec((None, None, d, tn), lambda j, b, be, nu: (layer, be[b], 0, j))
    hm = pl.pallas_call(
        _moe_gate_up_kernel,
        grid_spec=pltpu.PrefetchScalarGridSpec(
            num_scalar_prefetch=2,
            grid=(ff // tn, nblk),
            in_specs=[pl.BlockSpec((bm, d), lambda j, b, be, nu: (b, 0)), w_in_spec, w_in_spec],
            out_specs=pl.BlockSpec((bm, tn), lambda j, b, be, nu: (b, j))),
        out_shape=jax.ShapeDtypeStruct((n_slots, ff), BF16),
        compiler_params=_cparams(("arbitrary", "arbitrary")),
        name="moe_gate_up",
    )(block_exp, n_used, xs, w_gate, w_up)
    tn2 = _pick(d, MM_TN)
    return pl.pallas_call(
        _moe_down_kernel,
        grid_spec=pltpu.PrefetchScalarGridSpec(
            num_scalar_prefetch=2,
            grid=(d // tn2, nblk),
            in_specs=[pl.BlockSpec((bm, ff), lambda j, b, be, nu: (b, 0)),
                      pl.BlockSpec((None, None, ff, tn2), lambda j, b, be, nu: (layer, be[b], 0, j))],
            out_specs=pl.BlockSpec((bm, tn2), lambda j, b, be, nu: (b, j))),
        out_shape=jax.ShapeDtypeStruct((n_slots, d), F32),
        compiler_params=_cparams(("arbitrary", "arbitrary")),
        name="moe_down",
    )(block_exp, n_used, hm, w_down)


def _combine_ln_kernel(slot_ref, x_ref, w_ref, g_ref, b_ref, yb_hbm, of_ref, ob_ref, buf, sem, *, alpha):
    tm = x_ref.shape[0]
    base = pl.program_id(0) * tm

    def issue(i, c):
        for k in range(TOP_K):
            _row_copy(yb_hbm, buf.at[k], slot_ref[(base + i) * TOP_K + k], i, sem).start()
        return c

    lax.fori_loop(0, tm, issue, 0)

    def wait(i, c):
        for k in range(TOP_K):
            _row_copy(yb_hbm, buf.at[k], 0, i, sem).wait()
        return c

    lax.fori_loop(0, tm, wait, 0)
    w = w_ref[...]
    f = w[:, 0:1] * buf[0] + w[:, 1:2] * buf[1]
    z = _ln_math(alpha * x_ref[...] + f, g_ref[...], b_ref[...])
    of_ref[...] = z
    ob_ref[...] = z.astype(BF16)


def _combine_ln(x, yb, pair_slot, gate, g, b, alpha, layer):
    t, d = x.shape
    tm = min(128, t)
    return pl.pallas_call(
        functools.partial(_combine_ln_kernel, alpha=alpha),
        grid_spec=pltpu.PrefetchScalarGridSpec(
            num_scalar_prefetch=1,
            grid=(t // tm,),
            in_specs=[pl.BlockSpec((tm, d), lambda i, s: (i, 0)),
                      pl.BlockSpec((tm, TOP_K), lambda i, s: (i, 0)),
                      pl.BlockSpec((None, 1, d), lambda i, s: (layer, 0, 0)),
                      pl.BlockSpec((None, 1, d), lambda i, s: (layer, 0, 0)),
                      pl.BlockSpec(memory_space=pl.ANY)],
            out_specs=[pl.BlockSpec((tm, d), lambda i, s: (i, 0)),
                       pl.BlockSpec((tm, d), lambda i, s: (i, 0))],
            scratch_shapes=[pltpu.VMEM((TOP_K, tm, d), F32), pltpu.SemaphoreType.DMA(())]),
        out_shape=[jax.ShapeDtypeStruct((t, d), F32), jax.ShapeDtypeStruct((t, d), BF16)],
        compiler_params=_cparams(("arbitrary",)),
        name="moe_combine_ln2",
    )(pair_slot, x, gate, g, b, yb)


def kernel(x_prompt, x_sample, state_delta, state_conv, state_s5, state_ret, ln_in_g, ln_in_b, w_in, dn_conv_w, dn_a_log, dn_dt_bias, dn_norm_w, s5_lam_re, s5_lam_im, s5_log_step, s5_b_re, s5_b_im, s5_c_re, s5_c_im, s5_d, s5_w_glu, s5_b_glu, w_up_dn, w_up_s5, w_up_ret, w_o, ln1_g, ln1_b, router_w, router_b, w_gate_e, w_up_e, w_down_e, ln2_g, ln2_b):
    bp, lp, d = x_prompt.shape
    bs, ls, _ = x_sample.shape
    depth = w_in.shape[0]
    nh, dk, dv = state_delta.shape[2:]
    qkv = dn_conv_w.shape[2]
    hist = state_conv.shape[2]
    dnw = nh * dv
    s5w = s5_d.shape[1]
    g5, p5 = s5_lam_re.shape[1:]
    rh, rdk, rdv = state_ret.shape[2:]
    rw = rh * rdv
    n_exp = router_w.shape[1]
    tp, ts = bp * lp, bs * ls
    t = tp + ts
    alpha = (2 * depth) ** 0.25
    assert ls == SUBLANES and qkv == 3 * dnw and dk == dv == rdk == rdv == LANES

    sizes = (qkv, dnw, nh, nh, s5w, rh * rdk, rh * rdk, rw, rw, 3 * d)
    offs = [0]
    for s in sizes:
        offs.append(offs[-1] + s)
    o_dn, o_b, o_s5, o_ret, o_gates = offs[0], offs[2], offs[4], offs[5], offs[9]

    xf, xb = _ln_in(x_prompt.reshape(tp, d), x_sample.reshape(ts, d), ln_in_g, ln_in_b)

    ep = -(-n_exp // LANES) * LANES
    rw_pad = jnp.pad(router_w.astype(F32), ((0, 0), (0, ep - n_exp)))
    rb_pad = jnp.pad(router_b.astype(F32).reshape(1, n_exp), ((0, 0), (0, ep - n_exp)), constant_values=-1e30)

    cs_p, sn_p = _rope_tables(jnp.arange(lp, dtype=jnp.int32), rdk)
    cs_s, sn_s = _rope_tables(PAST_LEN + jnp.arange(ls, dtype=jnp.int32), rdk)
    rs = min(ROW_TILE, ts)
    cs_s = jnp.tile(cs_s, (rs // ls, 1))
    sn_s = jnp.tile(sn_s, (rs // ls, 1))

    new_delta_p, new_conv_p, new_s5_p, new_ret_p = [], [], [], []
    new_delta_s, new_conv_s, new_s5_s, new_ret_s = [], [], [], []

    for l in range(depth):
        wl = w_in[l]
        w_dn = wl[:, o_dn:o_dn + qkv + dnw].astype(BF16)
        w_ba = jnp.pad(wl[:, o_b:o_b + 2 * nh], ((0, 0), (0, LANES - 2 * nh))).astype(BF16)
        w_s5 = wl[:, o_s5:o_s5 + s5w].astype(BF16)
        w_ret = wl[:, o_ret:o_ret + 4 * rw].astype(BF16)
        w_gt = wl[:, o_gates:o_gates + 3 * d].astype(BF16)
        p_dn = _matmul(xb, w_dn, tile_major=True, name="proj_dn")
        ba = _matmul(xb, w_ba, name="proj_ba")
        u5 = _matmul(xb, w_s5, name="proj_s5")
        p_ret = _matmul(xb, w_ret, tile_major=True, name="proj_ret")
        gates = _matmul(xb, w_gt, name="proj_gates")

        tn = p_dn.shape[-1]
        bat = ba[:, :2 * nh].T
        zpad = jnp.zeros((nh,), F32)
        hpr = jnp.pad(jnp.stack([jnp.concatenate([zpad, dn_a_log[l].astype(F32)]),
                                 jnp.concatenate([zpad, dn_dt_bias[l].astype(F32)])]),
                      ((0, 0), (0, LANES - 2 * nh)))
        hpc = hpr[:, :2 * nh].T
        norm_w = dn_norm_w.reshape(depth, 1, dv)

        odn_p, dlt_p = _delta_prompt(p_dn, ba, bat, hpr, hpc, dn_conv_w, norm_w, l,
                                     nb=bp, seq=lp, nh=nh, dk=dk, dv=dv, row0=0)
        bufx = jnp.pad(state_conv[l], ((0, 0), (0, ls - hist), (0, 0))).reshape(ts, qkv)
        odn_s, dlt_s = _delta_sample(p_dn, ba, bat, hpr, hpc, dn_conv_w, norm_w, bufx, state_delta, l,
                                     nb=bs, seq=ls, nh=nh, dk=dk, dv=dv, row0=tp, hist=hist)
        o_dn_all = jnp.concatenate([odn_p, odn_s], axis=0)
        nq = qkv // tn
        tail = jnp.arange(hist, dtype=jnp.int32)
        idx_p = (jnp.arange(bp, dtype=jnp.int32)[:, None] * lp + (lp - hist) + tail[None, :]).reshape(-1)
        idx_s = (tp + jnp.arange(bs, dtype=jnp.int32)[:, None] * ls + (ls - hist) + tail[None, :]).reshape(-1)
        cp = jnp.take(p_dn, idx_p, axis=1)[:nq]
        cs_ = jnp.take(p_dn, idx_s, axis=1)[:nq]
        new_conv_p.append(jnp.moveaxis(cp, 0, 1).reshape(bp, hist, qkv))
        new_conv_s.append(jnp.moveaxis(cs_, 0, 1).reshape(bs, hist, qkv))
        new_delta_p.append(dlt_p)
        new_delta_s.append(dlt_s)

        bblk, cblk, lam = _s5_params(s5_lam_re[l], s5_lam_im[l], s5_log_step[l], s5_b_re[l], s5_b_im[l],
                                     s5_c_re[l], s5_c_im[l])
        d5 = s5_d.reshape(depth, 1, s5w)
        bg5 = s5_b_glu.reshape(depth, 1, s5w)
        os5_p, hr_p, hi_p = _s5_prompt(u5, bblk, cblk, lam, d5, s5_w_glu, bg5, l, nb=bp, seq=lp, row0=0)
        h0 = state_s5[l].astype(F32)
        os5_s, hr_s, hi_s = _s5_sample(u5, bblk, cblk, lam, d5, s5_w_glu, bg5,
                                       h0[..., 0].reshape(bs, g5 * p5), h0[..., 1].reshape(bs, g5 * p5), l,
                                       nb=bs, seq=ls, row0=tp)
        o_s5_all = jnp.concatenate([os5_p, os5_s], axis=0)
        new_s5_p.append(jnp.stack([hr_p.reshape(bp, g5, p5), hi_p.reshape(bp, g5, p5)], -1))
        new_s5_s.append(jnp.stack([hr_s.reshape(bs, g5, p5), hi_s.reshape(bs, g5, p5)], -1))

        oret_p, rt_p = _ret_prompt(p_ret, cs_p, sn_p, nb=bp, seq=lp, nh=rh, dk=rdk, dv=rdv, row0=0)
        oret_s, rt_s = _ret_sample(p_ret, cs_s, sn_s, state_ret, l, nb=bs, seq=ls, nh=rh, dk=rdk, dv=rdv,
                                   row0=tp)
        o_ret_all = jnp.concatenate([oret_p, oret_s], axis=0)
        new_ret_p.append(rt_p)
        new_ret_s.append(rt_s)

        merged = _upmerge(o_dn_all, o_s5_all, o_ret_all, gates, w_up_dn, w_up_s5, w_up_ret, l)
        mix = _matmul(merged, w_o, layer=l, name="w_o")
        xf, xb, probs = _ln1_router(xf, mix, ln1_g.reshape(depth, 1, d), ln1_b.reshape(depth, 1, d),
                                    rw_pad, rb_pad, alpha, l)

        expert_idx, gate = _route(probs[:, :n_exp], n_exp)
        slot_tok, pair_slot, block_exp, n_used, _ = _dispatch(expert_idx, n_exp, MOE_BM)
        xs = _gather_rows(xf, slot_tok, MOE_BM)
        yb = _moe_experts(xs, block_exp, n_used, w_gate_e, w_up_e, w_down_e, l, MOE_BM)
        xf, xb = _combine_ln(xf, yb, pair_slot, gate, ln2_g.reshape(depth, 1, d), ln2_b.reshape(depth, 1, d),
                             alpha, l)

    y_prompt = xf[:tp].reshape(bp, lp, d)
    y_sample = xf[tp:].reshape(bs, ls, d)
    st = lambda xs_, ref: jnp.stack(xs_).astype(ref.dtype)
    return (y_prompt, y_sample,
            st(new_delta_p, state_delta), st(new_conv_p, state_conv), st(new_s5_p, state_s5),
            st(new_ret_p, state_ret),
            st(new_delta_s, state_delta), st(new_conv_s, state_conv), st(new_s5_s, state_s5),
            st(new_ret_s, state_ret))
```

```python
import functools
import math

import jax
import jax.numpy as jnp
from jax import lax
from jax.experimental import pallas as pl
from jax.experimental.pallas import tpu as pltpu

F32 = jnp.float32
BF16 = jnp.bfloat16
HI = lax.Precision.HIGHEST

LANES = 128
SUBLANES = 8
VMEM_LIMIT = 56 * 1024 * 1024

LN_EPS = 1e-5
RMS_EPS = 1e-6
ROPE_BASE = 10000.0
PAST_LEN = 16384
CHUNK = 64
N_EXPERT_GROUPS = 4
TOP_K = 2
S5_GROUP_BLOCK = 8

ROW_TILE = 256
MM_TM = 1024
MM_TN = 256
MOE_BM = 256
MOE_TN_UP = 512
MOE_TN_DOWN = 1024


def _cparams(sem):
    return pltpu.CompilerParams(dimension_semantics=sem, vmem_limit_bytes=VMEM_LIMIT)


def _bdot(a, b):
    return jnp.dot(a.astype(BF16), b.astype(BF16), preferred_element_type=F32)


def _bdot_nt(a, b):
    return lax.dot_general(a.astype(BF16), b.astype(BF16), (((1,), (1,)), ((), ())),
                           preferred_element_type=F32)


def _bdot_tn(a, b):
    return lax.dot_general(a.astype(BF16), b.astype(BF16), (((0,), (0,)), ((), ())),
                           preferred_element_type=F32)


def _hdot(a, b):
    return jnp.dot(a, b, precision=HI, preferred_element_type=F32)


def _ln_math(x, g, b):
    mu = jnp.mean(x, -1, keepdims=True)
    xc = x - mu
    var = jnp.mean(xc * xc, -1, keepdims=True)
    return xc * lax.rsqrt(var + LN_EPS) * g + b


def _ln_in_kernel(xp_ref, xs_ref, g_ref, b_ref, of_ref, ob_ref, *, n_prompt_tiles):
    i = pl.program_id(0)

    def run(src):
        y = _ln_math(src[...], g_ref[...], b_ref[...])
        of_ref[...] = y
        ob_ref[...] = y.astype(BF16)

    @pl.when(i < n_prompt_tiles)
    def _():
        run(xp_ref)

    @pl.when(i >= n_prompt_tiles)
    def _():
        run(xs_ref)


def _ln_in(xp, xs, g, b):
    tp, d = xp.shape
    ts = xs.shape[0]
    tm = ROW_TILE
    npt, nst = tp // tm, ts // tm
    t = tp + ts
    return pl.pallas_call(
        functools.partial(_ln_in_kernel, n_prompt_tiles=npt),
        grid=(npt + nst,),
        in_specs=[pl.BlockSpec((tm, d), lambda i: (jnp.minimum(i, npt - 1), 0)),
                  pl.BlockSpec((tm, d), lambda i: (jnp.maximum(i - npt, 0), 0)),
                  pl.BlockSpec((1, d), lambda i: (0, 0)),
                  pl.BlockSpec((1, d), lambda i: (0, 0))],
        out_specs=[pl.BlockSpec((tm, d), lambda i: (i, 0)),
                   pl.BlockSpec((tm, d), lambda i: (i, 0))],
        out_shape=[jax.ShapeDtypeStruct((t, d), F32), jax.ShapeDtypeStruct((t, d), BF16)],
        compiler_params=_cparams(("arbitrary",)),
        name="ln_in",
    )(xp, xs, g.reshape(1, d), b.reshape(1, d))


def _ln1_router_kernel(x_ref, y_ref, g_ref, b_ref, rw_ref, rb_ref, of_ref, ob_ref, pr_ref, *, alpha):
    z = _ln_math(alpha * x_ref[...] + y_ref[...], g_ref[...], b_ref[...])
    of_ref[...] = z
    ob_ref[...] = z.astype(BF16)
    logits = _hdot(z, rw_ref[...]) + rb_ref[...]
    m = jnp.max(logits, -1, keepdims=True)
    e = jnp.exp(logits - m)
    pr_ref[...] = e / jnp.sum(e, -1, keepdims=True)


def _ln1_router(x, y, g, b, rw, rb, alpha, layer):
    t, d = x.shape
    tm = ROW_TILE
    ep = rw.shape[1]
    return pl.pallas_call(
        functools.partial(_ln1_router_kernel, alpha=alpha),
        grid=(t // tm,),
        in_specs=[pl.BlockSpec((tm, d), lambda i: (i, 0)),
                  pl.BlockSpec((tm, d), lambda i: (i, 0)),
                  pl.BlockSpec((None, 1, d), lambda i: (layer, 0, 0)),
                  pl.BlockSpec((None, 1, d), lambda i: (layer, 0, 0)),
                  pl.BlockSpec((d, ep), lambda i: (0, 0)),
                  pl.BlockSpec((1, ep), lambda i: (0, 0))],
        out_specs=[pl.BlockSpec((tm, d), lambda i: (i, 0)),
                   pl.BlockSpec((tm, d), lambda i: (i, 0)),
                   pl.BlockSpec((tm, ep), lambda i: (i, 0))],
        out_shape=[jax.ShapeDtypeStruct((t, d), F32), jax.ShapeDtypeStruct((t, d), BF16),
                   jax.ShapeDtypeStruct((t, ep), F32)],
        compiler_params=_cparams(("arbitrary",)),
        name="ln1_router",
    )(x, y, g, b, rw, rb)


def _mm_kernel(a_ref, w_ref, o_ref):
    o_ref[...] = jnp.dot(a_ref[...], w_ref[...].astype(BF16),
                         preferred_element_type=F32).astype(o_ref.dtype)


def _pick(n, pref):
    for c in (pref, 512, 256, 128):
        if c <= pref and n % c == 0:
            return c
    return n


def _matmul(a, w, *, layer=None, tile_major=False, out_dtype=F32, name="mm"):
    m, k = a.shape
    n = w.shape[-1]
    tm = _pick(m, MM_TM)
    tn = _pick(n, MM_TN)
    if w.ndim == 3:
        w_spec = pl.BlockSpec((None, k, tn), lambda i, j: (layer, 0, j))
    else:
        w_spec = pl.BlockSpec((k, tn), lambda i, j: (0, j))
    if tile_major:
        out_spec = pl.BlockSpec((None, tm, tn), lambda i, j: (j, i, 0))
        out_shape = jax.ShapeDtypeStruct((n // tn, m, tn), out_dtype)
    else:
        out_spec = pl.BlockSpec((tm, tn), lambda i, j: (i, j))
        out_shape = jax.ShapeDtypeStruct((m, n), out_dtype)
    return pl.pallas_call(
        _mm_kernel,
        grid=(m // tm, n // tn),
        in_specs=[pl.BlockSpec((tm, k), lambda i, j: (i, 0)), w_spec],
        out_specs=out_spec,
        out_shape=out_shape,
        compiler_params=_cparams(("arbitrary", "arbitrary")),
        name=name,
    )(a, w)


def _upmerge_kernel(odn, os5, oret, gdn, gs5, gret, wdn, ws5, wret, o_ref):
    def branch(o, w, g):
        return jax.nn.sigmoid(g[...]) * jnp.dot(o[...], w[...].astype(BF16), preferred_element_type=F32)

    o_ref[...] = (branch(odn, wdn, gdn) + branch(os5, ws5, gs5) + branch(oret, wret, gret)).astype(o_ref.dtype)


def _upmerge(o_dn, o_s5, o_ret, gates, w_dn, w_s5, w_ret, layer):
    t = o_dn.shape[0]
    d = w_dn.shape[-1]
    tm = _pick(t, MM_TM)
    tn = _pick(d, MM_TN)
    nj = d // tn

    def a_spec(a):
        return pl.BlockSpec((tm, a.shape[1]), lambda i, j: (i, 0))

    def g_spec(off):
        return pl.BlockSpec((tm, tn), lambda i, j: (i, off * nj + j))

    def w_spec(w):
        return pl.BlockSpec((None, w.shape[1], tn), lambda i, j: (layer, 0, j))

    return pl.pallas_call(
        _upmerge_kernel,
        grid=(t // tm, nj),
        in_specs=[a_spec(o_dn), a_spec(o_s5), a_spec(o_ret), g_spec(0), g_spec(1), g_spec(2),
                  w_spec(w_dn), w_spec(w_s5), w_spec(w_ret)],
        out_specs=pl.BlockSpec((tm, tn), lambda i, j: (i, j)),
        out_shape=jax.ShapeDtypeStruct((t, d), BF16),
        compiler_params=_cparams(("arbitrary", "arbitrary")),
        name="upmerge",
    )(o_dn, o_s5, o_ret, gates, gates, gates, w_dn, w_s5, w_ret)


def _chunk_masks(r, c):
    ti = lax.broadcasted_iota(jnp.int32, (r, r), 0)
    si = lax.broadcasted_iota(jnp.int32, (r, r), 1)
    same = (ti // c) == (si // c)
    return same, same & (si <= ti), same & (si < ti)


def _shift_rows_carry(x, prev8, s):
    rolled = pltpu.roll(x, s, 0)
    prev_rolled = pltpu.roll(prev8, s, 0)
    row8 = lax.broadcasted_iota(jnp.int32, prev8.shape, 0)
    first = jnp.where(row8 < s, prev_rolled, rolled[0:SUBLANES])
    return jnp.concatenate([first, rolled[SUBLANES:]], axis=0)


def _shift_rows_seq8(x, bufx, s, hist):
    r = x.shape[0]
    rolled = pltpu.roll(x, s, 0)
    brolled = pltpu.roll(bufx, (r - (hist - s)) % r, 0)
    t8 = lax.broadcasted_iota(jnp.int32, x.shape, 0) % SUBLANES
    return jnp.where(t8 < s, brolled, rolled)


def _causal_conv(x, w_ref, shift):
    n = w_ref.shape[0]
    y = w_ref[n - 1:n, :] * x
    for s in range(1, n):
        y = y + w_ref[n - 1 - s:n - s, :] * shift(s)
    return y


def _silu(x):
    return x * jax.nn.sigmoid(x)


def _l2norm(x):
    return x * lax.rsqrt(jnp.sum(x * x, -1, keepdims=True) + RMS_EPS)


def _rms(x):
    return x * lax.rsqrt(jnp.mean(x * x, -1, keepdims=True) + RMS_EPS)


def _split3(x):
    h = x.astype(BF16)
    r1 = x - h.astype(F32)
    m = r1.astype(BF16)
    l = (r1 - m.astype(F32)).astype(BF16)
    return h, m, l


def _delta_gate_kernel(ba_ref, bat_ref, hpr_ref, hpc_ref, beta_ref, gcum_ref, gtot_ref, rows_ref, *, nh, c):
    r = ba_ref.shape[0]
    same, incl, _ = _chunk_masks(r, c)
    ti = lax.broadcasted_iota(jnp.int32, (r, r), 0)
    si = lax.broadcasted_iota(jnp.int32, (r, r), 1)
    incl_t = (same & (ti <= si)).astype(BF16)
    inclb = incl.astype(BF16)
    sameb = same.astype(BF16)
    dot = functools.partial(jnp.dot, preferred_element_type=F32)
    ba = ba_ref[...]
    lane = lax.broadcasted_iota(jnp.int32, ba.shape, 1)
    beta_ref[...] = jax.nn.sigmoid(ba)
    g_cols = -jnp.exp(hpr_ref[0:1, :]) * jax.nn.softplus(ba + hpr_ref[1:2, :])
    g_cols = jnp.where((lane >= nh) & (lane < 2 * nh), g_cols, 0.0)
    h, m, l = _split3(g_cols)
    gcum_ref[...] = dot(inclb, h) + dot(inclb, m) + dot(inclb, l)
    gtot_ref[...] = dot(sameb, h) + dot(sameb, m) + dot(sameb, l)
    g_rows = -jnp.exp(hpc_ref[:, 0:1]) * jax.nn.softplus(bat_ref[...] + hpc_ref[:, 1:2])
    h, m, l = _split3(g_rows)
    rows_ref[...] = dot(h, incl_t) + dot(m, incl_t) + dot(l, incl_t)


def _delta_gate(ba, bat, hpr, hpc, *, nh, c, row0, nrows):
    r = min(ROW_TILE, nrows)
    t0 = row0 // r
    nh2 = 2 * nh
    col = pl.BlockSpec((r, LANES), lambda i: (i, 0))
    return pl.pallas_call(
        functools.partial(_delta_gate_kernel, nh=nh, c=c),
        grid=(nrows // r,),
        in_specs=[pl.BlockSpec((r, LANES), lambda i: (t0 + i, 0)),
                  pl.BlockSpec((nh2, r), lambda i: (0, t0 + i)),
                  pl.BlockSpec((2, LANES), lambda i: (0, 0)),
                  pl.BlockSpec((nh2, 2), lambda i: (0, 0))],
        out_specs=[col, col, col, pl.BlockSpec((nh2, r), lambda i: (0, i))],
        out_shape=[jax.ShapeDtypeStruct((nrows, LANES), F32)] * 3 + [jax.ShapeDtypeStruct((nh2, nrows), F32)],
        compiler_params=_cparams(("arbitrary",)),
        name="delta_gate",
    )(ba, bat, hpr, hpc)


def _pick_col(cols, idx):
    lane = lax.broadcasted_iota(jnp.int32, cols.shape, 1)
    return jnp.sum(jnp.where(lane == idx, cols, 0.0), -1, keepdims=True)


def _delta_prep(q, k, v, beta_c, gc_c, gt_c, gc_r, c):
    r = q.shape[0]
    dk = k.shape[1]
    _, incl, strict = _chunk_masks(r, c)
    dec = jnp.exp(jnp.minimum(gc_c - gc_r, 0.0))
    m = jnp.where(strict, beta_c * dec * _bdot_nt(k, k), 0.0)
    a = jnp.where(incl, dec * _bdot_nt(q, k), 0.0)
    eye = (lax.broadcasted_iota(jnp.int32, (r, r), 0) == lax.broadcasted_iota(jnp.int32, (r, r), 1)).astype(F32)
    x = eye - m
    pw = m
    for _ in range(int(math.log2(c)) - 1):
        pw = _bdot(pw, pw)
        x = x + _bdot(x, pw)
    gin = jnp.exp(gc_c)
    wu = _bdot(x, jnp.concatenate([beta_c * gin * k, beta_c * v], axis=1))
    return a, wu[:, :dk], wu[:, dk:], k * jnp.exp(gt_c - gc_c), jnp.exp(gt_c), gin


def _delta_head_inputs(qa, ka, va, beta_ref, gcum_ref, gtot_ref, rows_ref, i, h, nh, dk):
    cols = slice(i * dk, (i + 1) * dk)
    q = _l2norm(qa[:, cols]) * (dk ** -0.5)
    k = _l2norm(ka[:, cols])
    beta_c = _pick_col(beta_ref[...], h)
    gc_c = _pick_col(gcum_ref[...], nh + h)
    gt_c = _pick_col(gtot_ref[...], nh + h)
    gc_r = rows_ref[pl.ds(nh + h, 1), :]
    return q, k, va[:, cols], beta_c, gc_c, gt_c, gc_r


def _delta_finish(a, u, qs, z, nw_ref):
    o = qs + _bdot(a, u)
    return _rms(o) * nw_ref[...] * _silu(z)


def _delta_prompt_kernel(q_ref, k_ref, v_ref, z_ref, beta_ref, gcum_ref, gtot_ref, rows_ref,
                         cwq_ref, cwk_ref, cwv_ref, nw_ref,
                         o_ref, st_ref,
                         s_sc, pq_sc, pk_sc, pv_sc,
                         *, nh, hb, c, dk):
    p = pl.program_id(1)
    j = pl.program_id(2)
    r = q_ref.shape[0]

    @pl.when(j == 0)
    def _():
        s_sc[...] = jnp.zeros_like(s_sc)
        pq_sc[...] = jnp.zeros_like(pq_sc)
        pk_sc[...] = jnp.zeros_like(pk_sc)
        pv_sc[...] = jnp.zeros_like(pv_sc)

    def conv(x_ref, w_ref, prev_sc):
        x = x_ref[...]
        prev8 = prev_sc[...]
        y = _causal_conv(x, w_ref, lambda s: _shift_rows_carry(x, prev8, s))
        prev_sc[...] = x[r - SUBLANES:, :]
        return _silu(y)

    qa = conv(q_ref, cwq_ref, pq_sc)
    ka = conv(k_ref, cwk_ref, pk_sc)
    va = conv(v_ref, cwv_ref, pv_sc)
    z = z_ref[...]

    for i in range(hb):
        q, k, v, beta_c, gc_c, gt_c, gc_r = _delta_head_inputs(qa, ka, va, beta_ref, gcum_ref, gtot_ref,
                                                               rows_ref, i, p * hb + i, nh, dk)
        a, w, u0, kd, eg, gin = _delta_prep(q, k, v, beta_c, gc_c, gt_c, gc_r, c)
        s = s_sc[i]
        us, qss = [], []
        for n in range(r // c):
            rows = slice(n * c, (n + 1) * c)
            us.append(u0[rows] - _bdot(w[rows], s))
            qss.append(gin[rows] * _bdot(q[rows], s))
            s = eg[n * c:n * c + 1] * s - _bdot(_bdot_tn(kd[rows], w[rows]), s) + _bdot_tn(kd[rows], u0[rows])
        s_sc[i] = s
        st_ref[0, i] = s
        cols = slice(i * dk, (i + 1) * dk)
        o = _delta_finish(a, jnp.concatenate(us, axis=0), jnp.concatenate(qss, axis=0), z[:, cols], nw_ref)
        o_ref[:, cols] = o.astype(o_ref.dtype)


def _delta_sample_kernel(q_ref, k_ref, v_ref, z_ref, beta_ref, gcum_ref, gtot_ref, rows_ref,
                         cwq_ref, cwk_ref, cwv_ref, nw_ref, bq_ref, bk_ref, bv_ref, sin_ref,
                         o_ref, st_ref,
                         q_s, w_s, u0_s, kd_s, eg_s, gin_s, u_s, qs_s,
                         *, nh, hb, c, dk, hist):
    p = pl.program_id(1)
    r = q_ref.shape[0]

    def conv(x_ref, w_ref, b_ref):
        x = x_ref[...]
        bufx = b_ref[...]
        return _silu(_causal_conv(x, w_ref, lambda s: _shift_rows_seq8(x, bufx, s, hist)))

    qa = conv(q_ref, cwq_ref, bq_ref)
    ka = conv(k_ref, cwk_ref, bk_ref)
    va = conv(v_ref, cwv_ref, bv_ref)
    z = z_ref[...]

    for i in range(hb):
        q, k, v, beta_c, gc_c, gt_c, gc_r = _delta_head_inputs(qa, ka, va, beta_ref, gcum_ref, gtot_ref,
                                                               rows_ref, i, p * hb + i, nh, dk)
        a, w, u0, kd, eg, gin = _delta_prep(q, k, v, beta_c, gc_c, gt_c, gc_r, c)
        q_s[...] = q
        w_s[...] = w
        u0_s[...] = u0
        kd_s[...] = kd
        eg_s[...] = jnp.broadcast_to(eg, eg_s.shape)
        gin_s[...] = jnp.broadcast_to(gin, gin_s.shape)

        def unit(n, carry, i=i):
            r0 = pl.multiple_of(n * c, c)
            rows = pl.ds(r0, c)
            s = sin_ref[n, i]
            wq = _bdot(jnp.concatenate([w_s[rows, :], q_s[rows, :]], axis=0), s)
            u = u0_s[rows, :] - wq[:c]
            u_s[rows, :] = u
            qs_s[rows, :] = gin_s[rows, :] * wq[c:]
            st_ref[n, i] = eg_s[pl.ds(r0, 1), :] * s + _bdot_tn(kd_s[rows, :], u)
            return carry

        lax.fori_loop(0, r // c, unit, 0, unroll=4)
        cols = slice(i * dk, (i + 1) * dk)
        o_ref[:, cols] = _delta_finish(a, u_s[...], qs_s[...], z[:, cols], nw_ref).astype(o_ref.dtype)


def _delta_scratch(r, dk, dv):
    return [pltpu.VMEM((r, dk), F32),
            pltpu.VMEM((r, dk), F32),
            pltpu.VMEM((r, dv), F32),
            pltpu.VMEM((r, dk), F32),
            pltpu.VMEM((r, dv), F32),
            pltpu.VMEM((r, dv), F32),
            pltpu.VMEM((r, dv), F32),
            pltpu.VMEM((r, dv), F32)]


def _delta_common_specs(tiles_per_seg, row_map, local_map, taps, dv, layer, tn, r, nh2):
    def seg(n):
        return pl.BlockSpec((None, r, tn), lambda *g: (n * tiles_per_seg + g[1], row_map(*g), 0))

    def cw(n):
        return pl.BlockSpec((None, taps, tn), lambda *g: (layer, 0, n * tiles_per_seg + g[1]))

    col = pl.BlockSpec((r, LANES), lambda *g: (local_map(*g), 0))
    return [seg(0), seg(1), seg(2), seg(3), col, col, col,
            pl.BlockSpec((nh2, r), lambda *g: (0, local_map(*g))),
            cw(0), cw(1), cw(2),
            pl.BlockSpec((None, 1, dv), lambda *g: (layer, 0, 0))]


def _delta_prompt(p_dn, gates, conv_w, norm_w, layer, *, nb, seq, nh, dk, dv, row0):
    tn = p_dn.shape[-1]
    hb = tn // dk
    r = min(ROW_TILE, seq)
    c = math.gcd(seq, CHUNK)
    npair = nh // hb
    nt = seq // r
    t0 = row0 // r
    kern = functools.partial(_delta_prompt_kernel, nh=nh, hb=hb, c=c, dk=dk)
    specs = _delta_common_specs(npair, lambda b, p, j: t0 + b * nt + j, lambda b, p, j: b * nt + j,
                                conv_w.shape[1], dv, layer, tn, r, 2 * nh)
    return pl.pallas_call(
        kern,
        grid=(nb, npair, nt),
        in_specs=specs,
        out_specs=[pl.BlockSpec((r, tn), lambda b, p, j: (b * nt + j, p)),
                   pl.BlockSpec((1, hb, dk, dv), lambda b, p, j: (b, p, 0, 0))],
        out_shape=[jax.ShapeDtypeStruct((nb * seq, nh * dv), BF16),
                   jax.ShapeDtypeStruct((nb, nh, dk, dv), F32)],
        scratch_shapes=[pltpu.VMEM((hb, dk, dv), F32),
                        pltpu.VMEM((SUBLANES, tn), F32), pltpu.VMEM((SUBLANES, tn), F32),
                        pltpu.VMEM((SUBLANES, tn), F32)],
        compiler_params=_cparams(("arbitrary", "arbitrary", "arbitrary")),
        name="delta_prompt",
    )(p_dn, p_dn, p_dn, p_dn, *gates, conv_w, conv_w, conv_w, norm_w)


def _delta_sample(p_dn, gates, conv_w, norm_w, bufx, state, layer, *, nb, seq, nh, dk, dv, row0, hist):
    tn = p_dn.shape[-1]
    hb = tn // dk
    c = seq
    r = min(ROW_TILE, nb * seq)
    npair = nh // hb
    nt = nb * seq // r
    spt = r // seq
    t0 = row0 // r
    kern = functools.partial(_delta_sample_kernel, nh=nh, hb=hb, c=c, dk=dk, hist=hist)
    specs = _delta_common_specs(npair, lambda i, p: t0 + i, lambda i, p: i,
                                conv_w.shape[1], dv, layer, tn, r, 2 * nh)

    def buf(n):
        return pl.BlockSpec((r, tn), lambda i, p: (i, n * npair + p))

    specs += [buf(0), buf(1), buf(2),
              pl.BlockSpec((None, spt, hb, dk, dv), lambda i, p: (layer, i, p, 0, 0))]
    return pl.pallas_call(
        kern,
        grid=(nt, npair),
        in_specs=specs,
        out_specs=[pl.BlockSpec((r, tn), lambda i, p: (i, p)),
                   pl.BlockSpec((spt, hb, dk, dv), lambda i, p: (i, p, 0, 0))],
        out_shape=[jax.ShapeDtypeStruct((nb * seq, nh * dv), BF16),
                   jax.ShapeDtypeStruct((nb, nh, dk, dv), F32)],
        scratch_shapes=_delta_scratch(r, dk, dv),
        compiler_params=_cparams(("arbitrary", "arbitrary")),
        name="delta_sample",
    )(p_dn, p_dn, p_dn, p_dn, *gates, conv_w, conv_w, conv_w, norm_w, bufx, bufx, bufx, state)


def _rotary(x, cs_ref, sn_ref):
    half = x.shape[1] // 2
    return x * cs_ref[...] + pltpu.roll(x, half, 1) * sn_ref[...]


def _ret_head(q, k, v, g, dec_ref, qd_ref, kd_ref, cd_ref, c, state_in, state_out,
              q_s, kd_s, v_s, qr_s):
    r = q.shape[0]
    scores = _bdot_nt(q, k) * dec_ref[...]
    inner = _bdot(scores, v)
    q_s[...] = q
    kd_s[...] = k * kd_ref[...]
    v_s[...] = v
    cd = cd_ref[0:1, :]

    def unit(n, carry):
        r0 = pl.multiple_of(n * c, c)
        rows = pl.ds(r0, c)
        s = state_in(n, carry)
        qr_s[rows, :] = _bdot(q_s[rows, :], s)
        s_new = cd * s + _bdot_tn(kd_s[rows, :], v_s[rows, :])
        return state_out(n, s_new)

    carry = lax.fori_loop(0, r // c, unit, state_in(None, None))
    o = inner + qd_ref[...] * qr_s[...]
    return _rms(o) * _silu(g), carry


def _ret_prompt_kernel(q_ref, k_ref, v_ref, g_ref, cs_ref, sn_ref, dec_ref, qd_ref, kd_ref, cd_ref,
                       o_ref, st_ref, s_sc, q_s, kd_s, v_s, qr_s, *, hb, c, dk):
    j = pl.program_id(2)

    @pl.when(j == 0)
    def _():
        s_sc[...] = jnp.zeros_like(s_sc)

    for i in range(hb):
        cols = slice(i * dk, (i + 1) * dk)
        q = _rotary(q_ref[:, cols], cs_ref, sn_ref)
        k = _rotary(k_ref[:, cols], cs_ref, sn_ref) * (dk ** -0.5)

        def state_in(n, carry, i=i):
            return s_sc[i] if n is None else carry

        def state_out(n, s_new):
            return s_new

        o, s_fin = _ret_head(q, k, v_ref[:, cols], g_ref[:, cols], dec_ref.at[i], qd_ref.at[i], kd_ref.at[i],
                             cd_ref.at[i], c, state_in, state_out, q_s, kd_s, v_s, qr_s)
        s_sc[i] = s_fin
        st_ref[0, i] = s_fin
        o_ref[:, cols] = o.astype(o_ref.dtype)


def _ret_sample_kernel(q_ref, k_ref, v_ref, g_ref, cs_ref, sn_ref, dec_ref, qd_ref, kd_ref, cd_ref, sin_ref,
                       o_ref, st_ref, q_s, kd_s, v_s, qr_s, *, hb, c, dk):
    for i in range(hb):
        cols = slice(i * dk, (i + 1) * dk)
        q = _rotary(q_ref[:, cols], cs_ref, sn_ref)
        k = _rotary(k_ref[:, cols], cs_ref, sn_ref) * (dk ** -0.5)

        def state_in(n, carry, i=i):
            return 0 if n is None else sin_ref[n, i]

        def state_out(n, s_new, i=i):
            st_ref[n, i] = s_new
            return 0

        o, _ = _ret_head(q, k, v_ref[:, cols], g_ref[:, cols], dec_ref.at[i], qd_ref.at[i], kd_ref.at[i],
                         cd_ref.at[i], c, state_in, state_out, q_s, kd_s, v_s, qr_s)
        o_ref[:, cols] = o.astype(o_ref.dtype)


def _ret_tables(nh, r, c, dk):
    log_gamma = jnp.log1p(-jnp.exp2(-5.0 - jnp.arange(nh, dtype=F32)))
    ti = jnp.arange(r)
    same = (ti[:, None] // c) == (ti[None, :] // c)
    rel = (ti[:, None] - ti[None, :]).astype(F32)
    dec = jnp.where(same & (rel >= 0), jnp.exp(jnp.maximum(rel, 0.0) * log_gamma[:, None, None]), 0.0)
    pos = (ti % c).astype(F32)
    qd = jnp.exp((pos + 1.0) * log_gamma[:, None])
    kd = jnp.exp((c - 1.0 - pos) * log_gamma[:, None])
    cd = jnp.exp(c * log_gamma)
    bc = lambda a: jnp.broadcast_to(a[..., None], a.shape + (dk,))
    return dec, bc(qd), bc(kd), jnp.broadcast_to(cd[:, None, None], (nh, SUBLANES, dk))


def _rope_tables(pos, dk):
    half = dk // 2
    inv = ROPE_BASE ** (-jnp.arange(half, dtype=F32) / half)
    ang = pos.astype(F32)[:, None] * inv[None, :]
    cos, sin = jnp.cos(ang), jnp.sin(ang)
    return jnp.concatenate([cos, cos], -1), jnp.concatenate([-sin, sin], -1)


def _ret_specs(npair, row_map, pos_map, tn, r, hb, dk):
    def seg(n):
        return pl.BlockSpec((None, r, tn), lambda *g: (n * npair + g[1], row_map(*g), 0))

    def tab(shape):
        return pl.BlockSpec((hb,) + shape, lambda *g: (g[1],) + (0,) * len(shape))

    return [seg(0), seg(1), seg(2), seg(3),
            pl.BlockSpec((r, dk), lambda *g: (pos_map(*g), 0)),
            pl.BlockSpec((r, dk), lambda *g: (pos_map(*g), 0)),
            tab((r, r)), tab((r, dk)), tab((r, dk)), tab((SUBLANES, dk))]


def _ret_scratch(r, dk, dv):
    return [pltpu.VMEM((r, dk), F32), pltpu.VMEM((r, dk), F32), pltpu.VMEM((r, dv), F32),
            pltpu.VMEM((r, dv), F32)]


def _ret_prompt(p_ret, cs, sn, *, nb, seq, nh, dk, dv, row0):
    tn = p_ret.shape[-1]
    hb = tn // dk
    r = min(ROW_TILE, seq)
    c = math.gcd(seq, CHUNK)
    npair = nh // hb
    nt = seq // r
    t0 = row0 // r
    dec, qd, kd, cd = _ret_tables(nh, r, c, dk)
    specs = _ret_specs(npair, lambda b, p, j: t0 + b * nt + j, lambda b, p, j: j, tn, r, hb, dk)
    return pl.pallas_call(
        functools.partial(_ret_prompt_kernel, hb=hb, c=c, dk=dk),
        grid=(nb, npair, nt),
        in_specs=specs,
        out_specs=[pl.BlockSpec((r, tn), lambda b, p, j: (b * nt + j, p)),
                   pl.BlockSpec((1, hb, dk, dv), lambda b, p, j: (b, p, 0, 0))],
        out_shape=[jax.ShapeDtypeStruct((nb * seq, nh * dv), BF16),
                   jax.ShapeDtypeStruct((nb, nh, dk, dv), F32)],
        scratch_shapes=[pltpu.VMEM((hb, dk, dv), F32)] + _ret_scratch(r, dk, dv),
        compiler_params=_cparams(("arbitrary", "arbitrary", "arbitrary")),
        name="ret_prompt",
    )(p_ret, p_ret, p_ret, p_ret, cs, sn, dec, qd, kd, cd)


def _ret_sample(p_ret, cs, sn, state, layer, *, nb, seq, nh, dk, dv, row0):
    tn = p_ret.shape[-1]
    hb = tn // dk
    c = seq
    r = min(ROW_TILE, nb * seq)
    npair = nh // hb
    nt = nb * seq // r
    spt = r // seq
    t0 = row0 // r
    dec, qd, kd, cd = _ret_tables(nh, r, c, dk)
    specs = _ret_specs(npair, lambda i, p: t0 + i, lambda i, p: 0, tn, r, hb, dk)
    specs += [pl.BlockSpec((None, spt, hb, dk, dv), lambda i, p: (layer, i, p, 0, 0))]
    return pl.pallas_call(
        functools.partial(_ret_sample_kernel, hb=hb, c=c, dk=dk),
        grid=(nt, npair),
        in_specs=specs,
        out_specs=[pl.BlockSpec((r, tn), lambda i, p: (i, p)),
                   pl.BlockSpec((spt, hb, dk, dv), lambda i, p: (i, p, 0, 0))],
        out_shape=[jax.ShapeDtypeStruct((nb * seq, nh * dv), BF16),
                   jax.ShapeDtypeStruct((nb, nh, dk, dv), F32)],
        scratch_shapes=_ret_scratch(r, dk, dv),
        compiler_params=_cparams(("arbitrary", "arbitrary")),
        name="ret_sample",
    )(p_ret, p_ret, p_ret, p_ret, cs, sn, dec, qd, kd, cd, state)


S5_COL_CHUNK = 1024


def _s5_project_in(u_ref, bblk_ref, bur_s, bui_s):
    nblk = bblk_ref.shape[0]
    half = bblk_ref.shape[2] // 2
    for i in range(nblk):
        res = _hdot(u_ref[:, i * LANES:(i + 1) * LANES], bblk_ref[i])
        bur_s[:, i * half:(i + 1) * half] = res[:, :half]
        bui_s[:, i * half:(i + 1) * half] = res[:, half:]


def _s5_scan_rows(bur_s, bui_s, lam_ref, cols, row_start, nrows, hr, hi):
    lr = lam_ref[0:1, cols]
    li = lam_ref[1:2, cols]

    def step(t, carry):
        hr, hi = carry
        row = pl.ds(row_start + t, 1)
        nr = lr * hr - li * hi + bur_s[row, cols]
        ni = lr * hi + li * hr + bui_s[row, cols]
        bur_s[row, cols] = nr
        bui_s[row, cols] = ni
        return nr, ni

    return lax.fori_loop(0, nrows, step, (hr, hi))


def _s5_project_out(u_ref, bur_s, bui_s, cblk_ref, d_ref, wglu_ref, bglu_ref, o_ref):
    nblk = cblk_ref.shape[0]
    half = cblk_ref.shape[1] // 2
    ys = []
    for i in range(nblk):
        cb = cblk_ref[i]
        y = _bdot(bur_s[:, i * half:(i + 1) * half], cb[:half]) + _bdot(bui_s[:, i * half:(i + 1) * half], cb[half:])
        ys.append(y)
    y = jnp.concatenate(ys, axis=1) + d_ref[...] * u_ref[...]
    z = jax.nn.gelu(y)
    z = z * jax.nn.sigmoid(_bdot(z, wglu_ref[...]) + bglu_ref[...])
    o_ref[...] = z.astype(o_ref.dtype)


def _s5_prompt_kernel(u_ref, bblk_ref, cblk_ref, lam_ref, d_ref, wglu_ref, bglu_ref,
                      o_ref, hr_ref, hi_ref, bur_s, bui_s, hc_s):
    j = pl.program_id(1)
    r = u_ref.shape[0]
    ns = bur_s.shape[1]

    @pl.when(j == 0)
    def _():
        hc_s[...] = jnp.zeros_like(hc_s)

    _s5_project_in(u_ref, bblk_ref, bur_s, bui_s)
    cw = min(S5_COL_CHUNK, ns)
    for cc in range(ns // cw):
        cols = slice(cc * cw, (cc + 1) * cw)
        hr, hi = _s5_scan_rows(bur_s, bui_s, lam_ref, cols, 0, r, hc_s[0:1, cols], hc_s[1:2, cols])
        hc_s[0:1, cols] = hr
        hc_s[1:2, cols] = hi
    hr_ref[0] = hc_s[0:1, :]
    hi_ref[0] = hc_s[1:2, :]
    _s5_project_out(u_ref, bur_s, bui_s, cblk_ref, d_ref, wglu_ref, bglu_ref, o_ref)


def _s5_sample_kernel(u_ref, bblk_ref, cblk_ref, lam_ref, d_ref, wglu_ref, bglu_ref, h0r_ref, h0i_ref,
                      o_ref, hr_ref, hi_ref, bur_s, bui_s, *, seq):
    r = u_ref.shape[0]
    ns = bur_s.shape[1]
    _s5_project_in(u_ref, bblk_ref, bur_s, bui_s)
    cw = min(S5_COL_CHUNK, ns)
    for cc in range(ns // cw):
        cols = slice(cc * cw, (cc + 1) * cw)

        def per_seq(b, carry, cols=cols):
            hr, hi = _s5_scan_rows(bur_s, bui_s, lam_ref, cols, b * seq, seq,
                                   h0r_ref[pl.ds(b, 1), cols], h0i_ref[pl.ds(b, 1), cols])
            hr_ref[pl.ds(b, 1), cols] = hr
            hi_ref[pl.ds(b, 1), cols] = hi
            return carry

        lax.fori_loop(0, r // seq, per_seq, 0)
    _s5_project_out(u_ref, bur_s, bui_s, cblk_ref, d_ref, wglu_ref, bglu_ref, o_ref)


def _s5_params(lam_re, lam_im, log_step, b_re, b_im, c_re, c_im):
    g, p = lam_re.shape
    gs = b_re.shape[-1]
    dt = jnp.exp(log_step.astype(F32))[:, None]
    ar, ai = lam_re.astype(F32), lam_im.astype(F32)
    mag = jnp.exp(ar * dt)
    lbr, lbi = mag * jnp.cos(ai * dt), mag * jnp.sin(ai * dt)
    den = ar * ar + ai * ai
    nr, ni = lbr - 1.0, lbi
    cr = (nr * ar + ni * ai) / den
    ci = (ni * ar - nr * ai) / den
    bbr = cr[..., None] * b_re - ci[..., None] * b_im
    bbi = cr[..., None] * b_im + ci[..., None] * b_re
    gb = S5_GROUP_BLOCK
    nblk = g // gb
    eye = jnp.eye(gb, dtype=F32)
    bb = jnp.stack([bbr, bbi]).reshape(2, nblk, gb, p, gs)
    bblk = jnp.einsum('qigpc,gh->igcqhp', bb, eye).reshape(nblk, gb * gs, 2 * gb * p)
    cc = jnp.stack([c_re.astype(F32), -c_im.astype(F32)]).reshape(2, nblk, gb, gs, p)
    cblk = jnp.einsum('qigcp,gh->iqgphc', cc, eye).reshape(nblk, 2 * gb * p, gb * gs)
    lam = jnp.stack([lbr.reshape(-1), lbi.reshape(-1)])
    lam = jnp.concatenate([lam, jnp.zeros((SUBLANES - 2, g * p), F32)], axis=0)
    return bblk, cblk, lam


def _s5_const_specs(bblk, cblk, lam, w, layer, nmap):
    z = lambda n: (lambda *g: (0,) * n)
    return [pl.BlockSpec(bblk.shape, z(3)), pl.BlockSpec(cblk.shape, z(3)), pl.BlockSpec(lam.shape, z(2)),
            pl.BlockSpec((None, 1, w), lambda *g: (layer, 0, 0)),
            pl.BlockSpec((None, w, w), lambda *g: (layer, 0, 0)),
            pl.BlockSpec((None, 1, w), lambda *g: (layer, 0, 0))]


def _s5_prompt(u, bblk, cblk, lam, d, wglu, bglu, layer, *, nb, seq, row0):
    w = u.shape[1]
    ns = lam.shape[1]
    r = min(ROW_TILE, seq)
    nt = seq // r
    t0 = row0 // r
    return pl.pallas_call(
        _s5_prompt_kernel,
        grid=(nb, nt),
        in_specs=[pl.BlockSpec((r, w), lambda b, j: (t0 + b * nt + j, 0))]
        + _s5_const_specs(bblk, cblk, lam, w, layer, 2),
        out_specs=[pl.BlockSpec((r, w), lambda b, j: (b * nt + j, 0)),
                   pl.BlockSpec((1, 1, ns), lambda b, j: (b, 0, 0)),
                   pl.BlockSpec((1, 1, ns), lambda b, j: (b, 0, 0))],
        out_shape=[jax.ShapeDtypeStruct((nb * seq, w), BF16),
                   jax.ShapeDtypeStruct((nb, 1, ns), F32), jax.ShapeDtypeStruct((nb, 1, ns), F32)],
        scratch_shapes=[pltpu.VMEM((r, ns), F32), pltpu.VMEM((r, ns), F32), pltpu.VMEM((SUBLANES, ns), F32)],
        compiler_params=_cparams(("arbitrary", "arbitrary")),
        name="s5_prompt",
    )(u, bblk, cblk, lam, d, wglu, bglu)


def _s5_sample(u, bblk, cblk, lam, d, wglu, bglu, h0r, h0i, layer, *, nb, seq, row0):
    w = u.shape[1]
    ns = lam.shape[1]
    r = min(ROW_TILE, nb * seq)
    nt = nb * seq // r
    spt = r // seq
    t0 = row0 // r
    return pl.pallas_call(
        functools.partial(_s5_sample_kernel, seq=seq),
        grid=(nt,),
        in_specs=[pl.BlockSpec((r, w), lambda i: (t0 + i, 0))]
        + _s5_const_specs(bblk, cblk, lam, w, layer, 1)
        + [pl.BlockSpec((spt, ns), lambda i: (i, 0)), pl.BlockSpec((spt, ns), lambda i: (i, 0))],
        out_specs=[pl.BlockSpec((r, w), lambda i: (i, 0)),
                   pl.BlockSpec((spt, ns), lambda i: (i, 0)),
                   pl.BlockSpec((spt, ns), lambda i: (i, 0))],
        out_shape=[jax.ShapeDtypeStruct((nb * seq, w), BF16),
                   jax.ShapeDtypeStruct((nb, ns), F32), jax.ShapeDtypeStruct((nb, ns), F32)],
        scratch_shapes=[pltpu.VMEM((r, ns), F32), pltpu.VMEM((r, ns), F32)],
        compiler_params=_cparams(("arbitrary",)),
        name="s5_sample",
    )(u, bblk, cblk, lam, d, wglu, bglu, h0r, h0i)


def _route(probs, n_experts):
    epg = n_experts // N_EXPERT_GROUPS
    grouped = probs.reshape(-1, N_EXPERT_GROUPS, epg)
    group_score = lax.top_k(grouped, TOP_K)[0].sum(-1)
    g_sel = jnp.argmax(group_score, axis=-1).astype(jnp.int32)
    in_group = jnp.einsum('tge,tg->te', grouped, jax.nn.one_hot(g_sel, N_EXPERT_GROUPS, dtype=F32))
    top_p, top_i = lax.top_k(in_group, TOP_K)
    gate = top_p / jnp.sum(top_p, -1, keepdims=True)
    return g_sel[:, None] * epg + top_i.astype(jnp.int32), gate


def _dispatch(expert_idx, n_experts, bm):
    t = expert_idx.shape[0]
    n_pairs = t * TOP_K
    flat_e = expert_idx.reshape(-1)
    flat_tok = jnp.repeat(jnp.arange(t, dtype=jnp.int32), TOP_K)
    onehot = (flat_e[:, None] == jnp.arange(n_experts, dtype=jnp.int32)[None, :]).astype(jnp.int32)
    csum = jnp.cumsum(onehot, axis=0)
    counts = csum[-1]
    rank = jnp.sum((csum - onehot) * onehot, axis=1)
    padded = (counts + bm - 1) // bm * bm
    pstart = jnp.cumsum(padded) - padded
    pair_slot = jnp.sum(onehot * pstart[None, :], axis=1) + rank
    n_blocks = -(-(n_pairs + n_experts * (bm - 1)) // bm)
    n_slots = n_blocks * bm
    slot_tok = jnp.zeros((n_slots,), jnp.int32).at[pair_slot].set(flat_tok)
    block_exp = jnp.minimum(jnp.searchsorted(jnp.cumsum(padded), jnp.arange(n_blocks, dtype=jnp.int32) * bm,
                                             side='right'), n_experts - 1).astype(jnp.int32)
    n_used = (jnp.sum(padded) // bm).astype(jnp.int32).reshape(1)
    return slot_tok, pair_slot, block_exp, n_used, n_blocks


def _row_copy(src_hbm, dst, src_row, dst_row, sem):
    return pltpu.make_async_copy(src_hbm.at[pl.ds(src_row, 1)], dst.at[pl.ds(dst_row, 1)], sem)


def _gather_rows_kernel(tok_ref, x_hbm, o_ref, buf, sem):
    rows = buf.shape[0]
    base = pl.program_id(0) * rows

    def issue(i, c):
        _row_copy(x_hbm, buf, tok_ref[base + i], i, sem).start()
        return c

    lax.fori_loop(0, rows, issue, 0)

    def wait(i, c):
        _row_copy(x_hbm, buf, 0, i, sem).wait()
        return c

    lax.fori_loop(0, rows, wait, 0)
    o_ref[...] = buf[...].astype(o_ref.dtype)


def _gather_rows(x, slot_tok, bm):
    n_slots = slot_tok.shape[0]
    d = x.shape[1]
    rows = min(bm, 128)
    return pl.pallas_call(
        _gather_rows_kernel,
        grid_spec=pltpu.PrefetchScalarGridSpec(
            num_scalar_prefetch=1,
            grid=(n_slots // rows,),
            in_specs=[pl.BlockSpec(memory_space=pl.ANY)],
            out_specs=pl.BlockSpec((rows, d), lambda i, tok: (i, 0)),
            scratch_shapes=[pltpu.VMEM((rows, d), F32), pltpu.SemaphoreType.DMA(())]),
        out_shape=jax.ShapeDtypeStruct((n_slots, d), BF16),
        compiler_params=_cparams(("arbitrary",)),
        name="moe_gather",
    )(slot_tok, x)


def _expert_changed(be_ref, blk):
    return jnp.logical_or(blk == 0, be_ref[blk] != be_ref[jnp.maximum(blk - 1, 0)])


def _moe_gate_up_kernel(be_ref, nu_ref, x_ref, wg_ref, wu_ref, o_ref, wg_sc, wu_sc):
    blk = pl.program_id(1)

    @pl.when(_expert_changed(be_ref, blk))
    def _():
        wg_sc[...] = wg_ref[...].astype(BF16)
        wu_sc[...] = wu_ref[...].astype(BF16)

    @pl.when(blk < nu_ref[0])
    def _():
        x = x_ref[...]
        g = jnp.dot(x, wg_sc[...], preferred_element_type=F32)
        u = jnp.dot(x, wu_sc[...], preferred_element_type=F32)
        o_ref[...] = (_silu(g) * u).astype(o_ref.dtype)

    @pl.when(blk >= nu_ref[0])
    def _():
        o_ref[...] = jnp.zeros_like(o_ref)


def _moe_down_kernel(be_ref, nu_ref, h_ref, wd_ref, o_ref, wd_sc):
    blk = pl.program_id(1)

    @pl.when(_expert_changed(be_ref, blk))
    def _():
        wd_sc[...] = wd_ref[...].astype(BF16)

    @pl.when(blk < nu_ref[0])
    def _():
        o_ref[...] = jnp.dot(h_ref[...], wd_sc[...], preferred_element_type=F32)

    @pl.when(blk >= nu_ref[0])
    def _():
        o_ref[...] = jnp.zeros_like(o_ref)


def _moe_experts(xs, block_exp, n_used, w_gate, w_up, w_down, layer, bm):
    n_slots, d = xs.shape
    ff = w_gate.shape[-1]
    nblk = n_slots // bm
    tn = _pick(ff, MOE_TN_UP)
    w_in_spec = pl.BlockSpec((None, None, d, tn), lambda j, b, be, nu: (layer, be[b], 0, j))
    hm = pl.pallas_call(
        _moe_gate_up_kernel,
        grid_spec=pltpu.PrefetchScalarGridSpec(
            num_scalar_prefetch=2,
            grid=(ff // tn, nblk),
            in_specs=[pl.BlockSpec((bm, d), lambda j, b, be, nu: (b, 0)), w_in_spec, w_in_spec],
            out_specs=pl.BlockSpec((bm, tn), lambda j, b, be, nu: (b, j)),
            scratch_shapes=[pltpu.VMEM((d, tn), BF16), pltpu.VMEM((d, tn), BF16)]),
        out_shape=jax.ShapeDtypeStruct((n_slots, ff), BF16),
        compiler_params=_cparams(("arbitrary", "arbitrary")),
        name="moe_gate_up",
    )(block_exp, n_used, xs, w_gate, w_up)
    tn2 = _pick(d, MOE_TN_DOWN)
    return pl.pallas_call(
        _moe_down_kernel,
        grid_spec=pltpu.PrefetchScalarGridSpec(
            num_scalar_prefetch=2,
            grid=(d // tn2, nblk),
            in_specs=[pl.BlockSpec((bm, ff), lambda j, b, be, nu: (b, 0)),
                      pl.BlockSpec((None, None, ff, tn2), lambda j, b, be, nu: (layer, be[b], 0, j))],
            out_specs=pl.BlockSpec((bm, tn2), lambda j, b, be, nu: (b, j)),
            scratch_shapes=[pltpu.VMEM((ff, tn2), BF16)]),
        out_shape=jax.ShapeDtypeStruct((n_slots, d), F32),
        compiler_params=_cparams(("arbitrary", "arbitrary")),
        name="moe_down",
    )(block_exp, n_used, hm, w_down)


def _combine_ln_kernel(slot_ref, x_ref, w_ref, g_ref, b_ref, yb_hbm, of_ref, ob_ref, buf, sem, *, alpha):
    tm = x_ref.shape[0]
    base = pl.program_id(0) * tm

    def issue(i, c):
        for k in range(TOP_K):
            _row_copy(yb_hbm, buf.at[k], slot_ref[(base + i) * TOP_K + k], i, sem).start()
        return c

    lax.fori_loop(0, tm, issue, 0)

    def wait(i, c):
        for k in range(TOP_K):
            _row_copy(yb_hbm, buf.at[k], 0, i, sem).wait()
        return c

    lax.fori_loop(0, tm, wait, 0)
    w = w_ref[...]
    f = w[:, 0:1] * buf[0] + w[:, 1:2] * buf[1]
    z = _ln_math(alpha * x_ref[...] + f, g_ref[...], b_ref[...])
    of_ref[...] = z
    ob_ref[...] = z.astype(BF16)


def _combine_ln(x, yb, pair_slot, gate, g, b, alpha, layer):
    t, d = x.shape
    tm = min(128, t)
    return pl.pallas_call(
        functools.partial(_combine_ln_kernel, alpha=alpha),
        grid_spec=pltpu.PrefetchScalarGridSpec(
            num_scalar_prefetch=1,
            grid=(t // tm,),
            in_specs=[pl.BlockSpec((tm, d), lambda i, s: (i, 0)),
                      pl.BlockSpec((tm, TOP_K), lambda i, s: (i, 0)),
                      pl.BlockSpec((None, 1, d), lambda i, s: (layer, 0, 0)),
                      pl.BlockSpec((None, 1, d), lambda i, s: (layer, 0, 0)),
                      pl.BlockSpec(memory_space=pl.ANY)],
            out_specs=[pl.BlockSpec((tm, d), lambda i, s: (i, 0)),
                       pl.BlockSpec((tm, d), lambda i, s: (i, 0))],
            scratch_shapes=[pltpu.VMEM((TOP_K, tm, d), F32), pltpu.SemaphoreType.DMA(())]),
        out_shape=[jax.ShapeDtypeStruct((t, d), F32), jax.ShapeDtypeStruct((t, d), BF16)],
        compiler_params=_cparams(("arbitrary",)),
        name="moe_combine_ln2",
    )(pair_slot, x, gate, g, b, yb)


def kernel(x_prompt, x_sample, state_delta, state_conv, state_s5, state_ret, ln_in_g, ln_in_b, w_in, dn_conv_w, dn_a_log, dn_dt_bias, dn_norm_w, s5_lam_re, s5_lam_im, s5_log_step, s5_b_re, s5_b_im, s5_c_re, s5_c_im, s5_d, s5_w_glu, s5_b_glu, w_up_dn, w_up_s5, w_up_ret, w_o, ln1_g, ln1_b, router_w, router_b, w_gate_e, w_up_e, w_down_e, ln2_g, ln2_b):
    bp, lp, d = x_prompt.shape
    bs, ls, _ = x_sample.shape
    depth = w_in.shape[0]
    nh, dk, dv = state_delta.shape[2:]
    qkv = dn_conv_w.shape[2]
    hist = state_conv.shape[2]
    dnw = nh * dv
    s5w = s5_d.shape[1]
    g5, p5 = s5_lam_re.shape[1:]
    rh, rdk, rdv = state_ret.shape[2:]
    rw = rh * rdv
    n_exp = router_w.shape[1]
    tp, ts = bp * lp, bs * ls
    t = tp + ts
    alpha = (2 * depth) ** 0.25
    assert ls == SUBLANES and qkv == 3 * dnw and dk == dv == rdk == rdv == LANES

    sizes = (qkv, dnw, nh, nh, s5w, rh * rdk, rh * rdk, rw, rw, 3 * d)
    offs = [0]
    for s in sizes:
        offs.append(offs[-1] + s)
    o_dn, o_b, o_s5, o_ret, o_gates = offs[0], offs[2], offs[4], offs[5], offs[9]

    xf, xb = _ln_in(x_prompt.reshape(tp, d), x_sample.reshape(ts, d), ln_in_g, ln_in_b)

    ep = -(-n_exp // LANES) * LANES
    rw_pad = jnp.pad(router_w.astype(F32), ((0, 0), (0, ep - n_exp)))
    rb_pad = jnp.pad(router_b.astype(F32).reshape(1, n_exp), ((0, 0), (0, ep - n_exp)), constant_values=-1e30)

    cs_p, sn_p = _rope_tables(jnp.arange(lp, dtype=jnp.int32), rdk)
    cs_s, sn_s = _rope_tables(PAST_LEN + jnp.arange(ls, dtype=jnp.int32), rdk)
    rs = min(ROW_TILE, ts)
    cs_s = jnp.tile(cs_s, (rs // ls, 1))
    sn_s = jnp.tile(sn_s, (rs // ls, 1))

    new_delta_p, new_conv_p, new_s5_p, new_ret_p = [], [], [], []
    new_delta_s, new_conv_s, new_s5_s, new_ret_s = [], [], [], []

    for l in range(depth):
        wl = w_in[l]
        w_dn = wl[:, o_dn:o_dn + qkv + dnw].astype(BF16)
        w_ba = jnp.pad(wl[:, o_b:o_b + 2 * nh], ((0, 0), (0, LANES - 2 * nh))).astype(BF16)
        w_s5 = wl[:, o_s5:o_s5 + s5w].astype(BF16)
        w_ret = wl[:, o_ret:o_ret + 4 * rw].astype(BF16)
        w_gt = wl[:, o_gates:o_gates + 3 * d].astype(BF16)
        p_dn = _matmul(xb, w_dn, tile_major=True, name="proj_dn")
        ba = _matmul(xb, w_ba, name="proj_ba")
        u5 = _matmul(xb, w_s5, name="proj_s5")
        p_ret = _matmul(xb, w_ret, tile_major=True, name="proj_ret")
        gates = _matmul(xb, w_gt, name="proj_gates")

        tn = p_dn.shape[-1]
        bat = ba[:, :2 * nh].T
        zpad = jnp.zeros((nh,), F32)
        hpr = jnp.pad(jnp.stack([jnp.concatenate([zpad, dn_a_log[l].astype(F32)]),
                                 jnp.concatenate([zpad, dn_dt_bias[l].astype(F32)])]),
                      ((0, 0), (0, LANES - 2 * nh)))
        hpc = hpr[:, :2 * nh].T
        norm_w = dn_norm_w.reshape(depth, 1, dv)

        gates_p = _delta_gate(ba, bat, hpr, hpc, nh=nh, c=math.gcd(lp, CHUNK), row0=0, nrows=tp)
        gates_s = _delta_gate(ba, bat, hpr, hpc, nh=nh, c=ls, row0=tp, nrows=ts)
        odn_p, dlt_p = _delta_prompt(p_dn, gates_p, dn_conv_w, norm_w, l,
                                     nb=bp, seq=lp, nh=nh, dk=dk, dv=dv, row0=0)
        bufx = jnp.pad(state_conv[l], ((0, 0), (0, ls - hist), (0, 0))).reshape(ts, qkv)
        odn_s, dlt_s = _delta_sample(p_dn, gates_s, dn_conv_w, norm_w, bufx, state_delta, l,
                                     nb=bs, seq=ls, nh=nh, dk=dk, dv=dv, row0=tp, hist=hist)
        o_dn_all = jnp.concatenate([odn_p, odn_s], axis=0)
        nq = qkv // tn
        tail = jnp.arange(hist, dtype=jnp.int32)
        idx_p = (jnp.arange(bp, dtype=jnp.int32)[:, None] * lp + (lp - hist) + tail[None, :]).reshape(-1)
        idx_s = (tp + jnp.arange(bs, dtype=jnp.int32)[:, None] * ls + (ls - hist) + tail[None, :]).reshape(-1)
        cp = jnp.take(p_dn, idx_p, axis=1)[:nq]
        cs_ = jnp.take(p_dn, idx_s, axis=1)[:nq]
        new_conv_p.append(jnp.moveaxis(cp, 0, 1).reshape(bp, hist, qkv))
        new_conv_s.append(jnp.moveaxis(cs_, 0, 1).reshape(bs, hist, qkv))
        new_delta_p.append(dlt_p)
        new_delta_s.append(dlt_s)

        bblk, cblk, lam = _s5_params(s5_lam_re[l], s5_lam_im[l], s5_log_step[l], s5_b_re[l], s5_b_im[l],
                                     s5_c_re[l], s5_c_im[l])
        d5 = s5_d.reshape(depth, 1, s5w)
        bg5 = s5_b_glu.reshape(depth, 1, s5w)
        os5_p, hr_p, hi_p = _s5_prompt(u5, bblk, cblk, lam, d5, s5_w_glu, bg5, l, nb=bp, seq=lp, row0=0)
        h0 = state_s5[l].astype(F32)
        os5_s, hr_s, hi_s = _s5_sample(u5, bblk, cblk, lam, d5, s5_w_glu, bg5,
                                       h0[..., 0].reshape(bs, g5 * p5), h0[..., 1].reshape(bs, g5 * p5), l,
                                       nb=bs, seq=ls, row0=tp)
        o_s5_all = jnp.concatenate([os5_p, os5_s], axis=0)
        new_s5_p.append(jnp.stack([hr_p.reshape(bp, g5, p5), hi_p.reshape(bp, g5, p5)], -1))
        new_s5_s.append(jnp.stack([hr_s.reshape(bs, g5, p5), hi_s.reshape(bs, g5, p5)], -1))

        oret_p, rt_p = _ret_prompt(p_ret, cs_p, sn_p, nb=bp, seq=lp, nh=rh, dk=rdk, dv=rdv, row0=0)
        oret_s, rt_s = _ret_sample(p_ret, cs_s, sn_s, state_ret, l, nb=bs, seq=ls, nh=rh, dk=rdk, dv=rdv,
                                   row0=tp)
        o_ret_all = jnp.concatenate([oret_p, oret_s], axis=0)
        new_ret_p.append(rt_p)
        new_ret_s.append(rt_s)

        merged = _upmerge(o_dn_all, o_s5_all, o_ret_all, gates, w_up_dn, w_up_s5, w_up_ret, l)
        mix = _matmul(merged, w_o, layer=l, name="w_o")
        xf, xb, probs = _ln1_router(xf, mix, ln1_g.reshape(depth, 1, d), ln1_b.reshape(depth, 1, d),
                                    rw_pad, rb_pad, alpha, l)

        expert_idx, gate = _route(probs[:, :n_exp], n_exp)
        slot_tok, pair_slot, block_exp, n_used, _ = _dispatch(expert_idx, n_exp, MOE_BM)
        xs = _gather_rows(xf, slot_tok, MOE_BM)
        yb = _moe_experts(xs, block_exp, n_used, w_gate_e, w_up_e, w_down_e, l, MOE_BM)
        xf, xb = _combine_ln(xf, yb, pair_slot, gate, ln2_g.reshape(depth, 1, d), ln2_b.reshape(depth, 1, d),
                             alpha, l)

    y_prompt = xf[:tp].reshape(bp, lp, d)
    y_sample = xf[tp:].reshape(bs, ls, d)
    st = lambda xs_, ref: jnp.stack(xs_).astype(ref.dtype)
    return (y_prompt, y_sample,
            st(new_delta_p, state_delta), st(new_conv_p, state_conv), st(new_s5_p, state_s5),
            st(new_ret_p, state_ret),
            st(new_delta_s, state_delta), st(new_conv_s, state_conv), st(new_s5_s, state_s5),
            st(new_ret_s, state_ret))
```

```python
import functools
import math

import jax
import jax.numpy as jnp
from jax import lax
from jax.experimental import pallas as pl
from jax.experimental.pallas import tpu as pltpu

F32 = jnp.float32
BF16 = jnp.bfloat16
HI = lax.Precision.HIGHEST

LANES = 128
SUBLANES = 8
VMEM_LIMIT = 56 * 1024 * 1024

LN_EPS = 1e-5
RMS_EPS = 1e-6
ROPE_BASE = 10000.0
PAST_LEN = 16384
CHUNK = 64
N_EXPERT_GROUPS = 4
TOP_K = 2
S5_GROUP_BLOCK = 8

ROW_TILE = 256
MM_TM = 1024
MM_TN = 256
MIXER_TN = 512
MOE_BM = 256
MOE_TN_UP = 512
MOE_TN_DOWN = 1024


def _cparams(sem):
    return pltpu.CompilerParams(dimension_semantics=sem, vmem_limit_bytes=VMEM_LIMIT)


def _bdot(a, b):
    return jnp.dot(a.astype(BF16), b.astype(BF16), preferred_element_type=F32)


def _bdot_nt(a, b):
    return lax.dot_general(a.astype(BF16), b.astype(BF16), (((1,), (1,)), ((), ())),
                           preferred_element_type=F32)


def _bdot_tn(a, b):
    return lax.dot_general(a.astype(BF16), b.astype(BF16), (((0,), (0,)), ((), ())),
                           preferred_element_type=F32)


def _hdot(a, b):
    return jnp.dot(a, b, precision=HI, preferred_element_type=F32)


def _ln_math(x, g, b):
    mu = jnp.mean(x, -1, keepdims=True)
    xc = x - mu
    var = jnp.mean(xc * xc, -1, keepdims=True)
    return xc * lax.rsqrt(var + LN_EPS) * g + b


def _ln_in_kernel(xp_ref, xs_ref, g_ref, b_ref, of_ref, ob_ref, *, n_prompt_tiles):
    i = pl.program_id(0)

    def run(src):
        y = _ln_math(src[...], g_ref[...], b_ref[...])
        of_ref[...] = y
        ob_ref[...] = y.astype(BF16)

    @pl.when(i < n_prompt_tiles)
    def _():
        run(xp_ref)

    @pl.when(i >= n_prompt_tiles)
    def _():
        run(xs_ref)


def _ln_in(xp, xs, g, b):
    tp, d = xp.shape
    ts = xs.shape[0]
    tm = ROW_TILE
    npt, nst = tp // tm, ts // tm
    t = tp + ts
    return pl.pallas_call(
        functools.partial(_ln_in_kernel, n_prompt_tiles=npt),
        grid=(npt + nst,),
        in_specs=[pl.BlockSpec((tm, d), lambda i: (jnp.minimum(i, npt - 1), 0)),
                  pl.BlockSpec((tm, d), lambda i: (jnp.maximum(i - npt, 0), 0)),
                  pl.BlockSpec((1, d), lambda i: (0, 0)),
                  pl.BlockSpec((1, d), lambda i: (0, 0))],
        out_specs=[pl.BlockSpec((tm, d), lambda i: (i, 0)),
                   pl.BlockSpec((tm, d), lambda i: (i, 0))],
        out_shape=[jax.ShapeDtypeStruct((t, d), F32), jax.ShapeDtypeStruct((t, d), BF16)],
        compiler_params=_cparams(("arbitrary",)),
        name="ln_in",
    )(xp, xs, g.reshape(1, d), b.reshape(1, d))


def _ln1_router_kernel(x_ref, y_ref, g_ref, b_ref, rw_ref, rb_ref, of_ref, ob_ref, pr_ref, *, alpha):
    z = _ln_math(alpha * x_ref[...] + y_ref[...], g_ref[...], b_ref[...])
    of_ref[...] = z
    ob_ref[...] = z.astype(BF16)
    logits = _hdot(z, rw_ref[...]) + rb_ref[...]
    m = jnp.max(logits, -1, keepdims=True)
    e = jnp.exp(logits - m)
    pr_ref[...] = e / jnp.sum(e, -1, keepdims=True)


def _ln1_router(x, y, g, b, rw, rb, alpha, layer):
    t, d = x.shape
    tm = ROW_TILE
    ep = rw.shape[1]
    return pl.pallas_call(
        functools.partial(_ln1_router_kernel, alpha=alpha),
        grid=(t // tm,),
        in_specs=[pl.BlockSpec((tm, d), lambda i: (i, 0)),
                  pl.BlockSpec((tm, d), lambda i: (i, 0)),
                  pl.BlockSpec((None, 1, d), lambda i: (layer, 0, 0)),
                  pl.BlockSpec((None, 1, d), lambda i: (layer, 0, 0)),
                  pl.BlockSpec((d, ep), lambda i: (0, 0)),
                  pl.BlockSpec((1, ep), lambda i: (0, 0))],
        out_specs=[pl.BlockSpec((tm, d), lambda i: (i, 0)),
                   pl.BlockSpec((tm, d), lambda i: (i, 0)),
                   pl.BlockSpec((tm, ep), lambda i: (i, 0))],
        out_shape=[jax.ShapeDtypeStruct((t, d), F32), jax.ShapeDtypeStruct((t, d), BF16),
                   jax.ShapeDtypeStruct((t, ep), F32)],
        compiler_params=_cparams(("arbitrary",)),
        name="ln1_router",
    )(x, y, g, b, rw, rb)


def _mm_kernel(a_ref, w_ref, o_ref):
    o_ref[...] = jnp.dot(a_ref[...], w_ref[...].astype(BF16),
                         preferred_element_type=F32).astype(o_ref.dtype)


def _pick(n, pref):
    for c in (pref, 512, 256, 128):
        if c <= pref and n % c == 0:
            return c
    return n


def _mm_nt_kernel(a_ref, wt_ref, o_ref):
    o_ref[...] = lax.dot_general(a_ref[...], wt_ref[0].astype(BF16), (((1,), (1,)), ((), ())),
                                 preferred_element_type=F32).astype(o_ref.dtype)


def _matmul_nt(a, wt, *, layer, row0, n, tile_major=False, tn_pref=MM_TN, name="mm_nt"):
    m, k = a.shape
    tm = _pick(m, MM_TM)
    tn = _pick(n, tn_pref)
    assert row0 % SUBLANES == 0
    w_spec = pl.BlockSpec((pl.Element(1), pl.Element(tn), pl.Element(k)),
                          lambda i, j: (layer, pl.multiple_of(row0 + j * tn, SUBLANES), 0))
    if tile_major:
        out_spec = pl.BlockSpec((None, tm, tn), lambda i, j: (j, i, 0))
        out_shape = jax.ShapeDtypeStruct((n // tn, m, tn), F32)
    else:
        out_spec = pl.BlockSpec((tm, tn), lambda i, j: (i, j))
        out_shape = jax.ShapeDtypeStruct((m, n), F32)
    return pl.pallas_call(
        _mm_nt_kernel,
        grid=(m // tm, n // tn),
        in_specs=[pl.BlockSpec((tm, k), lambda i, j: (i, 0)), w_spec],
        out_specs=out_spec,
        out_shape=out_shape,
        compiler_params=_cparams(("arbitrary", "arbitrary")),
        name=name,
    )(a, wt)


def _matmul(a, w, *, layer=None, col0=0, n=None, tile_major=False, out_dtype=F32, tn_pref=MM_TN, name="mm"):
    m, k = a.shape
    n = w.shape[-1] if n is None else n
    tm = _pick(m, MM_TM)
    tn = _pick(n, tn_pref)
    assert col0 % tn == 0
    j0 = col0 // tn
    if w.ndim == 3:
        w_spec = pl.BlockSpec((None, k, tn), lambda i, j: (layer, 0, j0 + j))
    else:
        w_spec = pl.BlockSpec((k, tn), lambda i, j: (0, j0 + j))
    if tile_major:
        out_spec = pl.BlockSpec((None, tm, tn), lambda i, j: (j, i, 0))
        out_shape = jax.ShapeDtypeStruct((n // tn, m, tn), out_dtype)
    else:
        out_spec = pl.BlockSpec((tm, tn), lambda i, j: (i, j))
        out_shape = jax.ShapeDtypeStruct((m, n), out_dtype)
    return pl.pallas_call(
        _mm_kernel,
        grid=(m // tm, n // tn),
        in_specs=[pl.BlockSpec((tm, k), lambda i, j: (i, 0)), w_spec],
        out_specs=out_spec,
        out_shape=out_shape,
        compiler_params=_cparams(("arbitrary", "arbitrary")),
        name=name,
    )(a, w)


def _upmerge_kernel(odn_p, os5_p, oret_p, odn_s, os5_s, oret_s, gdn, gs5, gret, wdn, ws5, wret, o_ref,
                    *, n_prompt_tiles):
    is_prompt = pl.program_id(0) < n_prompt_tiles

    def branch(op, os_, w, g):
        o = jnp.where(is_prompt, op[...], os_[...])
        return jax.nn.sigmoid(g[...]) * jnp.dot(o, w[...].astype(BF16), preferred_element_type=F32)

    o_ref[...] = (branch(odn_p, odn_s, wdn, gdn) + branch(os5_p, os5_s, ws5, gs5)
                  + branch(oret_p, oret_s, wret, gret)).astype(o_ref.dtype)


def _upmerge(outs_p, outs_s, gates, w_dn, w_s5, w_ret, layer):
    tp, ts = outs_p[0].shape[0], outs_s[0].shape[0]
    t = tp + ts
    d = w_dn.shape[-1]
    tm = _pick(math.gcd(tp, ts), MM_TM // 2)
    tn = _pick(d, MM_TN)
    nj = d // tn
    npt = tp // tm
    nst = ts // tm

    def p_spec(a):
        return pl.BlockSpec((tm, a.shape[1]), lambda i, j: (jnp.minimum(i, npt - 1), 0))

    def s_spec(a):
        return pl.BlockSpec((tm, a.shape[1]), lambda i, j: (jnp.maximum(i - npt, 0), 0))

    def g_spec(off):
        return pl.BlockSpec((tm, tn), lambda i, j: (i, off * nj + j))

    def w_spec(w):
        return pl.BlockSpec((None, w.shape[1], tn), lambda i, j: (layer, 0, j))

    return pl.pallas_call(
        functools.partial(_upmerge_kernel, n_prompt_tiles=npt),
        grid=(npt + nst, nj),
        in_specs=[p_spec(a) for a in outs_p] + [s_spec(a) for a in outs_s]
        + [g_spec(0), g_spec(1), g_spec(2), w_spec(w_dn), w_spec(w_s5), w_spec(w_ret)],
        out_specs=pl.BlockSpec((tm, tn), lambda i, j: (i, j)),
        out_shape=jax.ShapeDtypeStruct((t, d), BF16),
        compiler_params=_cparams(("arbitrary", "arbitrary")),
        name="upmerge",
    )(*outs_p, *outs_s, gates, gates, gates, w_dn, w_s5, w_ret)


def _chunk_masks(r, c):
    ti = lax.broadcasted_iota(jnp.int32, (r, r), 0)
    si = lax.broadcasted_iota(jnp.int32, (r, r), 1)
    same = (ti // c) == (si // c)
    return same, same & (si <= ti), same & (si < ti)


def _shift_rows_carry(x, prev8, s):
    rolled = pltpu.roll(x, s, 0)
    prev_rolled = pltpu.roll(prev8, s, 0)
    row8 = lax.broadcasted_iota(jnp.int32, prev8.shape, 0)
    first = jnp.where(row8 < s, prev_rolled, rolled[0:SUBLANES])
    return jnp.concatenate([first, rolled[SUBLANES:]], axis=0)


def _shift_rows_seq8(x, bufx, s, hist):
    r = x.shape[0]
    rolled = pltpu.roll(x, s, 0)
    brolled = pltpu.roll(bufx, (r - (hist - s)) % r, 0)
    t8 = lax.broadcasted_iota(jnp.int32, x.shape, 0) % SUBLANES
    return jnp.where(t8 < s, brolled, rolled)


def _causal_conv(x, w_ref, shift):
    n = w_ref.shape[0]
    y = w_ref[n - 1:n, :] * x
    for s in range(1, n):
        y = y + w_ref[n - 1 - s:n - s, :] * shift(s)
    return y


def _silu(x):
    return x * jax.nn.sigmoid(x)


def _l2norm(x):
    return x * lax.rsqrt(jnp.sum(x * x, -1, keepdims=True) + RMS_EPS)


def _rms(x):
    return x * lax.rsqrt(jnp.mean(x * x, -1, keepdims=True) + RMS_EPS)


def _split3(x):
    h = x.astype(BF16)
    r1 = x - h.astype(F32)
    m = r1.astype(BF16)
    l = (r1 - m.astype(F32)).astype(BF16)
    return h, m, l


def _delta_gate_kernel(ba_ref, bat_ref, hpr_ref, hpc_ref, beta_ref, gcum_ref, gtot_ref, rows_ref, *, nh, c):
    r = ba_ref.shape[0]
    same, incl, _ = _chunk_masks(r, c)
    ti = lax.broadcasted_iota(jnp.int32, (r, r), 0)
    si = lax.broadcasted_iota(jnp.int32, (r, r), 1)
    incl_t = (same & (ti <= si)).astype(BF16)
    inclb = incl.astype(BF16)
    sameb = same.astype(BF16)
    dot = functools.partial(jnp.dot, preferred_element_type=F32)
    ba = ba_ref[...]
    lane = lax.broadcasted_iota(jnp.int32, ba.shape, 1)
    beta_ref[...] = jax.nn.sigmoid(ba)
    g_cols = -jnp.exp(hpr_ref[0:1, :]) * jax.nn.softplus(ba + hpr_ref[1:2, :])
    g_cols = jnp.where((lane >= nh) & (lane < 2 * nh), g_cols, 0.0)
    h, m, l = _split3(g_cols)
    gcum_ref[...] = dot(inclb, h) + dot(inclb, m) + dot(inclb, l)
    gtot_ref[...] = dot(sameb, h) + dot(sameb, m) + dot(sameb, l)
    g_rows = -jnp.exp(hpc_ref[:, 0:1]) * jax.nn.softplus(bat_ref[...] + hpc_ref[:, 1:2])
    h, m, l = _split3(g_rows)
    rows_ref[...] = dot(h, incl_t) + dot(m, incl_t) + dot(l, incl_t)


def _delta_gate(ba, bat, hpr, hpc, *, nh, c, row0, nrows):
    r = min(ROW_TILE, nrows)
    t0 = row0 // r
    nh2 = 2 * nh
    col = pl.BlockSpec((r, LANES), lambda i: (i, 0))
    return pl.pallas_call(
        functools.partial(_delta_gate_kernel, nh=nh, c=c),
        grid=(nrows // r,),
        in_specs=[pl.BlockSpec((r, LANES), lambda i: (t0 + i, 0)),
                  pl.BlockSpec((nh2, r), lambda i: (0, t0 + i)),
                  pl.BlockSpec((2, LANES), lambda i: (0, 0)),
                  pl.BlockSpec((nh2, 2), lambda i: (0, 0))],
        out_specs=[col, col, col, pl.BlockSpec((nh2, r), lambda i: (0, i))],
        out_shape=[jax.ShapeDtypeStruct((nrows, LANES), F32)] * 3 + [jax.ShapeDtypeStruct((nh2, nrows), F32)],
        compiler_params=_cparams(("arbitrary",)),
        name="delta_gate",
    )(ba, bat, hpr, hpc)


def _pick_col(cols, idx):
    lane = lax.broadcasted_iota(jnp.int32, cols.shape, 1)
    return jnp.sum(jnp.where(lane == idx, cols, 0.0), -1, keepdims=True)


def _delta_prep(q, k, v, beta_c, gc_c, gt_c, gc_r, c):
    r = q.shape[0]
    dk = k.shape[1]
    _, incl, strict = _chunk_masks(r, c)
    dec = jnp.exp(jnp.minimum(gc_c - gc_r, 0.0))
    m = jnp.where(strict, beta_c * dec * _bdot_nt(k, k), 0.0)
    a = jnp.where(incl, dec * _bdot_nt(q, k), 0.0)
    eye = (lax.broadcasted_iota(jnp.int32, (r, r), 0) == lax.broadcasted_iota(jnp.int32, (r, r), 1)).astype(F32)
    x = eye - m
    pw = m
    for _ in range(int(math.log2(c)) - 1):
        pw = _bdot(pw, pw)
        x = x + _bdot(x, pw)
    gin = jnp.exp(gc_c)
    wu = _bdot(x, jnp.concatenate([beta_c * gin * k, beta_c * v], axis=1))
    return a, wu[:, :dk], wu[:, dk:], k * jnp.exp(gt_c - gc_c), jnp.exp(gt_c), gin


def _delta_head_inputs(qa, ka, va, beta_ref, gcum_ref, gtot_ref, rows_ref, i, h, nh, dk):
    cols = slice(i * dk, (i + 1) * dk)
    q = _l2norm(qa[:, cols]) * (dk ** -0.5)
    k = _l2norm(ka[:, cols])
    beta_c = _pick_col(beta_ref[...], h)
    gc_c = _pick_col(gcum_ref[...], nh + h)
    gt_c = _pick_col(gtot_ref[...], nh + h)
    gc_r = rows_ref[pl.ds(nh + h, 1), :]
    return q, k, va[:, cols], beta_c, gc_c, gt_c, gc_r


def _delta_finish(a, u, qs, z, nw_ref):
    o = qs + _bdot(a, u)
    return _rms(o) * nw_ref[...] * _silu(z)


def _delta_prompt_kernel(q_ref, k_ref, v_ref, z_ref, beta_ref, gcum_ref, gtot_ref, rows_ref,
                         cwq_ref, cwk_ref, cwv_ref, nw_ref,
                         o_ref, st_ref,
                         s_sc, pq_sc, pk_sc, pv_sc,
                         *, nh, hb, c, dk):
    p = pl.program_id(1)
    j = pl.program_id(2)
    r = q_ref.shape[0]

    @pl.when(j == 0)
    def _():
        s_sc[...] = jnp.zeros_like(s_sc)
        pq_sc[...] = jnp.zeros_like(pq_sc)
        pk_sc[...] = jnp.zeros_like(pk_sc)
        pv_sc[...] = jnp.zeros_like(pv_sc)

    def conv(x_ref, w_ref, prev_sc):
        x = x_ref[...]
        prev8 = prev_sc[...]
        y = _causal_conv(x, w_ref, lambda s: _shift_rows_carry(x, prev8, s))
        prev_sc[...] = x[r - SUBLANES:, :]
        return _silu(y)

    qa = conv(q_ref, cwq_ref, pq_sc)
    ka = conv(k_ref, cwk_ref, pk_sc)
    va = conv(v_ref, cwv_ref, pv_sc)
    z = z_ref[...]

    for i in range(hb):
        q, k, v, beta_c, gc_c, gt_c, gc_r = _delta_head_inputs(qa, ka, va, beta_ref, gcum_ref, gtot_ref,
                                                               rows_ref, i, p * hb + i, nh, dk)
        a, w, u0, kd, eg, gin = _delta_prep(q, k, v, beta_c, gc_c, gt_c, gc_r, c)
        s = s_sc[i]
        us, qss = [], []
        for n in range(r // c):
            rows = slice(n * c, (n + 1) * c)
            us.append(u0[rows] - _bdot(w[rows], s))
            qss.append(gin[rows] * _bdot(q[rows], s))
            s = eg[n * c:n * c + 1] * s - _bdot(_bdot_tn(kd[rows], w[rows]), s) + _bdot_tn(kd[rows], u0[rows])
        s_sc[i] = s
        st_ref[0, i] = s
        cols = slice(i * dk, (i + 1) * dk)
        o = _delta_finish(a, jnp.concatenate(us, axis=0), jnp.concatenate(qss, axis=0), z[:, cols], nw_ref)
        o_ref[:, cols] = o.astype(o_ref.dtype)


def _delta_sample_kernel(q_ref, k_ref, v_ref, z_ref, beta_ref, gcum_ref, gtot_ref, rows_ref,
                         cwq_ref, cwk_ref, cwv_ref, nw_ref, bq_ref, bk_ref, bv_ref, sin_ref,
                         o_ref, st_ref,
                         q_s, w_s, u0_s, kd_s, eg_s, gin_s, u_s, qs_s,
                         *, nh, hb, c, dk, hist):
    p = pl.program_id(1)
    r = q_ref.shape[0]

    def conv(x_ref, w_ref, b_ref):
        x = x_ref[...]
        bufx = b_ref[...]
        return _silu(_causal_conv(x, w_ref, lambda s: _shift_rows_seq8(x, bufx, s, hist)))

    qa = conv(q_ref, cwq_ref, bq_ref)
    ka = conv(k_ref, cwk_ref, bk_ref)
    va = conv(v_ref, cwv_ref, bv_ref)
    z = z_ref[...]

    for i in range(hb):
        q, k, v, beta_c, gc_c, gt_c, gc_r = _delta_head_inputs(qa, ka, va, beta_ref, gcum_ref, gtot_ref,
                                                               rows_ref, i, p * hb + i, nh, dk)
        a, w, u0, kd, eg, gin = _delta_prep(q, k, v, beta_c, gc_c, gt_c, gc_r, c)
        q_s[...] = q
        w_s[...] = w
        u0_s[...] = u0
        kd_s[...] = kd
        eg_s[...] = jnp.broadcast_to(eg, eg_s.shape)
        gin_s[...] = jnp.broadcast_to(gin, gin_s.shape)

        def unit(n, carry, i=i):
            r0 = pl.multiple_of(n * c, c)
            rows = pl.ds(r0, c)
            s = sin_ref[n, i]
            wq = _bdot(jnp.concatenate([w_s[rows, :], q_s[rows, :]], axis=0), s)
            u = u0_s[rows, :] - wq[:c]
            u_s[rows, :] = u
            qs_s[rows, :] = gin_s[rows, :] * wq[c:]
            st_ref[n, i] = eg_s[pl.ds(r0, 1), :] * s + _bdot_tn(kd_s[rows, :], u)
            return carry

        lax.fori_loop(0, r // c, unit, 0, unroll=4)
        cols = slice(i * dk, (i + 1) * dk)
        o_ref[:, cols] = _delta_finish(a, u_s[...], qs_s[...], z[:, cols], nw_ref).astype(o_ref.dtype)


def _delta_scratch(r, dk, dv):
    return [pltpu.VMEM((r, dk), F32),
            pltpu.VMEM((r, dk), F32),
            pltpu.VMEM((r, dv), F32),
            pltpu.VMEM((r, dk), F32),
            pltpu.VMEM((r, dv), F32),
            pltpu.VMEM((r, dv), F32),
            pltpu.VMEM((r, dv), F32),
            pltpu.VMEM((r, dv), F32)]


def _delta_common_specs(tiles_per_seg, row_map, local_map, taps, dv, layer, tn, r, nh2):
    def seg(n):
        return pl.BlockSpec((None, r, tn), lambda *g: (n * tiles_per_seg + g[1], row_map(*g), 0))

    def cw(n):
        return pl.BlockSpec((None, taps, tn), lambda *g: (layer, 0, n * tiles_per_seg + g[1]))

    col = pl.BlockSpec((r, LANES), lambda *g: (local_map(*g), 0))
    return [seg(0), seg(1), seg(2), seg(3), col, col, col,
            pl.BlockSpec((nh2, r), lambda *g: (0, local_map(*g))),
            cw(0), cw(1), cw(2),
            pl.BlockSpec((None, 1, dv), lambda *g: (layer, 0, 0))]


def _delta_prompt(p_dn, gates, conv_w, norm_w, layer, *, nb, seq, nh, dk, dv, row0):
    tn = p_dn.shape[-1]
    hb = tn // dk
    r = min(ROW_TILE, seq)
    c = math.gcd(seq, CHUNK)
    npair = nh // hb
    nt = seq // r
    t0 = row0 // r
    kern = functools.partial(_delta_prompt_kernel, nh=nh, hb=hb, c=c, dk=dk)
    specs = _delta_common_specs(npair, lambda b, p, j: t0 + b * nt + j, lambda b, p, j: b * nt + j,
                                conv_w.shape[1], dv, layer, tn, r, 2 * nh)
    return pl.pallas_call(
        kern,
        grid=(nb, npair, nt),
        in_specs=specs,
        out_specs=[pl.BlockSpec((r, tn), lambda b, p, j: (b * nt + j, p)),
                   pl.BlockSpec((1, hb, dk, dv), lambda b, p, j: (b, p, 0, 0))],
        out_shape=[jax.ShapeDtypeStruct((nb * seq, nh * dv), BF16),
                   jax.ShapeDtypeStruct((nb, nh, dk, dv), F32)],
        scratch_shapes=[pltpu.VMEM((hb, dk, dv), F32),
                        pltpu.VMEM((SUBLANES, tn), F32), pltpu.VMEM((SUBLANES, tn), F32),
                        pltpu.VMEM((SUBLANES, tn), F32)],
        compiler_params=_cparams(("arbitrary", "arbitrary", "arbitrary")),
        name="delta_prompt",
    )(p_dn, p_dn, p_dn, p_dn, *gates, conv_w, conv_w, conv_w, norm_w)


def _delta_sample(p_dn, gates, conv_w, norm_w, bufx, state, layer, *, nb, seq, nh, dk, dv, row0, hist):
    tn = p_dn.shape[-1]
    hb = tn // dk
    c = seq
    r = min(ROW_TILE, nb * seq)
    npair = nh // hb
    nt = nb * seq // r
    spt = r // seq
    t0 = row0 // r
    kern = functools.partial(_delta_sample_kernel, nh=nh, hb=hb, c=c, dk=dk, hist=hist)
    specs = _delta_common_specs(npair, lambda i, p: t0 + i, lambda i, p: i,
                                conv_w.shape[1], dv, layer, tn, r, 2 * nh)

    def buf(n):
        return pl.BlockSpec((r, tn), lambda i, p: (i, n * npair + p))

    specs += [buf(0), buf(1), buf(2),
              pl.BlockSpec((None, spt, hb, dk, dv), lambda i, p: (layer, i, p, 0, 0))]
    return pl.pallas_call(
        kern,
        grid=(nt, npair),
        in_specs=specs,
        out_specs=[pl.BlockSpec((r, tn), lambda i, p: (i, p)),
                   pl.BlockSpec((spt, hb, dk, dv), lambda i, p: (i, p, 0, 0))],
        out_shape=[jax.ShapeDtypeStruct((nb * seq, nh * dv), BF16),
                   jax.ShapeDtypeStruct((nb, nh, dk, dv), F32)],
        scratch_shapes=_delta_scratch(r, dk, dv),
        compiler_params=_cparams(("arbitrary", "arbitrary")),
        name="delta_sample",
    )(p_dn, p_dn, p_dn, p_dn, *gates, conv_w, conv_w, conv_w, norm_w, bufx, bufx, bufx, state)


def _rotary(x, cs_ref, sn_ref):
    half = x.shape[1] // 2
    return x * cs_ref[...] + pltpu.roll(x, half, 1) * sn_ref[...]


def _ret_head(q, k, v, g, dec_ref, qd_ref, kd_ref, cd_ref, c, state_in, state_out,
              q_s, kd_s, v_s, qr_s):
    r = q.shape[0]
    scores = _bdot_nt(q, k) * dec_ref[...]
    inner = _bdot(scores, v)
    q_s[...] = q
    kd_s[...] = k * kd_ref[...]
    v_s[...] = v
    cd = cd_ref[0:1, :]

    def unit(n, carry):
        r0 = pl.multiple_of(n * c, c)
        rows = pl.ds(r0, c)
        s = state_in(n, carry)
        qr_s[rows, :] = _bdot(q_s[rows, :], s)
        s_new = cd * s + _bdot_tn(kd_s[rows, :], v_s[rows, :])
        return state_out(n, s_new)

    carry = lax.fori_loop(0, r // c, unit, state_in(None, None))
    o = inner + qd_ref[...] * qr_s[...]
    return _rms(o) * _silu(g), carry


def _ret_prompt_kernel(q_ref, k_ref, v_ref, g_ref, cs_ref, sn_ref, dec_ref, qd_ref, kd_ref, cd_ref,
                       o_ref, st_ref, s_sc, q_s, kd_s, v_s, qr_s, *, hb, c, dk):
    j = pl.program_id(2)

    @pl.when(j == 0)
    def _():
        s_sc[...] = jnp.zeros_like(s_sc)

    for i in range(hb):
        cols = slice(i * dk, (i + 1) * dk)
        q = _rotary(q_ref[:, cols], cs_ref, sn_ref)
        k = _rotary(k_ref[:, cols], cs_ref, sn_ref) * (dk ** -0.5)

        def state_in(n, carry, i=i):
            return s_sc[i] if n is None else carry

        def state_out(n, s_new):
            return s_new

        o, s_fin = _ret_head(q, k, v_ref[:, cols], g_ref[:, cols], dec_ref.at[i], qd_ref.at[i], kd_ref.at[i],
                             cd_ref.at[i], c, state_in, state_out, q_s, kd_s, v_s, qr_s)
        s_sc[i] = s_fin
        st_ref[0, i] = s_fin
        o_ref[:, cols] = o.astype(o_ref.dtype)


def _ret_sample_kernel(q_ref, k_ref, v_ref, g_ref, cs_ref, sn_ref, dec_ref, qd_ref, kd_ref, cd_ref, sin_ref,
                       o_ref, st_ref, q_s, kd_s, v_s, qr_s, *, hb, c, dk):
    for i in range(hb):
        cols = slice(i * dk, (i + 1) * dk)
        q = _rotary(q_ref[:, cols], cs_ref, sn_ref)
        k = _rotary(k_ref[:, cols], cs_ref, sn_ref) * (dk ** -0.5)

        def state_in(n, carry, i=i):
            return 0 if n is None else sin_ref[n, i]

        def state_out(n, s_new, i=i):
            st_ref[n, i] = s_new
            return 0

        o, _ = _ret_head(q, k, v_ref[:, cols], g_ref[:, cols], dec_ref.at[i], qd_ref.at[i], kd_ref.at[i],
                         cd_ref.at[i], c, state_in, state_out, q_s, kd_s, v_s, qr_s)
        o_ref[:, cols] = o.astype(o_ref.dtype)


def _ret_tables(nh, r, c, dk):
    log_gamma = jnp.log1p(-jnp.exp2(-5.0 - jnp.arange(nh, dtype=F32)))
    ti = jnp.arange(r)
    same = (ti[:, None] // c) == (ti[None, :] // c)
    rel = (ti[:, None] - ti[None, :]).astype(F32)
    dec = jnp.where(same & (rel >= 0), jnp.exp(jnp.maximum(rel, 0.0) * log_gamma[:, None, None]), 0.0)
    pos = (ti % c).astype(F32)
    qd = jnp.exp((pos + 1.0) * log_gamma[:, None])
    kd = jnp.exp((c - 1.0 - pos) * log_gamma[:, None])
    cd = jnp.exp(c * log_gamma)
    bc = lambda a: jnp.broadcast_to(a[..., None], a.shape + (dk,))
    return dec, bc(qd), bc(kd), jnp.broadcast_to(cd[:, None, None], (nh, SUBLANES, dk))


def _rope_tables(pos, dk):
    half = dk // 2
    inv = ROPE_BASE ** (-jnp.arange(half, dtype=F32) / half)
    ang = pos.astype(F32)[:, None] * inv[None, :]
    cos, sin = jnp.cos(ang), jnp.sin(ang)
    return jnp.concatenate([cos, cos], -1), jnp.concatenate([-sin, sin], -1)


def _ret_specs(npair, row_map, pos_map, tn, r, hb, dk):
    def seg(n):
        return pl.BlockSpec((None, r, tn), lambda *g: (n * npair + g[1], row_map(*g), 0))

    def tab(shape):
        return pl.BlockSpec((hb,) + shape, lambda *g: (g[1],) + (0,) * len(shape))

    return [seg(0), seg(1), seg(2), seg(3),
            pl.BlockSpec((r, dk), lambda *g: (pos_map(*g), 0)),
            pl.BlockSpec((r, dk), lambda *g: (pos_map(*g), 0)),
            tab((r, r)), tab((r, dk)), tab((r, dk)), tab((SUBLANES, dk))]


def _ret_scratch(r, dk, dv):
    return [pltpu.VMEM((r, dk), F32), pltpu.VMEM((r, dk), F32), pltpu.VMEM((r, dv), F32),
            pltpu.VMEM((r, dv), F32)]


def _ret_prompt(p_ret, cs, sn, *, nb, seq, nh, dk, dv, row0):
    tn = p_ret.shape[-1]
    hb = tn // dk
    r = min(ROW_TILE, seq)
    c = math.gcd(seq, CHUNK)
    npair = nh // hb
    nt = seq // r
    t0 = row0 // r
    dec, qd, kd, cd = _ret_tables(nh, r, c, dk)
    specs = _ret_specs(npair, lambda b, p, j: t0 + b * nt + j, lambda b, p, j: j, tn, r, hb, dk)
    return pl.pallas_call(
        functools.partial(_ret_prompt_kernel, hb=hb, c=c, dk=dk),
        grid=(nb, npair, nt),
        in_specs=specs,
        out_specs=[pl.BlockSpec((r, tn), lambda b, p, j: (b * nt + j, p)),
                   pl.BlockSpec((1, hb, dk, dv), lambda b, p, j: (b, p, 0, 0))],
        out_shape=[jax.ShapeDtypeStruct((nb * seq, nh * dv), BF16),
                   jax.ShapeDtypeStruct((nb, nh, dk, dv), F32)],
        scratch_shapes=[pltpu.VMEM((hb, dk, dv), F32)] + _ret_scratch(r, dk, dv),
        compiler_params=_cparams(("arbitrary", "arbitrary", "arbitrary")),
        name="ret_prompt",
    )(p_ret, p_ret, p_ret, p_ret, cs, sn, dec, qd, kd, cd)


def _ret_sample(p_ret, cs, sn, state, layer, *, nb, seq, nh, dk, dv, row0):
    tn = p_ret.shape[-1]
    hb = tn // dk
    c = seq
    r = min(ROW_TILE, nb * seq)
    npair = nh // hb
    nt = nb * seq // r
    spt = r // seq
    t0 = row0 // r
    dec, qd, kd, cd = _ret_tables(nh, r, c, dk)
    specs = _ret_specs(npair, lambda i, p: t0 + i, lambda i, p: 0, tn, r, hb, dk)
    specs += [pl.BlockSpec((None, spt, hb, dk, dv), lambda i, p: (layer, i, p, 0, 0))]
    return pl.pallas_call(
        functools.partial(_ret_sample_kernel, hb=hb, c=c, dk=dk),
        grid=(nt, npair),
        in_specs=specs,
        out_specs=[pl.BlockSpec((r, tn), lambda i, p: (i, p)),
                   pl.BlockSpec((spt, hb, dk, dv), lambda i, p: (i, p, 0, 0))],
        out_shape=[jax.ShapeDtypeStruct((nb * seq, nh * dv), BF16),
                   jax.ShapeDtypeStruct((nb, nh, dk, dv), F32)],
        scratch_shapes=_ret_scratch(r, dk, dv),
        compiler_params=_cparams(("arbitrary", "arbitrary")),
        name="ret_sample",
    )(p_ret, p_ret, p_ret, p_ret, cs, sn, dec, qd, kd, cd, state)


S5_COL_CHUNK = 1024


def _s5_project_in(u_ref, bblk_ref, bur_s, bui_s):
    nblk = bblk_ref.shape[0]
    half = bblk_ref.shape[2] // 2
    for i in range(nblk):
        res = _bdot(u_ref[:, i * LANES:(i + 1) * LANES], bblk_ref[i])
        bur_s[:, i * half:(i + 1) * half] = res[:, :half]
        bui_s[:, i * half:(i + 1) * half] = res[:, half:]


def _s5_scan_rows(bur_s, bui_s, lam_ref, cols, row_start, nrows, hr, hi):
    lr = lam_ref[0:1, cols]
    li = lam_ref[1:2, cols]

    def step(t, carry):
        hr, hi = carry
        row = pl.ds(row_start + t, 1)
        nr = lr * hr - li * hi + bur_s[row, cols]
        ni = lr * hi + li * hr + bui_s[row, cols]
        bur_s[row, cols] = nr
        bui_s[row, cols] = ni
        return nr, ni

    return lax.fori_loop(0, nrows, step, (hr, hi))


def _s5_project_out(u_ref, bur_s, bui_s, cblk_ref, d_ref, wglu_ref, bglu_ref, o_ref):
    nblk = cblk_ref.shape[0]
    half = cblk_ref.shape[1] // 2
    ys = []
    for i in range(nblk):
        cb = cblk_ref[i]
        y = _bdot(bur_s[:, i * half:(i + 1) * half], cb[:half]) + _bdot(bui_s[:, i * half:(i + 1) * half], cb[half:])
        ys.append(y)
    y = jnp.concatenate(ys, axis=1) + d_ref[...] * u_ref[...]
    z = jax.nn.gelu(y)
    z = z * jax.nn.sigmoid(_bdot(z, wglu_ref[...]) + bglu_ref[...])
    o_ref[...] = z.astype(o_ref.dtype)


def _s5_prompt_kernel(u_ref, bblk_ref, cblk_ref, lam_ref, d_ref, wglu_ref, bglu_ref,
                      o_ref, hr_ref, hi_ref, bur_s, bui_s, hc_s):
    j = pl.program_id(1)
    r = u_ref.shape[0]
    ns = bur_s.shape[1]

    @pl.when(j == 0)
    def _():
        hc_s[...] = jnp.zeros_like(hc_s)

    _s5_project_in(u_ref, bblk_ref, bur_s, bui_s)
    cw = min(S5_COL_CHUNK, ns)
    for cc in range(ns // cw):
        cols = slice(cc * cw, (cc + 1) * cw)
        hr, hi = _s5_scan_rows(bur_s, bui_s, lam_ref, cols, 0, r, hc_s[0:1, cols], hc_s[1:2, cols])
        hc_s[0:1, cols] = hr
        hc_s[1:2, cols] = hi
    hr_ref[0] = hc_s[0:1, :]
    hi_ref[0] = hc_s[1:2, :]
    _s5_project_out(u_ref, bur_s, bui_s, cblk_ref, d_ref, wglu_ref, bglu_ref, o_ref)


def _s5_sample_kernel(u_ref, bblk_ref, cblk_ref, lam_ref, d_ref, wglu_ref, bglu_ref, h0r_ref, h0i_ref,
                      o_ref, hr_ref, hi_ref, bur_s, bui_s, *, seq):
    r = u_ref.shape[0]
    ns = bur_s.shape[1]
    _s5_project_in(u_ref, bblk_ref, bur_s, bui_s)
    cw = min(S5_COL_CHUNK, ns)
    for cc in range(ns // cw):
        cols = slice(cc * cw, (cc + 1) * cw)

        def per_seq(b, carry, cols=cols):
            hr, hi = _s5_scan_rows(bur_s, bui_s, lam_ref, cols, b * seq, seq,
                                   h0r_ref[pl.ds(b, 1), cols], h0i_ref[pl.ds(b, 1), cols])
            hr_ref[pl.ds(b, 1), cols] = hr
            hi_ref[pl.ds(b, 1), cols] = hi
            return carry

        lax.fori_loop(0, r // seq, per_seq, 0)
    _s5_project_out(u_ref, bur_s, bui_s, cblk_ref, d_ref, wglu_ref, bglu_ref, o_ref)


def _s5_params(lam_re, lam_im, log_step, b_re, b_im, c_re, c_im):
    g, p = lam_re.shape
    gs = b_re.shape[-1]
    dt = jnp.exp(log_step.astype(F32))[:, None]
    ar, ai = lam_re.astype(F32), lam_im.astype(F32)
    mag = jnp.exp(ar * dt)
    lbr, lbi = mag * jnp.cos(ai * dt), mag * jnp.sin(ai * dt)
    den = ar * ar + ai * ai
    nr, ni = lbr - 1.0, lbi
    cr = (nr * ar + ni * ai) / den
    ci = (ni * ar - nr * ai) / den
    bbr = cr[..., None] * b_re - ci[..., None] * b_im
    bbi = cr[..., None] * b_im + ci[..., None] * b_re
    gb = S5_GROUP_BLOCK
    nblk = g // gb
    eye = jnp.eye(gb, dtype=F32)
    bb = jnp.stack([bbr, bbi]).reshape(2, nblk, gb, p, gs)
    bblk = jnp.einsum('qigpc,gh->igcqhp', bb, eye).reshape(nblk, gb * gs, 2 * gb * p)
    cc = jnp.stack([c_re.astype(F32), -c_im.astype(F32)]).reshape(2, nblk, gb, gs, p)
    cblk = jnp.einsum('qigcp,gh->iqgphc', cc, eye).reshape(nblk, 2 * gb * p, gb * gs)
    lam = jnp.stack([lbr.reshape(-1), lbi.reshape(-1)])
    lam = jnp.concatenate([lam, jnp.zeros((SUBLANES - 2, g * p), F32)], axis=0)
    return bblk, cblk, lam


def _s5_const_specs(bblk, cblk, lam, w, layer, nmap):
    z = lambda n: (lambda *g: (0,) * n)
    return [pl.BlockSpec(bblk.shape, z(3)), pl.BlockSpec(cblk.shape, z(3)), pl.BlockSpec(lam.shape, z(2)),
            pl.BlockSpec((None, 1, w), lambda *g: (layer, 0, 0)),
            pl.BlockSpec((None, w, w), lambda *g: (layer, 0, 0)),
            pl.BlockSpec((None, 1, w), lambda *g: (layer, 0, 0))]


def _s5_prompt(u, bblk, cblk, lam, d, wglu, bglu, layer, *, nb, seq, row0):
    w = u.shape[1]
    ns = lam.shape[1]
    r = min(ROW_TILE, seq)
    nt = seq // r
    t0 = row0 // r
    return pl.pallas_call(
        _s5_prompt_kernel,
        grid=(nb, nt),
        in_specs=[pl.BlockSpec((r, w), lambda b, j: (t0 + b * nt + j, 0))]
        + _s5_const_specs(bblk, cblk, lam, w, layer, 2),
        out_specs=[pl.BlockSpec((r, w), lambda b, j: (b * nt + j, 0)),
                   pl.BlockSpec((1, 1, ns), lambda b, j: (b, 0, 0)),
                   pl.BlockSpec((1, 1, ns), lambda b, j: (b, 0, 0))],
        out_shape=[jax.ShapeDtypeStruct((nb * seq, w), BF16),
                   jax.ShapeDtypeStruct((nb, 1, ns), F32), jax.ShapeDtypeStruct((nb, 1, ns), F32)],
        scratch_shapes=[pltpu.VMEM((r, ns), F32), pltpu.VMEM((r, ns), F32), pltpu.VMEM((SUBLANES, ns), F32)],
        compiler_params=_cparams(("arbitrary", "arbitrary")),
        name="s5_prompt",
    )(u, bblk, cblk, lam, d, wglu, bglu)


def _s5_sample(u, bblk, cblk, lam, d, wglu, bglu, h0r, h0i, layer, *, nb, seq, row0):
    w = u.shape[1]
    ns = lam.shape[1]
    r = min(ROW_TILE, nb * seq)
    nt = nb * seq // r
    spt = r // seq
    t0 = row0 // r
    return pl.pallas_call(
        functools.partial(_s5_sample_kernel, seq=seq),
        grid=(nt,),
        in_specs=[pl.BlockSpec((r, w), lambda i: (t0 + i, 0))]
        + _s5_const_specs(bblk, cblk, lam, w, layer, 1)
        + [pl.BlockSpec((spt, ns), lambda i: (i, 0)), pl.BlockSpec((spt, ns), lambda i: (i, 0))],
        out_specs=[pl.BlockSpec((r, w), lambda i: (i, 0)),
                   pl.BlockSpec((spt, ns), lambda i: (i, 0)),
                   pl.BlockSpec((spt, ns), lambda i: (i, 0))],
        out_shape=[jax.ShapeDtypeStruct((nb * seq, w), BF16),
                   jax.ShapeDtypeStruct((nb, ns), F32), jax.ShapeDtypeStruct((nb, ns), F32)],
        scratch_shapes=[pltpu.VMEM((r, ns), F32), pltpu.VMEM((r, ns), F32)],
        compiler_params=_cparams(("arbitrary",)),
        name="s5_sample",
    )(u, bblk, cblk, lam, d, wglu, bglu, h0r, h0i)


def _top2(x):
    n = x.shape[-1]
    ids = jnp.arange(n, dtype=jnp.int32)
    i1 = jnp.argmax(x, axis=-1).astype(jnp.int32)
    m1 = jnp.max(x, axis=-1)
    rest = jnp.where(ids == i1[..., None], -jnp.inf, x)
    i2 = jnp.argmax(rest, axis=-1).astype(jnp.int32)
    m2 = jnp.max(rest, axis=-1)
    return jnp.stack([m1, m2], -1), jnp.stack([i1, i2], -1)


def _route(probs, n_experts):
    epg = n_experts // N_EXPERT_GROUPS
    grouped = probs.reshape(-1, N_EXPERT_GROUPS, epg)
    group_score = _top2(grouped)[0].sum(-1)
    g_sel = jnp.argmax(group_score, axis=-1).astype(jnp.int32)
    in_group = jnp.sum(jnp.where(jnp.arange(N_EXPERT_GROUPS)[None, :, None] == g_sel[:, None, None], grouped, 0.0),
                       axis=1)
    top_p, top_i = _top2(in_group)
    gate = top_p / jnp.sum(top_p, -1, keepdims=True)
    return g_sel[:, None] * epg + top_i.astype(jnp.int32), gate


def _dispatch(expert_idx, n_experts, bm):
    t = expert_idx.shape[0]
    n_pairs = t * TOP_K
    flat_e = expert_idx.reshape(-1)
    flat_tok = jnp.repeat(jnp.arange(t, dtype=jnp.int32), TOP_K)
    onehot = (flat_e[:, None] == jnp.arange(n_experts, dtype=jnp.int32)[None, :]).astype(jnp.int32)
    csum = jnp.cumsum(onehot, axis=0)
    counts = csum[-1]
    rank = jnp.sum((csum - onehot) * onehot, axis=1)
    padded = (counts + bm - 1) // bm * bm
    pstart = jnp.cumsum(padded) - padded
    pair_slot = jnp.sum(onehot * pstart[None, :], axis=1) + rank
    n_blocks = -(-(n_pairs + n_experts * (bm - 1)) // bm)
    n_slots = n_blocks * bm
    slot_tok = jnp.zeros((n_slots,), jnp.int32).at[pair_slot].set(flat_tok)
    block_exp = jnp.minimum(jnp.searchsorted(jnp.cumsum(padded), jnp.arange(n_blocks, dtype=jnp.int32) * bm,
                                             side='right'), n_experts - 1).astype(jnp.int32)
    n_used = (jnp.sum(padded) // bm).astype(jnp.int32).reshape(1)
    return slot_tok, pair_slot, block_exp, n_used, n_blocks


def _row_copy(src_hbm, dst, src_row, dst_row, sem):
    return pltpu.make_async_copy(src_hbm.at[pl.ds(src_row, 1)], dst.at[pl.ds(dst_row, 1)], sem)


def _gather_rows_kernel(tok_ref, x_hbm, o_ref, buf, sem):
    rows = buf.shape[0]
    base = pl.program_id(0) * rows

    def issue(i, c):
        _row_copy(x_hbm, buf, tok_ref[base + i], i, sem).start()
        return c

    lax.fori_loop(0, rows, issue, 0)

    def wait(i, c):
        _row_copy(x_hbm, buf, 0, i, sem).wait()
        return c

    lax.fori_loop(0, rows, wait, 0)
    o_ref[...] = buf[...].astype(o_ref.dtype)


def _gather_rows(x, slot_tok, bm):
    n_slots = slot_tok.shape[0]
    d = x.shape[1]
    rows = min(bm, 128)
    return pl.pallas_call(
        _gather_rows_kernel,
        grid_spec=pltpu.PrefetchScalarGridSpec(
            num_scalar_prefetch=1,
            grid=(n_slots // rows,),
            in_specs=[pl.BlockSpec(memory_space=pl.ANY)],
            out_specs=pl.BlockSpec((rows, d), lambda i, tok: (i, 0)),
            scratch_shapes=[pltpu.VMEM((rows, d), F32), pltpu.SemaphoreType.DMA(())]),
        out_shape=jax.ShapeDtypeStruct((n_slots, d), BF16),
        compiler_params=_cparams(("arbitrary",)),
        name="moe_gather",
    )(slot_tok, x)


def _expert_changed(be_ref, blk):
    return jnp.logical_or(blk == 0, be_ref[blk] != be_ref[jnp.maximum(blk - 1, 0)])


def _moe_gate_up_kernel(be_ref, nu_ref, x_ref, wg_ref, wu_ref, o_ref, wg_sc, wu_sc):
    blk = pl.program_id(1)

    @pl.when(_expert_changed(be_ref, blk))
    def _():
        wg_sc[...] = wg_ref[...].astype(BF16)
        wu_sc[...] = wu_ref[...].astype(BF16)

    @pl.when(blk < nu_ref[0])
    def _():
        x = x_ref[...]
        g = jnp.dot(x, wg_sc[...], preferred_element_type=F32)
        u = jnp.dot(x, wu_sc[...], preferred_element_type=F32)
        o_ref[...] = (_silu(g) * u).astype(o_ref.dtype)

    @pl.when(blk >= nu_ref[0])
    def _():
        o_ref[...] = jnp.zeros_like(o_ref)


def _moe_down_kernel(be_ref, nu_ref, h_ref, wd_ref, o_ref, wd_sc):
    blk = pl.program_id(1)

    @pl.when(_expert_changed(be_ref, blk))
    def _():
        wd_sc[...] = wd_ref[...].astype(BF16)

    @pl.when(blk < nu_ref[0])
    def _():
        o_ref[...] = jnp.dot(h_ref[...], wd_sc[...], preferred_element_type=F32)

    @pl.when(blk >= nu_ref[0])
    def _():
        o_ref[...] = jnp.zeros_like(o_ref)


def _moe_experts(xs, block_exp, n_used, w_gate, w_up, w_down, layer, bm):
    n_slots, d = xs.shape
    ff = w_gate.shape[-1]
    nblk = n_slots // bm
    tn = _pick(ff, MOE_TN_UP)
    w_in_spec = pl.BlockSpec((None, None, d, tn), lambda j, b, be, nu: (layer, be[b], 0, j))
    hm = pl.pallas_call(
        _moe_gate_up_kernel,
        grid_spec=pltpu.PrefetchScalarGridSpec(
            num_scalar_prefetch=2,
            grid=(ff // tn, nblk),
            in_specs=[pl.BlockSpec((bm, d), lambda j, b, be, nu: (b, 0)), w_in_spec, w_in_spec],
            out_specs=pl.BlockSpec((bm, tn), lambda j, b, be, nu: (b, j)),
            scratch_shapes=[pltpu.VMEM((d, tn), BF16), pltpu.VMEM((d, tn), BF16)]),
        out_shape=jax.ShapeDtypeStruct((n_slots, ff), BF16),
        compiler_params=_cparams(("arbitrary", "arbitrary")),
        name="moe_gate_up",
    )(block_exp, n_used, xs, w_gate, w_up)
    tn2 = _pick(d, MOE_TN_DOWN)
    return pl.pallas_call(
        _moe_down_kernel,
        grid_spec=pltpu.PrefetchScalarGridSpec(
            num_scalar_prefetch=2,
            grid=(d // tn2, nblk),
            in_specs=[pl.BlockSpec((bm, ff), lambda j, b, be, nu: (b, 0)),
                      pl.BlockSpec((None, None, ff, tn2), lambda j, b, be, nu: (layer, be[b], 0, j))],
            out_specs=pl.BlockSpec((bm, tn2), lambda j, b, be, nu: (b, j)),
            scratch_shapes=[pltpu.VMEM((ff, tn2), BF16)]),
        out_shape=jax.ShapeDtypeStruct((n_slots, d), F32),
        compiler_params=_cparams(("arbitrary", "arbitrary")),
        name="moe_down",
    )(block_exp, n_used, hm, w_down)


def _combine_ln_kernel(slot_ref, x_ref, w_ref, g_ref, b_ref, yb_hbm, of_ref, ob_ref, buf, sem, *, alpha):
    tm = x_ref.shape[0]
    base = pl.program_id(0) * tm

    def issue(i, c):
        for k in range(TOP_K):
            _row_copy(yb_hbm, buf.at[k], slot_ref[(base + i) * TOP_K + k], i, sem).start()
        return c

    lax.fori_loop(0, tm, issue, 0)

    def wait(i, c):
        for k in range(TOP_K):
            _row_copy(yb_hbm, buf.at[k], 0, i, sem).wait()
        return c

    lax.fori_loop(0, tm, wait, 0)
    w = w_ref[...]
    f = w[:, 0:1] * buf[0] + w[:, 1:2] * buf[1]
    z = _ln_math(alpha * x_ref[...] + f, g_ref[...], b_ref[...])
    of_ref[...] = z
    ob_ref[...] = z.astype(BF16)


def _combine_ln(x, yb, pair_slot, gate, g, b, alpha, layer):
    t, d = x.shape
    tm = min(128, t)
    return pl.pallas_call(
        functools.partial(_combine_ln_kernel, alpha=alpha),
        grid_spec=pltpu.PrefetchScalarGridSpec(
            num_scalar_prefetch=1,
            grid=(t // tm,),
            in_specs=[pl.BlockSpec((tm, d), lambda i, s: (i, 0)),
                      pl.BlockSpec((tm, TOP_K), lambda i, s: (i, 0)),
                      pl.BlockSpec((None, 1, d), lambda i, s: (layer, 0, 0)),
                      pl.BlockSpec((None, 1, d), lambda i, s: (layer, 0, 0)),
                      pl.BlockSpec(memory_space=pl.ANY)],
            out_specs=[pl.BlockSpec((tm, d), lambda i, s: (i, 0)),
                       pl.BlockSpec((tm, d), lambda i, s: (i, 0))],
            scratch_shapes=[pltpu.VMEM((TOP_K, tm, d), F32), pltpu.SemaphoreType.DMA(())]),
        out_shape=[jax.ShapeDtypeStruct((t, d), F32), jax.ShapeDtypeStruct((t, d), BF16)],
        compiler_params=_cparams(("arbitrary",)),
        name="moe_combine_ln2",
    )(pair_slot, x, gate, g, b, yb)


def kernel(x_prompt, x_sample, state_delta, state_conv, state_s5, state_ret, ln_in_g, ln_in_b, w_in, dn_conv_w, dn_a_log, dn_dt_bias, dn_norm_w, s5_lam_re, s5_lam_im, s5_log_step, s5_b_re, s5_b_im, s5_c_re, s5_c_im, s5_d, s5_w_glu, s5_b_glu, w_up_dn, w_up_s5, w_up_ret, w_o, ln1_g, ln1_b, router_w, router_b, w_gate_e, w_up_e, w_down_e, ln2_g, ln2_b):
    bp, lp, d = x_prompt.shape
    bs, ls, _ = x_sample.shape
    depth = w_in.shape[0]
    nh, dk, dv = state_delta.shape[2:]
    qkv = dn_conv_w.shape[2]
    hist = state_conv.shape[2]
    dnw = nh * dv
    s5w = s5_d.shape[1]
    g5, p5 = s5_lam_re.shape[1:]
    rh, rdk, rdv = state_ret.shape[2:]
    rw = rh * rdv
    n_exp = router_w.shape[1]
    tp, ts = bp * lp, bs * ls
    t = tp + ts
    alpha = (2 * depth) ** 0.25
    assert ls == SUBLANES and qkv == 3 * dnw and dk == dv == rdk == rdv == LANES

    sizes = (qkv, dnw, nh, nh, s5w, rh * rdk, rh * rdk, rw, rw, 3 * d)
    offs = [0]
    for s in sizes:
        offs.append(offs[-1] + s)
    o_dn, o_b, o_s5, o_ret, o_gates = offs[0], offs[2], offs[4], offs[5], offs[9]
    assert w_in.shape[2] == offs[-1] and 2 * nh < LANES
    w_in_t = jnp.swapaxes(w_in, 1, 2)

    xf, xb = _ln_in(x_prompt.reshape(tp, d), x_sample.reshape(ts, d), ln_in_g, ln_in_b)

    ep = -(-n_exp // LANES) * LANES
    rw_pad = jnp.pad(router_w.astype(F32), ((0, 0), (0, ep - n_exp)))
    rb_pad = jnp.pad(router_b.astype(F32).reshape(1, n_exp), ((0, 0), (0, ep - n_exp)), constant_values=-1e30)

    cs_p, sn_p = _rope_tables(jnp.arange(lp, dtype=jnp.int32), rdk)
    cs_s, sn_s = _rope_tables(PAST_LEN + jnp.arange(ls, dtype=jnp.int32), rdk)
    rs = min(ROW_TILE, ts)
    cs_s = jnp.tile(cs_s, (rs // ls, 1))
    sn_s = jnp.tile(sn_s, (rs // ls, 1))

    new_delta_p, new_conv_p, new_s5_p, new_ret_p = [], [], [], []
    new_delta_s, new_conv_s, new_s5_s, new_ret_s = [], [], [], []

    for l in range(depth):
        p_dn = _matmul_nt(xb, w_in_t, layer=l, row0=o_dn, n=qkv + dnw, tile_major=True,
                          tn_pref=min(MIXER_TN, dnw), name="proj_dn")
        ba = _matmul_nt(xb, w_in_t, layer=l, row0=o_b, n=LANES, name="proj_ba")
        u5 = _matmul_nt(xb, w_in_t, layer=l, row0=o_s5, n=s5w, name="proj_s5")
        p_ret = _matmul_nt(xb, w_in_t, layer=l, row0=o_ret, n=4 * rw, tile_major=True,
                           tn_pref=min(MIXER_TN, rw), name="proj_ret")
        gates = _matmul_nt(xb, w_in_t, layer=l, row0=o_gates, n=3 * d, name="proj_gates")

        tn = p_dn.shape[-1]
        bat = ba[:, :2 * nh].T
        zpad = jnp.zeros((nh,), F32)
        hpr = jnp.pad(jnp.stack([jnp.concatenate([zpad, dn_a_log[l].astype(F32)]),
                                 jnp.concatenate([zpad, dn_dt_bias[l].astype(F32)])]),
                      ((0, 0), (0, LANES - 2 * nh)))
        hpc = hpr[:, :2 * nh].T
        norm_w = dn_norm_w.reshape(depth, 1, dv)

        gates_p = _delta_gate(ba, bat, hpr, hpc, nh=nh, c=math.gcd(lp, CHUNK), row0=0, nrows=tp)
        gates_s = _delta_gate(ba, bat, hpr, hpc, nh=nh, c=ls, row0=tp, nrows=ts)
        odn_p, dlt_p = _delta_prompt(p_dn, gates_p, dn_conv_w, norm_w, l,
                                     nb=bp, seq=lp, nh=nh, dk=dk, dv=dv, row0=0)
        bufx = jnp.pad(state_conv[l], ((0, 0), (0, ls - hist), (0, 0))).reshape(ts, qkv)
        odn_s, dlt_s = _delta_sample(p_dn, gates_s, dn_conv_w, norm_w, bufx, state_delta, l,
                                     nb=bs, seq=ls, nh=nh, dk=dk, dv=dv, row0=tp, hist=hist)
        nq = qkv // tn
        cp = jnp.stack([lax.slice(p_dn, (0, b * lp + lp - hist, 0), (nq, (b + 1) * lp, tn))
                        for b in range(bp)], axis=1)
        cs_ = lax.slice(p_dn, (0, tp, 0), (nq, t, tn)).reshape(nq, bs, ls, tn)[:, :, ls - hist:]
        new_conv_p.append(jnp.moveaxis(cp, 0, 2).reshape(bp, hist, qkv))
        new_conv_s.append(jnp.moveaxis(cs_, 0, 2).reshape(bs, hist, qkv))
        new_delta_p.append(dlt_p)
        new_delta_s.append(dlt_s)

        bblk, cblk, lam = _s5_params(s5_lam_re[l], s5_lam_im[l], s5_log_step[l], s5_b_re[l], s5_b_im[l],
                                     s5_c_re[l], s5_c_im[l])
        d5 = s5_d.reshape(depth, 1, s5w)
        bg5 = s5_b_glu.reshape(depth, 1, s5w)
        os5_p, hr_p, hi_p = _s5_prompt(u5, bblk, cblk, lam, d5, s5_w_glu, bg5, l, nb=bp, seq=lp, row0=0)
        h0 = state_s5[l].astype(F32)
        os5_s, hr_s, hi_s = _s5_sample(u5, bblk, cblk, lam, d5, s5_w_glu, bg5,
                                       h0[..., 0].reshape(bs, g5 * p5), h0[..., 1].reshape(bs, g5 * p5), l,
                                       nb=bs, seq=ls, row0=tp)
        new_s5_p.append(jnp.stack([hr_p.reshape(bp, g5, p5), hi_p.reshape(bp, g5, p5)], -1))
        new_s5_s.append(jnp.stack([hr_s.reshape(bs, g5, p5), hi_s.reshape(bs, g5, p5)], -1))

        oret_p, rt_p = _ret_prompt(p_ret, cs_p, sn_p, nb=bp, seq=lp, nh=rh, dk=rdk, dv=rdv, row0=0)
        oret_s, rt_s = _ret_sample(p_ret, cs_s, sn_s, state_ret, l, nb=bs, seq=ls, nh=rh, dk=rdk, dv=rdv,
                                   row0=tp)
        new_ret_p.append(rt_p)
        new_ret_s.append(rt_s)

        merged = _upmerge((odn_p, os5_p, oret_p), (odn_s, os5_s, oret_s), gates, w_up_dn, w_up_s5, w_up_ret, l)
        mix = _matmul(merged, w_o, layer=l, name="w_o")
        xf, xb, probs = _ln1_router(xf, mix, ln1_g.reshape(depth, 1, d), ln1_b.reshape(depth, 1, d),
                                    rw_pad, rb_pad, alpha, l)

        expert_idx, gate = _route(probs[:, :n_exp], n_exp)
        slot_tok, pair_slot, block_exp, n_used, _ = _dispatch(expert_idx, n_exp, MOE_BM)
        xs = _gather_rows(xf, slot_tok, MOE_BM)
        yb = _moe_experts(xs, block_exp, n_used, w_gate_e, w_up_e, w_down_e, l, MOE_BM)
        xf, xb = _combine_ln(xf, yb, pair_slot, gate, ln2_g.reshape(depth, 1, d), ln2_b.reshape(depth, 1, d),
                             alpha, l)

    y_prompt = xf[:tp].reshape(bp, lp, d)
    y_sample = xf[tp:].reshape(bs, ls, d)
    st = lambda xs_, ref: jnp.stack(xs_).astype(ref.dtype)
    return (y_prompt, y_sample,
            st(new_delta_p, state_delta), st(new_conv_p, state_conv), st(new_s5_p, state_s5),
            st(new_ret_p, state_ret),
            st(new_delta_s, state_delta), st(new_conv_s, state_conv), st(new_s5_s, state_s5),
            st(new_ret_s, state_ret))
```

```python
import functools
import math

import jax
import jax.numpy as jnp
from jax import lax
from jax.experimental import pallas as pl
from jax.experimental.pallas import tpu as pltpu

F32 = jnp.float32
BF16 = jnp.bfloat16
HI = lax.Precision.HIGHEST

LANES = 128
SUBLANES = 8
VMEM_LIMIT = 56 * 1024 * 1024

LN_EPS = 1e-5
RMS_EPS = 1e-6
ROPE_BASE = 10000.0
PAST_LEN = 16384
CHUNK = 64
N_EXPERT_GROUPS = 4
TOP_K = 2
S5_GROUP_BLOCK = 8

ROW_TILE = 256
MM_TM = 1024
MM_TN = 256
MIXER_TN = 512
MOE_BM = 512
MOE_TN_UP = 256
MOE_TN_DOWN = 1024
GATHER_ROWS = 128


def _cparams(sem):
    return pltpu.CompilerParams(dimension_semantics=sem, vmem_limit_bytes=VMEM_LIMIT)


def _bdot(a, b):
    return jnp.dot(a.astype(BF16), b.astype(BF16), preferred_element_type=F32)


def _bdot_nt(a, b):
    return lax.dot_general(a.astype(BF16), b.astype(BF16), (((1,), (1,)), ((), ())),
                           preferred_element_type=F32)


def _bdot_tn(a, b):
    return lax.dot_general(a.astype(BF16), b.astype(BF16), (((0,), (0,)), ((), ())),
                           preferred_element_type=F32)


def _hdot(a, b):
    return jnp.dot(a, b, precision=HI, preferred_element_type=F32)


def _ln_math(x, g, b):
    mu = jnp.mean(x, -1, keepdims=True)
    xc = x - mu
    var = jnp.mean(xc * xc, -1, keepdims=True)
    return xc * lax.rsqrt(var + LN_EPS) * g + b


def _ln_in_kernel(xp_ref, xs_ref, g_ref, b_ref, of_ref, ob_ref, *, n_prompt_tiles):
    i = pl.program_id(0)

    def run(src):
        y = _ln_math(src[...], g_ref[...], b_ref[...])
        of_ref[...] = y
        ob_ref[...] = y.astype(BF16)

    @pl.when(i < n_prompt_tiles)
    def _():
        run(xp_ref)

    @pl.when(i >= n_prompt_tiles)
    def _():
        run(xs_ref)


def _ln_in(xp, xs, g, b):
    tp, d = xp.shape
    ts = xs.shape[0]
    tm = ROW_TILE
    npt, nst = tp // tm, ts // tm
    t = tp + ts
    return pl.pallas_call(
        functools.partial(_ln_in_kernel, n_prompt_tiles=npt),
        grid=(npt + nst,),
        in_specs=[pl.BlockSpec((tm, d), lambda i: (jnp.minimum(i, npt - 1), 0)),
                  pl.BlockSpec((tm, d), lambda i: (jnp.maximum(i - npt, 0), 0)),
                  pl.BlockSpec((1, d), lambda i: (0, 0)),
                  pl.BlockSpec((1, d), lambda i: (0, 0))],
        out_specs=[pl.BlockSpec((tm, d), lambda i: (i, 0)),
                   pl.BlockSpec((tm, d), lambda i: (i, 0))],
        out_shape=[jax.ShapeDtypeStruct((t, d), F32), jax.ShapeDtypeStruct((t, d), BF16)],
        compiler_params=_cparams(("arbitrary",)),
        name="ln_in",
    )(xp, xs, g.reshape(1, d), b.reshape(1, d))


def _ln1_router_kernel(x_ref, y_ref, g_ref, b_ref, rw_ref, rb_ref, of_ref, ob_ref, pr_ref, *, alpha):
    z = _ln_math(alpha * x_ref[...] + y_ref[...], g_ref[...], b_ref[...])
    of_ref[...] = z
    ob_ref[...] = z.astype(BF16)
    logits = _hdot(z, rw_ref[...]) + rb_ref[...]
    m = jnp.max(logits, -1, keepdims=True)
    e = jnp.exp(logits - m)
    pr_ref[...] = e / jnp.sum(e, -1, keepdims=True)


def _ln1_router(x, y, g, b, rw, rb, alpha, layer):
    t, d = x.shape
    tm = ROW_TILE
    ep = rw.shape[1]
    return pl.pallas_call(
        functools.partial(_ln1_router_kernel, alpha=alpha),
        grid=(t // tm,),
        in_specs=[pl.BlockSpec((tm, d), lambda i: (i, 0)),
                  pl.BlockSpec((tm, d), lambda i: (i, 0)),
                  pl.BlockSpec((None, 1, d), lambda i: (layer, 0, 0)),
                  pl.BlockSpec((None, 1, d), lambda i: (layer, 0, 0)),
                  pl.BlockSpec((d, ep), lambda i: (0, 0)),
                  pl.BlockSpec((1, ep), lambda i: (0, 0))],
        out_specs=[pl.BlockSpec((tm, d), lambda i: (i, 0)),
                   pl.BlockSpec((tm, d), lambda i: (i, 0)),
                   pl.BlockSpec((tm, ep), lambda i: (i, 0))],
        out_shape=[jax.ShapeDtypeStruct((t, d), F32), jax.ShapeDtypeStruct((t, d), BF16),
                   jax.ShapeDtypeStruct((t, ep), F32)],
        compiler_params=_cparams(("arbitrary",)),
        name="ln1_router",
    )(x, y, g, b, rw, rb)


def _mm_kernel(a_ref, w_ref, o_ref):
    o_ref[...] = jnp.dot(a_ref[...], w_ref[...].astype(BF16),
                         preferred_element_type=F32).astype(o_ref.dtype)


def _pick(n, pref):
    for c in (pref, 512, 256, 128):
        if c <= pref and n % c == 0:
            return c
    return n


def _mm_nt_kernel(a_ref, wt_ref, o_ref):
    o_ref[...] = lax.dot_general(a_ref[...], wt_ref[0].astype(BF16), (((1,), (1,)), ((), ())),
                                 preferred_element_type=F32).astype(o_ref.dtype)


def _matmul_nt(a, wt, *, layer, row0, n, tile_major=False, tn_pref=MM_TN, name="mm_nt"):
    m, k = a.shape
    tm = _pick(m, MM_TM)
    tn = _pick(n, tn_pref)
    assert row0 % SUBLANES == 0
    w_spec = pl.BlockSpec((pl.Element(1), pl.Element(tn), pl.Element(k)),
                          lambda i, j: (layer, pl.multiple_of(row0 + j * tn, SUBLANES), 0))
    if tile_major:
        out_spec = pl.BlockSpec((None, tm, tn), lambda i, j: (j, i, 0))
        out_shape = jax.ShapeDtypeStruct((n // tn, m, tn), F32)
    else:
        out_spec = pl.BlockSpec((tm, tn), lambda i, j: (i, j))
        out_shape = jax.ShapeDtypeStruct((m, n), F32)
    return pl.pallas_call(
        _mm_nt_kernel,
        grid=(m // tm, n // tn),
        in_specs=[pl.BlockSpec((tm, k), lambda i, j: (i, 0)), w_spec],
        out_specs=out_spec,
        out_shape=out_shape,
        compiler_params=_cparams(("arbitrary", "arbitrary")),
        name=name,
    )(a, wt)


def _matmul(a, w, *, layer=None, col0=0, n=None, tile_major=False, out_dtype=F32, tn_pref=MM_TN, name="mm"):
    m, k = a.shape
    n = w.shape[-1] if n is None else n
    tm = _pick(m, MM_TM)
    tn = _pick(n, tn_pref)
    assert col0 % tn == 0
    j0 = col0 // tn
    if w.ndim == 3:
        w_spec = pl.BlockSpec((None, k, tn), lambda i, j: (layer, 0, j0 + j))
    else:
        w_spec = pl.BlockSpec((k, tn), lambda i, j: (0, j0 + j))
    if tile_major:
        out_spec = pl.BlockSpec((None, tm, tn), lambda i, j: (j, i, 0))
        out_shape = jax.ShapeDtypeStruct((n // tn, m, tn), out_dtype)
    else:
        out_spec = pl.BlockSpec((tm, tn), lambda i, j: (i, j))
        out_shape = jax.ShapeDtypeStruct((m, n), out_dtype)
    return pl.pallas_call(
        _mm_kernel,
        grid=(m // tm, n // tn),
        in_specs=[pl.BlockSpec((tm, k), lambda i, j: (i, 0)), w_spec],
        out_specs=out_spec,
        out_shape=out_shape,
        compiler_params=_cparams(("arbitrary", "arbitrary")),
        name=name,
    )(a, w)


def _upmerge_kernel(odn_p, os5_p, oret_p, odn_s, os5_s, oret_s, gdn, gs5, gret, wdn, ws5, wret, o_ref,
                    *, n_prompt_tiles):
    i = pl.program_id(0)

    def branch(o, w, g):
        return jax.nn.sigmoid(g[...]) * jnp.dot(o[...], w[...].astype(BF16), preferred_element_type=F32)

    def run(odn, os5, oret):
        o_ref[...] = (branch(odn, wdn, gdn) + branch(os5, ws5, gs5) + branch(oret, wret, gret)).astype(o_ref.dtype)

    @pl.when(i < n_prompt_tiles)
    def _():
        run(odn_p, os5_p, oret_p)

    @pl.when(i >= n_prompt_tiles)
    def _():
        run(odn_s, os5_s, oret_s)


def _upmerge(outs_p, outs_s, gates, w_dn, w_s5, w_ret, layer):
    tp, ts = outs_p[0].shape[0], outs_s[0].shape[0]
    t = tp + ts
    d = w_dn.shape[-1]
    tm = _pick(math.gcd(tp, ts), MM_TM)
    tn = _pick(d, MM_TN)
    nj = d // tn
    npt = tp // tm
    nst = ts // tm

    def p_spec(a):
        return pl.BlockSpec((tm, a.shape[1]), lambda i, j: (jnp.minimum(i, npt - 1), 0))

    def s_spec(a):
        return pl.BlockSpec((tm, a.shape[1]), lambda i, j: (jnp.maximum(i - npt, 0), 0))

    def g_spec(off):
        return pl.BlockSpec((tm, tn), lambda i, j: (i, off * nj + j))

    def w_spec(w):
        return pl.BlockSpec((None, w.shape[1], tn), lambda i, j: (layer, 0, j))

    return pl.pallas_call(
        functools.partial(_upmerge_kernel, n_prompt_tiles=npt),
        grid=(npt + nst, nj),
        in_specs=[p_spec(a) for a in outs_p] + [s_spec(a) for a in outs_s]
        + [g_spec(0), g_spec(1), g_spec(2), w_spec(w_dn), w_spec(w_s5), w_spec(w_ret)],
        out_specs=pl.BlockSpec((tm, tn), lambda i, j: (i, j)),
        out_shape=jax.ShapeDtypeStruct((t, d), BF16),
        compiler_params=_cparams(("arbitrary", "arbitrary")),
        name="upmerge",
    )(*outs_p, *outs_s, gates, gates, gates, w_dn, w_s5, w_ret)


def _chunk_masks(r, c):
    ti = lax.broadcasted_iota(jnp.int32, (r, r), 0)
    si = lax.broadcasted_iota(jnp.int32, (r, r), 1)
    same = (ti // c) == (si // c)
    return same, same & (si <= ti), same & (si < ti)


def _shift_rows_carry(x, prev8, s):
    rolled = pltpu.roll(x, s, 0)
    prev_rolled = pltpu.roll(prev8, s, 0)
    row8 = lax.broadcasted_iota(jnp.int32, prev8.shape, 0)
    first = jnp.where(row8 < s, prev_rolled, rolled[0:SUBLANES])
    return jnp.concatenate([first, rolled[SUBLANES:]], axis=0)


def _shift_rows_seq8(x, bufx, s, hist):
    r = x.shape[0]
    rolled = pltpu.roll(x, s, 0)
    brolled = pltpu.roll(bufx, (r - (hist - s)) % r, 0)
    t8 = lax.broadcasted_iota(jnp.int32, x.shape, 0) % SUBLANES
    return jnp.where(t8 < s, brolled, rolled)


def _causal_conv(x, w_ref, shift):
    n = w_ref.shape[0]
    y = w_ref[n - 1:n, :] * x
    for s in range(1, n):
        y = y + w_ref[n - 1 - s:n - s, :] * shift(s)
    return y


def _silu(x):
    return x * jax.nn.sigmoid(x)


def _l2norm(x):
    return x * lax.rsqrt(jnp.sum(x * x, -1, keepdims=True) + RMS_EPS)


def _rms(x):
    return x * lax.rsqrt(jnp.mean(x * x, -1, keepdims=True) + RMS_EPS)


def _split3(x):
    h = x.astype(BF16)
    r1 = x - h.astype(F32)
    m = r1.astype(BF16)
    l = (r1 - m.astype(F32)).astype(BF16)
    return h, m, l


def _delta_gate_kernel(ba_ref, bat_ref, hpr_ref, hpc_ref, beta_ref, gcum_ref, gtot_ref, rows_ref, *, nh, c):
    r = ba_ref.shape[0]
    same, incl, _ = _chunk_masks(r, c)
    ti = lax.broadcasted_iota(jnp.int32, (r, r), 0)
    si = lax.broadcasted_iota(jnp.int32, (r, r), 1)
    incl_t = (same & (ti <= si)).astype(BF16)
    inclb = incl.astype(BF16)
    sameb = same.astype(BF16)
    dot = functools.partial(jnp.dot, preferred_element_type=F32)
    ba = ba_ref[...]
    lane = lax.broadcasted_iota(jnp.int32, ba.shape, 1)
    beta_ref[...] = jax.nn.sigmoid(ba)
    g_cols = -jnp.exp(hpr_ref[0:1, :]) * jax.nn.softplus(ba + hpr_ref[1:2, :])
    g_cols = jnp.where((lane >= nh) & (lane < 2 * nh), g_cols, 0.0)
    h, m, l = _split3(g_cols)
    gcum_ref[...] = dot(inclb, h) + dot(inclb, m) + dot(inclb, l)
    gtot_ref[...] = dot(sameb, h) + dot(sameb, m) + dot(sameb, l)
    g_rows = -jnp.exp(hpc_ref[:, 0:1]) * jax.nn.softplus(bat_ref[...] + hpc_ref[:, 1:2])
    h, m, l = _split3(g_rows)
    rows_ref[...] = dot(h, incl_t) + dot(m, incl_t) + dot(l, incl_t)


def _delta_gate(ba, bat, hpr, hpc, *, nh, c, row0, nrows):
    r = min(ROW_TILE, nrows)
    t0 = row0 // r
    nh2 = 2 * nh
    col = pl.BlockSpec((r, LANES), lambda i: (i, 0))
    return pl.pallas_call(
        functools.partial(_delta_gate_kernel, nh=nh, c=c),
        grid=(nrows // r,),
        in_specs=[pl.BlockSpec((r, LANES), lambda i: (t0 + i, 0)),
                  pl.BlockSpec((nh2, r), lambda i: (0, t0 + i)),
                  pl.BlockSpec((2, LANES), lambda i: (0, 0)),
                  pl.BlockSpec((nh2, 2), lambda i: (0, 0))],
        out_specs=[col, col, col, pl.BlockSpec((nh2, r), lambda i: (0, i))],
        out_shape=[jax.ShapeDtypeStruct((nrows, LANES), F32)] * 3 + [jax.ShapeDtypeStruct((nh2, nrows), F32)],
        compiler_params=_cparams(("arbitrary",)),
        name="delta_gate",
    )(ba, bat, hpr, hpc)


def _pick_col(cols, idx):
    lane = lax.broadcasted_iota(jnp.int32, cols.shape, 1)
    return jnp.sum(jnp.where(lane == idx, cols, 0.0), -1, keepdims=True)


def _delta_prep(heads, c):
    r, dk = heads[0][1].shape
    _, incl, strict = _chunk_masks(r, c)
    ms, as_ = [], []
    for q, k, v, beta_c, gc_c, gt_c, gc_r in heads:
        dec = jnp.exp(jnp.minimum(gc_c - gc_r, 0.0))
        ms.append(jnp.where(strict, beta_c * dec * _bdot_nt(k, k), 0.0))
        as_.append(jnp.where(incl, dec * _bdot_nt(q, k), 0.0))
    eye = (lax.broadcasted_iota(jnp.int32, (r, r), 0) == lax.broadcasted_iota(jnp.int32, (r, r), 1)).astype(F32)
    xs = [eye - m for m in ms]
    pws = ms
    for _ in range(int(math.log2(c)) - 1):
        pws = [_bdot(pw, pw) for pw in pws]
        xs = [x + _bdot(x, pw) for x, pw in zip(xs, pws)]
    out = []
    for (q, k, v, beta_c, gc_c, gt_c, gc_r), a, x in zip(heads, as_, xs):
        gin = jnp.exp(gc_c)
        wu = _bdot(x, jnp.concatenate([beta_c * gin * k, beta_c * v], axis=1))
        out.append((a, wu[:, :dk], wu[:, dk:], k * jnp.exp(gt_c - gc_c), jnp.exp(gt_c), gin))
    return out


def _delta_head_inputs(qa, ka, va, beta_ref, gcum_ref, gtot_ref, rows_ref, i, h, nh, dk):
    cols = slice(i * dk, (i + 1) * dk)
    q = _l2norm(qa[:, cols]) * (dk ** -0.5)
    k = _l2norm(ka[:, cols])
    beta_c = _pick_col(beta_ref[...], h)
    gc_c = _pick_col(gcum_ref[...], nh + h)
    gt_c = _pick_col(gtot_ref[...], nh + h)
    gc_r = rows_ref[pl.ds(nh + h, 1), :]
    return q, k, va[:, cols], beta_c, gc_c, gt_c, gc_r


def _delta_finish(a, u, qs, z, nw_ref):
    o = qs + _bdot(a, u)
    return _rms(o) * nw_ref[...] * _silu(z)


def _delta_prompt_kernel(q_ref, k_ref, v_ref, z_ref, beta_ref, gcum_ref, gtot_ref, rows_ref,
                         cwq_ref, cwk_ref, cwv_ref, nw_ref,
                         o_ref, st_ref,
                         s_sc, pq_sc, pk_sc, pv_sc,
                         *, nh, hb, c, dk):
    p = pl.program_id(1)
    j = pl.program_id(2)
    r = q_ref.shape[0]

    @pl.when(j == 0)
    def _():
        s_sc[...] = jnp.zeros_like(s_sc)
        pq_sc[...] = jnp.zeros_like(pq_sc)
        pk_sc[...] = jnp.zeros_like(pk_sc)
        pv_sc[...] = jnp.zeros_like(pv_sc)

    def conv(x_ref, w_ref, prev_sc):
        x = x_ref[...]
        prev8 = prev_sc[...]
        y = _causal_conv(x, w_ref, lambda s: _shift_rows_carry(x, prev8, s))
        prev_sc[...] = x[r - SUBLANES:, :]
        return _silu(y)

    qa = conv(q_ref, cwq_ref, pq_sc)
    ka = conv(k_ref, cwk_ref, pk_sc)
    va = conv(v_ref, cwv_ref, pv_sc)
    z = z_ref[...]

    heads = [_delta_head_inputs(qa, ka, va, beta_ref, gcum_ref, gtot_ref, rows_ref, i, p * hb + i, nh, dk)
             for i in range(hb)]
    preps = _delta_prep(heads, c)
    nchunk = r // c
    chunk_rows = [slice(n * c, (n + 1) * c) for n in range(nchunk)]
    kws = [[_bdot_tn(pr[3][rows], pr[1][rows]) for rows in chunk_rows] for pr in preps]
    bcs = [[_bdot_tn(pr[3][rows], pr[2][rows]) for rows in chunk_rows] for pr in preps]
    ss = [s_sc[i] for i in range(hb)]
    us = [[] for _ in range(hb)]
    qss = [[] for _ in range(hb)]
    for n, rows in enumerate(chunk_rows):
        for i in range(hb):
            q = heads[i][0]
            _, w, u0, kd, eg, gin = preps[i]
            s = ss[i]
            us[i].append(u0[rows] - _bdot(w[rows], s))
            qss[i].append(gin[rows] * _bdot(q[rows], s))
            ss[i] = (eg[n * c:n * c + 1] * s + bcs[i][n]) - _bdot(kws[i][n], s)
    for i in range(hb):
        s_sc[i] = ss[i]
        st_ref[0, i] = ss[i]
        cols = slice(i * dk, (i + 1) * dk)
        o = _delta_finish(preps[i][0], jnp.concatenate(us[i], axis=0), jnp.concatenate(qss[i], axis=0),
                          z[:, cols], nw_ref)
        o_ref[:, cols] = o.astype(o_ref.dtype)


def _delta_sample_kernel(q_ref, k_ref, v_ref, z_ref, beta_ref, gcum_ref, gtot_ref, rows_ref,
                         cwq_ref, cwk_ref, cwv_ref, nw_ref, bq_ref, bk_ref, bv_ref, sin_ref,
                         o_ref, st_ref,
                         q_s, w_s, u0_s, kd_s, eg_s, gin_s, u_s, qs_s,
                         *, nh, hb, c, dk, hist):
    p = pl.program_id(1)
    r = q_ref.shape[0]

    def conv(x_ref, w_ref, b_ref):
        x = x_ref[...]
        bufx = b_ref[...]
        return _silu(_causal_conv(x, w_ref, lambda s: _shift_rows_seq8(x, bufx, s, hist)))

    qa = conv(q_ref, cwq_ref, bq_ref)
    ka = conv(k_ref, cwk_ref, bk_ref)
    va = conv(v_ref, cwv_ref, bv_ref)
    z = z_ref[...]

    heads = [_delta_head_inputs(qa, ka, va, beta_ref, gcum_ref, gtot_ref, rows_ref, i, p * hb + i, nh, dk)
             for i in range(hb)]
    for i in range(hb):
        q = heads[i][0]
        a, w, u0, kd, eg, gin = _delta_prep(heads[i:i + 1], c)[0]
        q_s[...] = q
        w_s[...] = w
        u0_s[...] = u0
        kd_s[...] = kd
        eg_s[...] = jnp.broadcast_to(eg, eg_s.shape)
        gin_s[...] = jnp.broadcast_to(gin, gin_s.shape)

        def unit(n, carry, i=i):
            r0 = pl.multiple_of(n * c, c)
            rows = pl.ds(r0, c)
            s = sin_ref[n, i]
            wq = _bdot(jnp.concatenate([w_s[rows, :], q_s[rows, :]], axis=0), s)
            u = u0_s[rows, :] - wq[:c]
            u_s[rows, :] = u
            qs_s[rows, :] = gin_s[rows, :] * wq[c:]
            st_ref[n, i] = eg_s[pl.ds(r0, 1), :] * s + _bdot_tn(kd_s[rows, :], u)
            return carry

        lax.fori_loop(0, r // c, unit, 0, unroll=4)
        cols = slice(i * dk, (i + 1) * dk)
        o_ref[:, cols] = _delta_finish(a, u_s[...], qs_s[...], z[:, cols], nw_ref).astype(o_ref.dtype)


def _delta_scratch(r, dk, dv):
    return [pltpu.VMEM((r, dk), F32),
            pltpu.VMEM((r, dk), F32),
            pltpu.VMEM((r, dv), F32),
            pltpu.VMEM((r, dk), F32),
            pltpu.VMEM((r, dv), F32),
            pltpu.VMEM((r, dv), F32),
            pltpu.VMEM((r, dv), F32),
            pltpu.VMEM((r, dv), F32)]


def _delta_common_specs(tiles_per_seg, row_map, local_map, taps, dv, layer, tn, r, nh2):
    def seg(n):
        return pl.BlockSpec((None, r, tn), lambda *g: (n * tiles_per_seg + g[1], row_map(*g), 0))

    def cw(n):
        return pl.BlockSpec((None, taps, tn), lambda *g: (layer, 0, n * tiles_per_seg + g[1]))

    col = pl.BlockSpec((r, LANES), lambda *g: (local_map(*g), 0))
    return [seg(0), seg(1), seg(2), seg(3), col, col, col,
            pl.BlockSpec((nh2, r), lambda *g: (0, local_map(*g))),
            cw(0), cw(1), cw(2),
            pl.BlockSpec((None, 1, dv), lambda *g: (layer, 0, 0))]


def _delta_prompt(p_dn, gates, conv_w, norm_w, layer, *, nb, seq, nh, dk, dv, row0):
    tn = p_dn.shape[-1]
    hb = tn // dk
    r = min(ROW_TILE, seq)
    c = math.gcd(seq, CHUNK)
    npair = nh // hb
    nt = seq // r
    t0 = row0 // r
    kern = functools.partial(_delta_prompt_kernel, nh=nh, hb=hb, c=c, dk=dk)
    specs = _delta_common_specs(npair, lambda b, p, j: t0 + b * nt + j, lambda b, p, j: b * nt + j,
                                conv_w.shape[1], dv, layer, tn, r, 2 * nh)
    return pl.pallas_call(
        kern,
        grid=(nb, npair, nt),
        in_specs=specs,
        out_specs=[pl.BlockSpec((r, tn), lambda b, p, j: (b * nt + j, p)),
                   pl.BlockSpec((1, hb, dk, dv), lambda b, p, j: (b, p, 0, 0))],
        out_shape=[jax.ShapeDtypeStruct((nb * seq, nh * dv), BF16),
                   jax.ShapeDtypeStruct((nb, nh, dk, dv), F32)],
        scratch_shapes=[pltpu.VMEM((hb, dk, dv), F32),
                        pltpu.VMEM((SUBLANES, tn), F32), pltpu.VMEM((SUBLANES, tn), F32),
                        pltpu.VMEM((SUBLANES, tn), F32)],
        compiler_params=_cparams(("arbitrary", "arbitrary", "arbitrary")),
        name="delta_prompt",
    )(p_dn, p_dn, p_dn, p_dn, *gates, conv_w, conv_w, conv_w, norm_w)


def _delta_sample(p_dn, gates, conv_w, norm_w, bufx, state, layer, *, nb, seq, nh, dk, dv, row0, hist):
    tn = p_dn.shape[-1]
    hb = tn // dk
    c = seq
    r = min(ROW_TILE, nb * seq)
    npair = nh // hb
    nt = nb * seq // r
    spt = r // seq
    t0 = row0 // r
    kern = functools.partial(_delta_sample_kernel, nh=nh, hb=hb, c=c, dk=dk, hist=hist)
    specs = _delta_common_specs(npair, lambda i, p: t0 + i, lambda i, p: i,
                                conv_w.shape[1], dv, layer, tn, r, 2 * nh)

    def buf(n):
        return pl.BlockSpec((r, tn), lambda i, p: (i, n * npair + p))

    specs += [buf(0), buf(1), buf(2),
              pl.BlockSpec((None, spt, hb, dk, dv), lambda i, p: (layer, i, p, 0, 0))]
    return pl.pallas_call(
        kern,
        grid=(nt, npair),
        in_specs=specs,
        out_specs=[pl.BlockSpec((r, tn), lambda i, p: (i, p)),
                   pl.BlockSpec((spt, hb, dk, dv), lambda i, p: (i, p, 0, 0))],
        out_shape=[jax.ShapeDtypeStruct((nb * seq, nh * dv), BF16),
                   jax.ShapeDtypeStruct((nb, nh, dk, dv), F32)],
        scratch_shapes=_delta_scratch(r, dk, dv),
        compiler_params=_cparams(("arbitrary", "arbitrary")),
        name="delta_sample",
    )(p_dn, p_dn, p_dn, p_dn, *gates, conv_w, conv_w, conv_w, norm_w, bufx, bufx, bufx, state)


def _rotary(x, cs_ref, sn_ref):
    half = x.shape[1] // 2
    return x * cs_ref[...] + pltpu.roll(x, half, 1) * sn_ref[...]


def _ret_head(q, k, v, g, dec_ref, qd_ref, kd_ref, cd_ref, c, state_in, state_out,
              q_s, kd_s, v_s, qr_s):
    r = q.shape[0]
    scores = _bdot_nt(q, k) * dec_ref[...]
    inner = _bdot(scores, v)
    q_s[...] = q
    kd_s[...] = k * kd_ref[...]
    v_s[...] = v
    cd = cd_ref[0:1, :]

    def unit(n, carry):
        r0 = pl.multiple_of(n * c, c)
        rows = pl.ds(r0, c)
        s = state_in(n, carry)
        qr_s[rows, :] = _bdot(q_s[rows, :], s)
        s_new = cd * s + _bdot_tn(kd_s[rows, :], v_s[rows, :])
        return state_out(n, s_new)

    carry = lax.fori_loop(0, r // c, unit, state_in(None, None))
    o = inner + qd_ref[...] * qr_s[...]
    return _rms(o) * _silu(g), carry


def _ret_prompt_kernel(q_ref, k_ref, v_ref, g_ref, cs_ref, sn_ref, dec_ref, qd_ref, kd_ref, cd_ref,
                       o_ref, st_ref, s_sc, q_s, kd_s, v_s, qr_s, *, hb, c, dk):
    j = pl.program_id(2)

    @pl.when(j == 0)
    def _():
        s_sc[...] = jnp.zeros_like(s_sc)

    for i in range(hb):
        cols = slice(i * dk, (i + 1) * dk)
        q = _rotary(q_ref[:, cols], cs_ref, sn_ref)
        k = _rotary(k_ref[:, cols], cs_ref, sn_ref) * (dk ** -0.5)

        def state_in(n, carry, i=i):
            return s_sc[i] if n is None else carry

        def state_out(n, s_new):
            return s_new

        o, s_fin = _ret_head(q, k, v_ref[:, cols], g_ref[:, cols], dec_ref.at[i], qd_ref.at[i], kd_ref.at[i],
                             cd_ref.at[i], c, state_in, state_out, q_s, kd_s, v_s, qr_s)
        s_sc[i] = s_fin
        st_ref[0, i] = s_fin
        o_ref[:, cols] = o.astype(o_ref.dtype)


def _ret_sample_kernel(q_ref, k_ref, v_ref, g_ref, cs_ref, sn_ref, dec_ref, qd_ref, kd_ref, cd_ref, sin_ref,
                       o_ref, st_ref, q_s, kd_s, v_s, qr_s, *, hb, c, dk):
    for i in range(hb):
        cols = slice(i * dk, (i + 1) * dk)
        q = _rotary(q_ref[:, cols], cs_ref, sn_ref)
        k = _rotary(k_ref[:, cols], cs_ref, sn_ref) * (dk ** -0.5)

        def state_in(n, carry, i=i):
            return 0 if n is None else sin_ref[n, i]

        def state_out(n, s_new, i=i):
            st_ref[n, i] = s_new
            return 0

        o, _ = _ret_head(q, k, v_ref[:, cols], g_ref[:, cols], dec_ref.at[i], qd_ref.at[i], kd_ref.at[i],
                         cd_ref.at[i], c, state_in, state_out, q_s, kd_s, v_s, qr_s)
        o_ref[:, cols] = o.astype(o_ref.dtype)


def _ret_tables(nh, r, c, dk):
    log_gamma = jnp.log1p(-jnp.exp2(-5.0 - jnp.arange(nh, dtype=F32)))
    ti = jnp.arange(r)
    same = (ti[:, None] // c) == (ti[None, :] // c)
    rel = (ti[:, None] - ti[None, :]).astype(F32)
    dec = jnp.where(same & (rel >= 0), jnp.exp(jnp.maximum(rel, 0.0) * log_gamma[:, None, None]), 0.0)
    pos = (ti % c).astype(F32)
    qd = jnp.exp((pos + 1.0) * log_gamma[:, None])
    kd = jnp.exp((c - 1.0 - pos) * log_gamma[:, None])
    cd = jnp.exp(c * log_gamma)
    bc = lambda a: jnp.broadcast_to(a[..., None], a.shape + (dk,))
    return dec, bc(qd), bc(kd), jnp.broadcast_to(cd[:, None, None], (nh, SUBLANES, dk))


def _rope_tables(pos, dk):
    half = dk // 2
    inv = ROPE_BASE ** (-jnp.arange(half, dtype=F32) / half)
    ang = pos.astype(F32)[:, None] * inv[None, :]
    cos, sin = jnp.cos(ang), jnp.sin(ang)
    return jnp.concatenate([cos, cos], -1), jnp.concatenate([-sin, sin], -1)


def _ret_specs(npair, row_map, pos_map, tn, r, hb, dk):
    def seg(n):
        return pl.BlockSpec((None, r, tn), lambda *g: (n * npair + g[1], row_map(*g), 0))

    def tab(shape):
        return pl.BlockSpec((hb,) + shape, lambda *g: (g[1],) + (0,) * len(shape))

    return [seg(0), seg(1), seg(2), seg(3),
            pl.BlockSpec((r, dk), lambda *g: (pos_map(*g), 0)),
            pl.BlockSpec((r, dk), lambda *g: (pos_map(*g), 0)),
            tab((r, r)), tab((r, dk)), tab((r, dk)), tab((SUBLANES, dk))]


def _ret_scratch(r, dk, dv):
    return [pltpu.VMEM((r, dk), F32), pltpu.VMEM((r, dk), F32), pltpu.VMEM((r, dv), F32),
            pltpu.VMEM((r, dv), F32)]


def _ret_prompt(p_ret, cs, sn, *, nb, seq, nh, dk, dv, row0):
    tn = p_ret.shape[-1]
    hb = tn // dk
    r = min(ROW_TILE, seq)
    c = math.gcd(seq, CHUNK)
    npair = nh // hb
    nt = seq // r
    t0 = row0 // r
    dec, qd, kd, cd = _ret_tables(nh, r, c, dk)
    specs = _ret_specs(npair, lambda b, p, j: t0 + b * nt + j, lambda b, p, j: j, tn, r, hb, dk)
    return pl.pallas_call(
        functools.partial(_ret_prompt_kernel, hb=hb, c=c, dk=dk),
        grid=(nb, npair, nt),
        in_specs=specs,
        out_specs=[pl.BlockSpec((r, tn), lambda b, p, j: (b * nt + j, p)),
                   pl.BlockSpec((1, hb, dk, dv), lambda b, p, j: (b, p, 0, 0))],
        out_shape=[jax.ShapeDtypeStruct((nb * seq, nh * dv), BF16),
                   jax.ShapeDtypeStruct((nb, nh, dk, dv), F32)],
        scratch_shapes=[pltpu.VMEM((hb, dk, dv), F32)] + _ret_scratch(r, dk, dv),
        compiler_params=_cparams(("arbitrary", "arbitrary", "arbitrary")),
        name="ret_prompt",
    )(p_ret, p_ret, p_ret, p_ret, cs, sn, dec, qd, kd, cd)


def _ret_sample(p_ret, cs, sn, state, layer, *, nb, seq, nh, dk, dv, row0):
    tn = p_ret.shape[-1]
    hb = tn // dk
    c = seq
    r = min(ROW_TILE, nb * seq)
    npair = nh // hb
    nt = nb * seq // r
    spt = r // seq
    t0 = row0 // r
    dec, qd, kd, cd = _ret_tables(nh, r, c, dk)
    specs = _ret_specs(npair, lambda i, p: t0 + i, lambda i, p: 0, tn, r, hb, dk)
    specs += [pl.BlockSpec((None, spt, hb, dk, dv), lambda i, p: (layer, i, p, 0, 0))]
    return pl.pallas_call(
        functools.partial(_ret_sample_kernel, hb=hb, c=c, dk=dk),
        grid=(nt, npair),
        in_specs=specs,
        out_specs=[pl.BlockSpec((r, tn), lambda i, p: (i, p)),
                   pl.BlockSpec((spt, hb, dk, dv), lambda i, p: (i, p, 0, 0))],
        out_shape=[jax.ShapeDtypeStruct((nb * seq, nh * dv), BF16),
                   jax.ShapeDtypeStruct((nb, nh, dk, dv), F32)],
        scratch_shapes=_ret_scratch(r, dk, dv),
        compiler_params=_cparams(("arbitrary", "arbitrary")),
        name="ret_sample",
    )(p_ret, p_ret, p_ret, p_ret, cs, sn, dec, qd, kd, cd, state)


S5_COL_CHUNK = 1024


def _s5_project_in(u_ref, bblk_ref, bur_s, bui_s):
    nblk = bblk_ref.shape[0]
    half = bblk_ref.shape[2] // 2
    for i in range(nblk):
        res = _bdot(u_ref[:, i * LANES:(i + 1) * LANES], bblk_ref[i])
        bur_s[:, i * half:(i + 1) * half] = res[:, :half]
        bui_s[:, i * half:(i + 1) * half] = res[:, half:]


def _s5_scan_rows(bur_s, bui_s, lam_ref, cols, row_start, nrows, hr, hi):
    lr = lam_ref[0:1, cols]
    li = lam_ref[1:2, cols]

    def step(t, carry):
        hr, hi = carry
        row = pl.ds(row_start + t, 1)
        nr = lr * hr - li * hi + bur_s[row, cols]
        ni = lr * hi + li * hr + bui_s[row, cols]
        bur_s[row, cols] = nr
        bui_s[row, cols] = ni
        return nr, ni

    return lax.fori_loop(0, nrows, step, (hr, hi))


def _s5_project_out(u_ref, bur_s, bui_s, cblk_ref, d_ref, wglu_ref, bglu_ref, o_ref):
    nblk = cblk_ref.shape[0]
    half = cblk_ref.shape[1] // 2
    ys = []
    for i in range(nblk):
        cb = cblk_ref[i]
        y = _bdot(bur_s[:, i * half:(i + 1) * half], cb[:half]) + _bdot(bui_s[:, i * half:(i + 1) * half], cb[half:])
        ys.append(y)
    y = jnp.concatenate(ys, axis=1) + d_ref[...] * u_ref[...]
    z = jax.nn.gelu(y)
    z = z * jax.nn.sigmoid(_bdot(z, wglu_ref[...]) + bglu_ref[...])
    o_ref[...] = z.astype(o_ref.dtype)


def _s5_prompt_kernel(u_ref, bblk_ref, cblk_ref, lam_ref, d_ref, wglu_ref, bglu_ref,
                      o_ref, hr_ref, hi_ref, bur_s, bui_s, hc_s):
    j = pl.program_id(1)
    r = u_ref.shape[0]
    ns = bur_s.shape[1]

    @pl.when(j == 0)
    def _():
        hc_s[...] = jnp.zeros_like(hc_s)

    _s5_project_in(u_ref, bblk_ref, bur_s, bui_s)
    cw = min(S5_COL_CHUNK, ns)
    for cc in range(ns // cw):
        cols = slice(cc * cw, (cc + 1) * cw)
        hr, hi = _s5_scan_rows(bur_s, bui_s, lam_ref, cols, 0, r, hc_s[0:1, cols], hc_s[1:2, cols])
        hc_s[0:1, cols] = hr
        hc_s[1:2, cols] = hi
    hr_ref[0] = hc_s[0:1, :]
    hi_ref[0] = hc_s[1:2, :]
    _s5_project_out(u_ref, bur_s, bui_s, cblk_ref, d_ref, wglu_ref, bglu_ref, o_ref)


def _s5_sample_kernel(u_ref, bblk_ref, cblk_ref, lam_ref, d_ref, wglu_ref, bglu_ref, h0r_ref, h0i_ref,
                      o_ref, hr_ref, hi_ref, bur_s, bui_s, *, seq):
    r = u_ref.shape[0]
    ns = bur_s.shape[1]
    _s5_project_in(u_ref, bblk_ref, bur_s, bui_s)
    cw = min(S5_COL_CHUNK, ns)
    for cc in range(ns // cw):
        cols = slice(cc * cw, (cc + 1) * cw)

        def per_seq(b, carry, cols=cols):
            hr, hi = _s5_scan_rows(bur_s, bui_s, lam_ref, cols, b * seq, seq,
                                   h0r_ref[pl.ds(b, 1), cols], h0i_ref[pl.ds(b, 1), cols])
            hr_ref[pl.ds(b, 1), cols] = hr
            hi_ref[pl.ds(b, 1), cols] = hi
            return carry

        lax.fori_loop(0, r // seq, per_seq, 0)
    _s5_project_out(u_ref, bur_s, bui_s, cblk_ref, d_ref, wglu_ref, bglu_ref, o_ref)


def _s5_params(lam_re, lam_im, log_step, b_re, b_im, c_re, c_im):
    g, p = lam_re.shape
    gs = b_re.shape[-1]
    dt = jnp.exp(log_step.astype(F32))[:, None]
    ar, ai = lam_re.astype(F32), lam_im.astype(F32)
    mag = jnp.exp(ar * dt)
    lbr, lbi = mag * jnp.cos(ai * dt), mag * jnp.sin(ai * dt)
    den = ar * ar + ai * ai
    nr, ni = lbr - 1.0, lbi
    cr = (nr * ar + ni * ai) / den
    ci = (ni * ar - nr * ai) / den
    bbr = cr[..., None] * b_re - ci[..., None] * b_im
    bbi = cr[..., None] * b_im + ci[..., None] * b_re
    gb = S5_GROUP_BLOCK
    nblk = g // gb
    eye = jnp.eye(gb, dtype=F32)
    bb = jnp.stack([bbr, bbi]).reshape(2, nblk, gb, p, gs)
    bblk = jnp.einsum('qigpc,gh->igcqhp', bb, eye).reshape(nblk, gb * gs, 2 * gb * p)
    cc = jnp.stack([c_re.astype(F32), -c_im.astype(F32)]).reshape(2, nblk, gb, gs, p)
    cblk = jnp.einsum('qigcp,gh->iqgphc', cc, eye).reshape(nblk, 2 * gb * p, gb * gs)
    lam = jnp.stack([lbr.reshape(-1), lbi.reshape(-1)])
    lam = jnp.concatenate([lam, jnp.zeros((SUBLANES - 2, g * p), F32)], axis=0)
    return bblk, cblk, lam


def _s5_const_specs(bblk, cblk, lam, w, layer, nmap):
    z = lambda n: (lambda *g: (0,) * n)
    return [pl.BlockSpec(bblk.shape, z(3)), pl.BlockSpec(cblk.shape, z(3)), pl.BlockSpec(lam.shape, z(2)),
            pl.BlockSpec((None, 1, w), lambda *g: (layer, 0, 0)),
            pl.BlockSpec((None, w, w), lambda *g: (layer, 0, 0)),
            pl.BlockSpec((None, 1, w), lambda *g: (layer, 0, 0))]


def _s5_prompt(u, bblk, cblk, lam, d, wglu, bglu, layer, *, nb, seq, row0):
    w = u.shape[1]
    ns = lam.shape[1]
    r = min(ROW_TILE, seq)
    nt = seq // r
    t0 = row0 // r
    return pl.pallas_call(
        _s5_prompt_kernel,
        grid=(nb, nt),
        in_specs=[pl.BlockSpec((r, w), lambda b, j: (t0 + b * nt + j, 0))]
        + _s5_const_specs(bblk, cblk, lam, w, layer, 2),
        out_specs=[pl.BlockSpec((r, w), lambda b, j: (b * nt + j, 0)),
                   pl.BlockSpec((1, 1, ns), lambda b, j: (b, 0, 0)),
                   pl.BlockSpec((1, 1, ns), lambda b, j: (b, 0, 0))],
        out_shape=[jax.ShapeDtypeStruct((nb * seq, w), BF16),
                   jax.ShapeDtypeStruct((nb, 1, ns), F32), jax.ShapeDtypeStruct((nb, 1, ns), F32)],
        scratch_shapes=[pltpu.VMEM((r, ns), F32), pltpu.VMEM((r, ns), F32), pltpu.VMEM((SUBLANES, ns), F32)],
        compiler_params=_cparams(("arbitrary", "arbitrary")),
        name="s5_prompt",
    )(u, bblk, cblk, lam, d, wglu, bglu)


def _s5_sample(u, bblk, cblk, lam, d, wglu, bglu, h0r, h0i, layer, *, nb, seq, row0):
    w = u.shape[1]
    ns = lam.shape[1]
    r = min(ROW_TILE, nb * seq)
    nt = nb * seq // r
    spt = r // seq
    t0 = row0 // r
    return pl.pallas_call(
        functools.partial(_s5_sample_kernel, seq=seq),
        grid=(nt,),
        in_specs=[pl.BlockSpec((r, w), lambda i: (t0 + i, 0))]
        + _s5_const_specs(bblk, cblk, lam, w, layer, 1)
        + [pl.BlockSpec((spt, ns), lambda i: (i, 0)), pl.BlockSpec((spt, ns), lambda i: (i, 0))],
        out_specs=[pl.BlockSpec((r, w), lambda i: (i, 0)),
                   pl.BlockSpec((spt, ns), lambda i: (i, 0)),
                   pl.BlockSpec((spt, ns), lambda i: (i, 0))],
        out_shape=[jax.ShapeDtypeStruct((nb * seq, w), BF16),
                   jax.ShapeDtypeStruct((nb, ns), F32), jax.ShapeDtypeStruct((nb, ns), F32)],
        scratch_shapes=[pltpu.VMEM((r, ns), F32), pltpu.VMEM((r, ns), F32)],
        compiler_params=_cparams(("arbitrary",)),
        name="s5_sample",
    )(u, bblk, cblk, lam, d, wglu, bglu, h0r, h0i)


def _top2(x):
    n = x.shape[-1]
    ids = jnp.arange(n, dtype=jnp.int32)
    i1 = jnp.argmax(x, axis=-1).astype(jnp.int32)
    m1 = jnp.max(x, axis=-1)
    rest = jnp.where(ids == i1[..., None], -jnp.inf, x)
    i2 = jnp.argmax(rest, axis=-1).astype(jnp.int32)
    m2 = jnp.max(rest, axis=-1)
    return jnp.stack([m1, m2], -1), jnp.stack([i1, i2], -1)


def _route(probs, n_experts):
    epg = n_experts // N_EXPERT_GROUPS
    grouped = probs.reshape(-1, N_EXPERT_GROUPS, epg)
    group_score = _top2(grouped)[0].sum(-1)
    g_sel = jnp.argmax(group_score, axis=-1).astype(jnp.int32)
    in_group = jnp.sum(jnp.where(jnp.arange(N_EXPERT_GROUPS)[None, :, None] == g_sel[:, None, None], grouped, 0.0),
                       axis=1)
    top_p, top_i = _top2(in_group)
    gate = top_p / jnp.sum(top_p, -1, keepdims=True)
    return g_sel[:, None] * epg + top_i.astype(jnp.int32), gate


def _dispatch(expert_idx, n_experts, bm):
    t = expert_idx.shape[0]
    n_pairs = t * TOP_K
    flat_e = expert_idx.reshape(-1)
    flat_tok = jnp.repeat(jnp.arange(t, dtype=jnp.int32), TOP_K)
    onehot = (flat_e[:, None] == jnp.arange(n_experts, dtype=jnp.int32)[None, :]).astype(jnp.int32)
    csum = jnp.cumsum(onehot, axis=0)
    counts = csum[-1]
    rank = jnp.sum((csum - onehot) * onehot, axis=1)
    padded = (counts + bm - 1) // bm * bm
    pstart = jnp.cumsum(padded) - padded
    pair_slot = jnp.sum(onehot * pstart[None, :], axis=1) + rank
    n_blocks = -(-(n_pairs + n_experts * (bm - 1)) // bm)
    n_slots = n_blocks * bm
    slot_tok = jnp.zeros((n_slots,), jnp.int32).at[pair_slot].set(flat_tok)
    block_exp = jnp.minimum(jnp.searchsorted(jnp.cumsum(padded), jnp.arange(n_blocks, dtype=jnp.int32) * bm,
                                             side='right'), n_experts - 1).astype(jnp.int32)
    n_used = (jnp.sum(padded) // bm).astype(jnp.int32).reshape(1)
    return slot_tok, pair_slot, block_exp, n_used, n_blocks


def _row_copy(src_hbm, dst, src_row, dst_row, sem):
    return pltpu.make_async_copy(src_hbm.at[pl.ds(src_row, 1)], dst.at[pl.ds(dst_row, 1)], sem)


def _gather_rows_kernel(tok_ref, nu_ref, x_hbm, o_ref, buf, sem):
    rows = buf.shape[1]
    i = pl.program_id(0)
    n_live = nu_ref[0]

    def issue(step):
        slot = step % 2

        @pl.when(step < n_live)
        def _():
            def body(r, c):
                _row_copy(x_hbm, buf.at[slot], tok_ref[step * rows + r], r, sem.at[slot]).start()
                return c

            lax.fori_loop(0, rows, body, 0)

    @pl.when(i == 0)
    def _():
        issue(i)

    issue(i + 1)

    @pl.when(i < n_live)
    def _():
        slot = i % 2

        def body(r, c):
            _row_copy(x_hbm, buf.at[slot], 0, r, sem.at[slot]).wait()
            return c

        lax.fori_loop(0, rows, body, 0)
        o_ref[...] = buf[slot].astype(o_ref.dtype)

    @pl.when(i >= n_live)
    def _():
        o_ref[...] = jnp.zeros_like(o_ref)


def _gather_rows(x, slot_tok, n_used, bm):
    n_slots = slot_tok.shape[0]
    d = x.shape[1]
    rows = min(bm, GATHER_ROWS)
    n_live = n_used * (bm // rows)
    return pl.pallas_call(
        _gather_rows_kernel,
        grid_spec=pltpu.PrefetchScalarGridSpec(
            num_scalar_prefetch=2,
            grid=(n_slots // rows,),
            in_specs=[pl.BlockSpec(memory_space=pl.ANY)],
            out_specs=pl.BlockSpec((rows, d), lambda i, tok, nu: (i, 0)),
            scratch_shapes=[pltpu.VMEM((2, rows, d), F32), pltpu.SemaphoreType.DMA((2,))]),
        out_shape=jax.ShapeDtypeStruct((n_slots, d), BF16),
        compiler_params=_cparams(("arbitrary",)),
        name="moe_gather",
    )(slot_tok, n_live, x)


def _expert_changed(be_ref, blk):
    return jnp.logical_or(blk == 0, be_ref[blk] != be_ref[jnp.maximum(blk - 1, 0)])


def _moe_gate_up_kernel(be_ref, nu_ref, x_ref, wg_ref, wu_ref, o_ref, wg_sc, wu_sc):
    blk = pl.program_id(1)

    @pl.when(_expert_changed(be_ref, blk))
    def _():
        wg_sc[...] = wg_ref[...].astype(BF16)
        wu_sc[...] = wu_ref[...].astype(BF16)

    @pl.when(blk < nu_ref[0])
    def _():
        x = x_ref[...]
        g = jnp.dot(x, wg_sc[...], preferred_element_type=F32)
        u = jnp.dot(x, wu_sc[...], preferred_element_type=F32)
        o_ref[...] = (_silu(g) * u).astype(o_ref.dtype)

    @pl.when(blk >= nu_ref[0])
    def _():
        o_ref[...] = jnp.zeros_like(o_ref)


def _moe_down_kernel(be_ref, nu_ref, h_ref, wd_ref, o_ref, wd_sc):
    blk = pl.program_id(1)

    @pl.when(_expert_changed(be_ref, blk))
    def _():
        wd_sc[...] = wd_ref[...].astype(BF16)

    @pl.when(blk < nu_ref[0])
    def _():
        o_ref[...] = jnp.dot(h_ref[...], wd_sc[...], preferred_element_type=F32)

    @pl.when(blk >= nu_ref[0])
    def _():
        o_ref[...] = jnp.zeros_like(o_ref)


def _moe_experts(xs, block_exp, n_used, w_gate, w_up, w_down, layer, bm):
    n_slots, d = xs.shape
    ff = w_gate.shape[-1]
    nblk = n_slots // bm
    tn = _pick(ff, MOE_TN_UP)
    w_in_spec = pl.BlockSpec((None, None, d, tn), lambda j, b, be, nu: (layer, be[b], 0, j))
    hm = pl.pallas_call(
        _moe_gate_up_kernel,
        grid_spec=pltpu.PrefetchScalarGridSpec(
            num_scalar_prefetch=2,
            grid=(ff // tn, nblk),
            in_specs=[pl.BlockSpec((bm, d), lambda j, b, be, nu: (b, 0)), w_in_spec, w_in_spec],
            out_specs=pl.BlockSpec((bm, tn), lambda j, b, be, nu: (b, j)),
            scratch_shapes=[pltpu.VMEM((d, tn), BF16), pltpu.VMEM((d, tn), BF16)]),
        out_shape=jax.ShapeDtypeStruct((n_slots, ff), BF16),
        compiler_params=_cparams(("arbitrary", "arbitrary")),
        name="moe_gate_up",
    )(block_exp, n_used, xs, w_gate, w_up)
    tn2 = _pick(d, MOE_TN_DOWN)
    return pl.pallas_call(
        _moe_down_kernel,
        grid_spec=pltpu.PrefetchScalarGridSpec(
            num_scalar_prefetch=2,
            grid=(d // tn2, nblk),
            in_specs=[pl.BlockSpec((bm, ff), lambda j, b, be, nu: (b, 0)),
                      pl.BlockSpec((None, None, ff, tn2), lambda j, b, be, nu: (layer, be[b], 0, j))],
            out_specs=pl.BlockSpec((bm, tn2), lambda j, b, be, nu: (b, j)),
            scratch_shapes=[pltpu.VMEM((ff, tn2), BF16)]),
        out_shape=jax.ShapeDtypeStruct((n_slots, d), F32),
        compiler_params=_cparams(("arbitrary", "arbitrary")),
        name="moe_down",
    )(block_exp, n_used, hm, w_down)


def _combine_ln_kernel(slot_ref, x_ref, w_ref, g_ref, b_ref, yb_hbm, of_ref, ob_ref, buf, sem, *, alpha):
    tm = x_ref.shape[0]
    i = pl.program_id(0)

    def issue(step):
        slot = step % 2

        def body(r, c):
            for k in range(TOP_K):
                _row_copy(yb_hbm, buf.at[slot, k], slot_ref[(step * tm + r) * TOP_K + k], r, sem.at[slot]).start()
            return c

        lax.fori_loop(0, tm, body, 0)

    @pl.when(i == 0)
    def _():
        issue(i)

    @pl.when(i + 1 < pl.num_programs(0))
    def _():
        issue(i + 1)

    slot = i % 2

    def wait(r, c):
        for k in range(TOP_K):
            _row_copy(yb_hbm, buf.at[slot, k], 0, r, sem.at[slot]).wait()
        return c

    lax.fori_loop(0, tm, wait, 0)
    w = w_ref[...]
    f = w[:, 0:1] * buf[slot, 0] + w[:, 1:2] * buf[slot, 1]
    z = _ln_math(alpha * x_ref[...] + f, g_ref[...], b_ref[...])
    of_ref[...] = z
    ob_ref[...] = z.astype(BF16)


def _combine_ln(x, yb, pair_slot, gate, g, b, alpha, layer):
    t, d = x.shape
    tm = min(128, t)
    return pl.pallas_call(
        functools.partial(_combine_ln_kernel, alpha=alpha),
        grid_spec=pltpu.PrefetchScalarGridSpec(
            num_scalar_prefetch=1,
            grid=(t // tm,),
            in_specs=[pl.BlockSpec((tm, d), lambda i, s: (i, 0)),
                      pl.BlockSpec((tm, TOP_K), lambda i, s: (i, 0)),
                      pl.BlockSpec((None, 1, d), lambda i, s: (layer, 0, 0)),
                      pl.BlockSpec((None, 1, d), lambda i, s: (layer, 0, 0)),
                      pl.BlockSpec(memory_space=pl.ANY)],
            out_specs=[pl.BlockSpec((tm, d), lambda i, s: (i, 0)),
                       pl.BlockSpec((tm, d), lambda i, s: (i, 0))],
            scratch_shapes=[pltpu.VMEM((2, TOP_K, tm, d), F32), pltpu.SemaphoreType.DMA((2,))]),
        out_shape=[jax.ShapeDtypeStruct((t, d), F32), jax.ShapeDtypeStruct((t, d), BF16)],
        compiler_params=_cparams(("arbitrary",)),
        name="moe_combine_ln2",
    )(pair_slot, x, gate, g, b, yb)


def kernel(x_prompt, x_sample, state_delta, state_conv, state_s5, state_ret, ln_in_g, ln_in_b, w_in, dn_conv_w, dn_a_log, dn_dt_bias, dn_norm_w, s5_lam_re, s5_lam_im, s5_log_step, s5_b_re, s5_b_im, s5_c_re, s5_c_im, s5_d, s5_w_glu, s5_b_glu, w_up_dn, w_up_s5, w_up_ret, w_o, ln1_g, ln1_b, router_w, router_b, w_gate_e, w_up_e, w_down_e, ln2_g, ln2_b):
    bp, lp, d = x_prompt.shape
    bs, ls, _ = x_sample.shape
    depth = w_in.shape[0]
    nh, dk, dv = state_delta.shape[2:]
    qkv = dn_conv_w.shape[2]
    hist = state_conv.shape[2]
    dnw = nh * dv
    s5w = s5_d.shape[1]
    g5, p5 = s5_lam_re.shape[1:]
    rh, rdk, rdv = state_ret.shape[2:]
    rw = rh * rdv
    n_exp = router_w.shape[1]
    tp, ts = bp * lp, bs * ls
    t = tp + ts
    alpha = (2 * depth) ** 0.25
    assert ls == SUBLANES and qkv == 3 * dnw and dk == dv == rdk == rdv == LANES

    sizes = (qkv, dnw, nh, nh, s5w, rh * rdk, rh * rdk, rw, rw, 3 * d)
    offs = [0]
    for s in sizes:
        offs.append(offs[-1] + s)
    o_dn, o_b, o_s5, o_ret, o_gates = offs[0], offs[2], offs[4], offs[5], offs[9]
    assert w_in.shape[2] == offs[-1] and 2 * nh < LANES
    w_in_t = jnp.swapaxes(w_in, 1, 2)

    xf, xb = _ln_in(x_prompt.reshape(tp, d), x_sample.reshape(ts, d), ln_in_g, ln_in_b)

    ep = -(-n_exp // LANES) * LANES
    rw_pad = jnp.pad(router_w.astype(F32), ((0, 0), (0, ep - n_exp)))
    rb_pad = jnp.pad(router_b.astype(F32).reshape(1, n_exp), ((0, 0), (0, ep - n_exp)), constant_values=-1e30)

    cs_p, sn_p = _rope_tables(jnp.arange(lp, dtype=jnp.int32), rdk)
    cs_s, sn_s = _rope_tables(PAST_LEN + jnp.arange(ls, dtype=jnp.int32), rdk)
    rs = min(ROW_TILE, ts)
    cs_s = jnp.tile(cs_s, (rs // ls, 1))
    sn_s = jnp.tile(sn_s, (rs // ls, 1))

    new_delta_p, new_conv_p, new_s5_p, new_ret_p = [], [], [], []
    new_delta_s, new_conv_s, new_s5_s, new_ret_s = [], [], [], []

    for l in range(depth):
        p_dn = _matmul_nt(xb, w_in_t, layer=l, row0=o_dn, n=qkv + dnw, tile_major=True,
                          tn_pref=min(MIXER_TN, dnw), name="proj_dn")
        ba = _matmul_nt(xb, w_in_t, layer=l, row0=o_b, n=LANES, name="proj_ba")
        u5 = _matmul_nt(xb, w_in_t, layer=l, row0=o_s5, n=s5w, name="proj_s5")
        p_ret = _matmul_nt(xb, w_in_t, layer=l, row0=o_ret, n=4 * rw, tile_major=True,
                           tn_pref=min(MIXER_TN, rw), name="proj_ret")
        gates = _matmul_nt(xb, w_in_t, layer=l, row0=o_gates, n=3 * d, tn_pref=512, name="proj_gates")

        tn = p_dn.shape[-1]
        bat = ba[:, :2 * nh].T
        zpad = jnp.zeros((nh,), F32)
        hpr = jnp.pad(jnp.stack([jnp.concatenate([zpad, dn_a_log[l].astype(F32)]),
                                 jnp.concatenate([zpad, dn_dt_bias[l].astype(F32)])]),
                      ((0, 0), (0, LANES - 2 * nh)))
        hpc = hpr[:, :2 * nh].T
        norm_w = dn_norm_w.reshape(depth, 1, dv)

        gates_p = _delta_gate(ba, bat, hpr, hpc, nh=nh, c=math.gcd(lp, CHUNK), row0=0, nrows=tp)
        gates_s = _delta_gate(ba, bat, hpr, hpc, nh=nh, c=ls, row0=tp, nrows=ts)
        odn_p, dlt_p = _delta_prompt(p_dn, gates_p, dn_conv_w, norm_w, l,
                                     nb=bp, seq=lp, nh=nh, dk=dk, dv=dv, row0=0)
        bufx = jnp.pad(state_conv[l], ((0, 0), (0, ls - hist), (0, 0))).reshape(ts, qkv)
        odn_s, dlt_s = _delta_sample(p_dn, gates_s, dn_conv_w, norm_w, bufx, state_delta, l,
                                     nb=bs, seq=ls, nh=nh, dk=dk, dv=dv, row0=tp, hist=hist)
        nq = qkv // tn
        cp = jnp.stack([lax.slice(p_dn, (0, b * lp + lp - hist, 0), (nq, (b + 1) * lp, tn))
                        for b in range(bp)], axis=1)
        cs_ = lax.slice(p_dn, (0, tp, 0), (nq, t, tn)).reshape(nq, bs, ls, tn)[:, :, ls - hist:]
        new_conv_p.append(jnp.moveaxis(cp, 0, 2).reshape(bp, hist, qkv))
        new_conv_s.append(jnp.moveaxis(cs_, 0, 2).reshape(bs, hist, qkv))
        new_delta_p.append(dlt_p)
        new_delta_s.append(dlt_s)

        bblk, cblk, lam = _s5_params(s5_lam_re[l], s5_lam_im[l], s5_log_step[l], s5_b_re[l], s5_b_im[l],
                                     s5_c_re[l], s5_c_im[l])
        d5 = s5_d.reshape(depth, 1, s5w)
        bg5 = s5_b_glu.reshape(depth, 1, s5w)
        os5_p, hr_p, hi_p = _s5_prompt(u5, bblk, cblk, lam, d5, s5_w_glu, bg5, l, nb=bp, seq=lp, row0=0)
        h0 = state_s5[l].astype(F32)
        os5_s, hr_s, hi_s = _s5_sample(u5, bblk, cblk, lam, d5, s5_w_glu, bg5,
                                       h0[..., 0].reshape(bs, g5 * p5), h0[..., 1].reshape(bs, g5 * p5), l,
                                       nb=bs, seq=ls, row0=tp)
        new_s5_p.append(jnp.stack([hr_p.reshape(bp, g5, p5), hi_p.reshape(bp, g5, p5)], -1))
        new_s5_s.append(jnp.stack([hr_s.reshape(bs, g5, p5), hi_s.reshape(bs, g5, p5)], -1))

        oret_p, rt_p = _ret_prompt(p_ret, cs_p, sn_p, nb=bp, seq=lp, nh=rh, dk=rdk, dv=rdv, row0=0)
        oret_s, rt_s = _ret_sample(p_ret, cs_s, sn_s, state_ret, l, nb=bs, seq=ls, nh=rh, dk=rdk, dv=rdv,
                                   row0=tp)
        new_ret_p.append(rt_p)
        new_ret_s.append(rt_s)

        merged = _upmerge((odn_p, os5_p, oret_p), (odn_s, os5_s, oret_s), gates, w_up_dn, w_up_s5, w_up_ret, l)
        mix = _matmul(merged, w_o, layer=l, tn_pref=512, name="w_o")
        xf, xb, probs = _ln1_router(xf, mix, ln1_g.reshape(depth, 1, d), ln1_b.reshape(depth, 1, d),
                                    rw_pad, rb_pad, alpha, l)

        expert_idx, gate = _route(probs[:, :n_exp], n_exp)
        slot_tok, pair_slot, block_exp, n_used, _ = _dispatch(expert_idx, n_exp, MOE_BM)
        xs = _gather_rows(xf, slot_tok, n_used, MOE_BM)
        yb = _moe_experts(xs, block_exp, n_used, w_gate_e, w_up_e, w_down_e, l, MOE_BM)
        xf, xb = _combine_ln(xf, yb, pair_slot, gate, ln2_g.reshape(depth, 1, d), ln2_b.reshape(depth, 1, d),
                             alpha, l)

    y_prompt = xf[:tp].reshape(bp, lp, d)
    y_sample = xf[tp:].reshape(bs, ls, d)
    st = lambda xs_, ref: jnp.stack(xs_).astype(ref.dtype)
    return (y_prompt, y_sample,
            st(new_delta_p, state_delta), st(new_conv_p, state_conv), st(new_s5_p, state_s5),
            st(new_ret_p, state_ret),
            st(new_delta_s, state_delta), st(new_conv_s, state_conv), st(new_s5_s, state_s5),
            st(new_ret_s, state_ret))
```

```python
import functools
import math

import jax
import jax.numpy as jnp
from jax import lax
from jax.experimental import pallas as pl
from jax.experimental.pallas import tpu as pltpu

F32 = jnp.float32
BF16 = jnp.bfloat16
HI = lax.Precision.HIGHEST

LANES = 128
SUBLANES = 8
VMEM_LIMIT = 56 * 1024 * 1024

LN_EPS = 1e-5
RMS_EPS = 1e-6
ROPE_BASE = 10000.0
PAST_LEN = 16384
CHUNK = 64
N_EXPERT_GROUPS = 4
TOP_K = 2
S5_GROUP_BLOCK = 8

ROW_TILE = 256
MM_TM = 1024
MM_TN = 256
MIXER_TN = 512
MOE_BM = 256
MOE_TN_UP = 512
MOE_TN_DOWN = 1024
GATHER_ROWS = 128
SAMPLE_GROUP = 8


def _cparams(sem):
    return pltpu.CompilerParams(dimension_semantics=sem, vmem_limit_bytes=VMEM_LIMIT)


def _bdot(a, b):
    return jnp.dot(a.astype(BF16), b.astype(BF16), preferred_element_type=F32)


def _bdot_nt(a, b):
    return lax.dot_general(a.astype(BF16), b.astype(BF16), (((1,), (1,)), ((), ())),
                           preferred_element_type=F32)


def _bdot_tn(a, b):
    return lax.dot_general(a.astype(BF16), b.astype(BF16), (((0,), (0,)), ((), ())),
                           preferred_element_type=F32)


def _hdot(a, b):
    return jnp.dot(a, b, precision=HI, preferred_element_type=F32)


def _ln_math(x, g, b):
    mu = jnp.mean(x, -1, keepdims=True)
    xc = x - mu
    var = jnp.mean(xc * xc, -1, keepdims=True)
    return xc * lax.rsqrt(var + LN_EPS) * g + b


def _ln_in_kernel(xp_ref, xs_ref, g_ref, b_ref, of_ref, ob_ref, *, n_prompt_tiles):
    i = pl.program_id(0)

    def run(src):
        y = _ln_math(src[...], g_ref[...], b_ref[...])
        of_ref[...] = y
        ob_ref[...] = y.astype(BF16)

    @pl.when(i < n_prompt_tiles)
    def _():
        run(xp_ref)

    @pl.when(i >= n_prompt_tiles)
    def _():
        run(xs_ref)


def _ln_in(xp, xs, g, b):
    tp, d = xp.shape
    ts = xs.shape[0]
    tm = ROW_TILE
    npt, nst = tp // tm, ts // tm
    t = tp + ts
    return pl.pallas_call(
        functools.partial(_ln_in_kernel, n_prompt_tiles=npt),
        grid=(npt + nst,),
        in_specs=[pl.BlockSpec((tm, d), lambda i: (jnp.minimum(i, npt - 1), 0)),
                  pl.BlockSpec((tm, d), lambda i: (jnp.maximum(i - npt, 0), 0)),
                  pl.BlockSpec((1, d), lambda i: (0, 0)),
                  pl.BlockSpec((1, d), lambda i: (0, 0))],
        out_specs=[pl.BlockSpec((tm, d), lambda i: (i, 0)),
                   pl.BlockSpec((tm, d), lambda i: (i, 0))],
        out_shape=[jax.ShapeDtypeStruct((t, d), F32), jax.ShapeDtypeStruct((t, d), BF16)],
        compiler_params=_cparams(("arbitrary",)),
        name="ln_in",
    )(xp, xs, g.reshape(1, d), b.reshape(1, d))


def _ln1_router_kernel(x_ref, y_ref, g_ref, b_ref, rw_ref, rb_ref, of_ref, ob_ref, pr_ref, *, alpha):
    z = _ln_math(alpha * x_ref[...] + y_ref[...], g_ref[...], b_ref[...])
    of_ref[...] = z
    ob_ref[...] = z.astype(BF16)
    logits = _hdot(z, rw_ref[...]) + rb_ref[...]
    m = jnp.max(logits, -1, keepdims=True)
    e = jnp.exp(logits - m)
    pr_ref[...] = e / jnp.sum(e, -1, keepdims=True)


def _ln1_router(x, y, g, b, rw, rb, alpha, layer):
    t, d = x.shape
    tm = ROW_TILE
    ep = rw.shape[1]
    return pl.pallas_call(
        functools.partial(_ln1_router_kernel, alpha=alpha),
        grid=(t // tm,),
        in_specs=[pl.BlockSpec((tm, d), lambda i: (i, 0)),
                  pl.BlockSpec((tm, d), lambda i: (i, 0)),
                  pl.BlockSpec((None, 1, d), lambda i: (layer, 0, 0)),
                  pl.BlockSpec((None, 1, d), lambda i: (layer, 0, 0)),
                  pl.BlockSpec((d, ep), lambda i: (0, 0)),
                  pl.BlockSpec((1, ep), lambda i: (0, 0))],
        out_specs=[pl.BlockSpec((tm, d), lambda i: (i, 0)),
                   pl.BlockSpec((tm, d), lambda i: (i, 0)),
                   pl.BlockSpec((tm, ep), lambda i: (i, 0))],
        out_shape=[jax.ShapeDtypeStruct((t, d), F32), jax.ShapeDtypeStruct((t, d), BF16),
                   jax.ShapeDtypeStruct((t, ep), F32)],
        compiler_params=_cparams(("arbitrary",)),
        name="ln1_router",
    )(x, y, g, b, rw, rb)


def _mm_kernel(a_ref, w_ref, o_ref):
    o_ref[...] = jnp.dot(a_ref[...], w_ref[...].astype(BF16),
                         preferred_element_type=F32).astype(o_ref.dtype)


def _pick(n, pref):
    for c in (pref, 512, 256, 128):
        if c <= pref and n % c == 0:
            return c
    return n


def _mm_nt_kernel(a_ref, wt_ref, o_ref):
    o_ref[...] = lax.dot_general(a_ref[...], wt_ref[0].astype(BF16), (((1,), (1,)), ((), ())),
                                 preferred_element_type=F32).astype(o_ref.dtype)


def _matmul_nt(a, wt, *, layer, row0, n, tile_major=False, tn_pref=MM_TN, name="mm_nt"):
    m, k = a.shape
    tm = _pick(m, MM_TM)
    tn = _pick(n, tn_pref)
    assert row0 % SUBLANES == 0
    w_spec = pl.BlockSpec((pl.Element(1), pl.Element(tn), pl.Element(k)),
                          lambda i, j: (layer, pl.multiple_of(row0 + j * tn, SUBLANES), 0))
    if tile_major:
        out_spec = pl.BlockSpec((None, tm, tn), lambda i, j: (j, i, 0))
        out_shape = jax.ShapeDtypeStruct((n // tn, m, tn), F32)
    else:
        out_spec = pl.BlockSpec((tm, tn), lambda i, j: (i, j))
        out_shape = jax.ShapeDtypeStruct((m, n), F32)
    return pl.pallas_call(
        _mm_nt_kernel,
        grid=(m // tm, n // tn),
        in_specs=[pl.BlockSpec((tm, k), lambda i, j: (i, 0)), w_spec],
        out_specs=out_spec,
        out_shape=out_shape,
        compiler_params=_cparams(("arbitrary", "arbitrary")),
        name=name,
    )(a, wt)


def _matmul(a, w, *, layer=None, col0=0, n=None, tile_major=False, out_dtype=F32, tn_pref=MM_TN, name="mm"):
    m, k = a.shape
    n = w.shape[-1] if n is None else n
    tm = _pick(m, MM_TM)
    tn = _pick(n, tn_pref)
    assert col0 % tn == 0
    j0 = col0 // tn
    if w.ndim == 3:
        w_spec = pl.BlockSpec((None, k, tn), lambda i, j: (layer, 0, j0 + j))
    else:
        w_spec = pl.BlockSpec((k, tn), lambda i, j: (0, j0 + j))
    if tile_major:
        out_spec = pl.BlockSpec((None, tm, tn), lambda i, j: (j, i, 0))
        out_shape = jax.ShapeDtypeStruct((n // tn, m, tn), out_dtype)
    else:
        out_spec = pl.BlockSpec((tm, tn), lambda i, j: (i, j))
        out_shape = jax.ShapeDtypeStruct((m, n), out_dtype)
    return pl.pallas_call(
        _mm_kernel,
        grid=(m // tm, n // tn),
        in_specs=[pl.BlockSpec((tm, k), lambda i, j: (i, 0)), w_spec],
        out_specs=out_spec,
        out_shape=out_shape,
        compiler_params=_cparams(("arbitrary", "arbitrary")),
        name=name,
    )(a, w)


def _upmerge_kernel(odn_p, os5_p, oret_p, odn_s, os5_s, oret_s, gdn, gs5, gret, wdn, ws5, wret, o_ref,
                    *, n_prompt_tiles):
    i = pl.program_id(0)

    def branch(o, w, g):
        return jax.nn.sigmoid(g[...]) * jnp.dot(o[...], w[...].astype(BF16), preferred_element_type=F32)

    def run(odn, os5, oret):
        o_ref[...] = (branch(odn, wdn, gdn) + branch(os5, ws5, gs5) + branch(oret, wret, gret)).astype(o_ref.dtype)

    @pl.when(i < n_prompt_tiles)
    def _():
        run(odn_p, os5_p, oret_p)

    @pl.when(i >= n_prompt_tiles)
    def _():
        run(odn_s, os5_s, oret_s)


def _upmerge(outs_p, outs_s, gates, w_dn, w_s5, w_ret, layer):
    tp, ts = outs_p[0].shape[0], outs_s[0].shape[0]
    t = tp + ts
    d = w_dn.shape[-1]
    tm = _pick(math.gcd(tp, ts), MM_TM)
    tn = _pick(d, MM_TN)
    nj = d // tn
    npt = tp // tm
    nst = ts // tm

    def p_spec(a):
        return pl.BlockSpec((tm, a.shape[1]), lambda i, j: (jnp.minimum(i, npt - 1), 0))

    def s_spec(a):
        return pl.BlockSpec((tm, a.shape[1]), lambda i, j: (jnp.maximum(i - npt, 0), 0))

    def g_spec(off):
        return pl.BlockSpec((tm, tn), lambda i, j: (i, off * nj + j))

    def w_spec(w):
        return pl.BlockSpec((None, w.shape[1], tn), lambda i, j: (layer, 0, j))

    return pl.pallas_call(
        functools.partial(_upmerge_kernel, n_prompt_tiles=npt),
        grid=(npt + nst, nj),
        in_specs=[p_spec(a) for a in outs_p] + [s_spec(a) for a in outs_s]
        + [g_spec(0), g_spec(1), g_spec(2), w_spec(w_dn), w_spec(w_s5), w_spec(w_ret)],
        out_specs=pl.BlockSpec((tm, tn), lambda i, j: (i, j)),
        out_shape=jax.ShapeDtypeStruct((t, d), BF16),
        compiler_params=_cparams(("arbitrary", "arbitrary")),
        name="upmerge",
    )(*outs_p, *outs_s, gates, gates, gates, w_dn, w_s5, w_ret)


def _chunk_masks(r, c):
    ti = lax.broadcasted_iota(jnp.int32, (r, r), 0)
    si = lax.broadcasted_iota(jnp.int32, (r, r), 1)
    same = (ti // c) == (si // c)
    return same, same & (si <= ti), same & (si < ti)


def _shift_rows_carry(x, prev8, s):
    rolled = pltpu.roll(x, s, 0)
    prev_rolled = pltpu.roll(prev8, s, 0)
    row8 = lax.broadcasted_iota(jnp.int32, prev8.shape, 0)
    first = jnp.where(row8 < s, prev_rolled, rolled[0:SUBLANES])
    return jnp.concatenate([first, rolled[SUBLANES:]], axis=0)


def _shift_rows_seq8(x, bufx, s, hist):
    r = x.shape[0]
    rolled = pltpu.roll(x, s, 0)
    brolled = pltpu.roll(bufx, (r - (hist - s)) % r, 0)
    t8 = lax.broadcasted_iota(jnp.int32, x.shape, 0) % SUBLANES
    return jnp.where(t8 < s, brolled, rolled)


def _causal_conv(x, w_ref, shift):
    n = w_ref.shape[0]
    y = w_ref[n - 1:n, :] * x
    for s in range(1, n):
        y = y + w_ref[n - 1 - s:n - s, :] * shift(s)
    return y


def _silu(x):
    return x * jax.nn.sigmoid(x)


def _l2norm(x):
    return x * lax.rsqrt(jnp.sum(x * x, -1, keepdims=True) + RMS_EPS)


def _rms(x):
    return x * lax.rsqrt(jnp.mean(x * x, -1, keepdims=True) + RMS_EPS)


def _split3(x):
    h = x.astype(BF16)
    r1 = x - h.astype(F32)
    m = r1.astype(BF16)
    l = (r1 - m.astype(F32)).astype(BF16)
    return h, m, l


def _delta_gate_kernel(ba_ref, bat_ref, hpr_ref, hpc_ref, beta_ref, gcum_ref, gtot_ref, rows_ref, *, nh, c):
    r = ba_ref.shape[0]
    same, incl, _ = _chunk_masks(r, c)
    ti = lax.broadcasted_iota(jnp.int32, (r, r), 0)
    si = lax.broadcasted_iota(jnp.int32, (r, r), 1)
    incl_t = (same & (ti <= si)).astype(BF16)
    inclb = incl.astype(BF16)
    sameb = same.astype(BF16)
    dot = functools.partial(jnp.dot, preferred_element_type=F32)
    ba = ba_ref[...]
    lane = lax.broadcasted_iota(jnp.int32, ba.shape, 1)
    beta_ref[...] = jax.nn.sigmoid(ba)
    g_cols = -jnp.exp(hpr_ref[0:1, :]) * jax.nn.softplus(ba + hpr_ref[1:2, :])
    g_cols = jnp.where((lane >= nh) & (lane < 2 * nh), g_cols, 0.0)
    h, m, l = _split3(g_cols)
    gcum_ref[...] = dot(inclb, h) + dot(inclb, m) + dot(inclb, l)
    gtot_ref[...] = dot(sameb, h) + dot(sameb, m) + dot(sameb, l)
    g_rows = -jnp.exp(hpc_ref[:, 0:1]) * jax.nn.softplus(bat_ref[...] + hpc_ref[:, 1:2])
    h, m, l = _split3(g_rows)
    rows_ref[...] = dot(h, incl_t) + dot(m, incl_t) + dot(l, incl_t)


def _delta_gate(ba, bat, hpr, hpc, *, nh, c, row0, nrows):
    r = min(ROW_TILE, nrows)
    t0 = row0 // r
    nh2 = 2 * nh
    col = pl.BlockSpec((r, LANES), lambda i: (i, 0))
    return pl.pallas_call(
        functools.partial(_delta_gate_kernel, nh=nh, c=c),
        grid=(nrows // r,),
        in_specs=[pl.BlockSpec((r, LANES), lambda i: (t0 + i, 0)),
                  pl.BlockSpec((nh2, r), lambda i: (0, t0 + i)),
                  pl.BlockSpec((2, LANES), lambda i: (0, 0)),
                  pl.BlockSpec((nh2, 2), lambda i: (0, 0))],
        out_specs=[col, col, col, pl.BlockSpec((nh2, r), lambda i: (0, i))],
        out_shape=[jax.ShapeDtypeStruct((nrows, LANES), F32)] * 3 + [jax.ShapeDtypeStruct((nh2, nrows), F32)],
        compiler_params=_cparams(("arbitrary",)),
        name="delta_gate",
    )(ba, bat, hpr, hpc)


def _pick_col(cols, idx):
    lane = lax.broadcasted_iota(jnp.int32, cols.shape, 1)
    return jnp.sum(jnp.where(lane == idx, cols, 0.0), -1, keepdims=True)


def _delta_prep(heads, c):
    r, dk = heads[0][1].shape
    _, incl, strict = _chunk_masks(r, c)
    ms, as_ = [], []
    for q, k, v, beta_c, gc_c, gt_c, gc_r in heads:
        dec = jnp.exp(jnp.minimum(gc_c - gc_r, 0.0))
        ms.append(jnp.where(strict, beta_c * dec * _bdot_nt(k, k), 0.0))
        as_.append(jnp.where(incl, dec * _bdot_nt(q, k), 0.0))
    eye = (lax.broadcasted_iota(jnp.int32, (r, r), 0) == lax.broadcasted_iota(jnp.int32, (r, r), 1)).astype(F32)
    xs = [eye - m for m in ms]
    pws = ms
    for _ in range(int(math.log2(c)) - 1):
        pws = [_bdot(pw, pw) for pw in pws]
        xs = [x + _bdot(x, pw) for x, pw in zip(xs, pws)]
    out = []
    for (q, k, v, beta_c, gc_c, gt_c, gc_r), a, x in zip(heads, as_, xs):
        gin = jnp.exp(gc_c)
        wu = _bdot(x, jnp.concatenate([beta_c * gin * k, beta_c * v], axis=1))
        out.append((a, wu[:, :dk], wu[:, dk:], k * jnp.exp(gt_c - gc_c), jnp.exp(gt_c), gin))
    return out


def _delta_head_inputs(qa, ka, va, beta_ref, gcum_ref, gtot_ref, rows_ref, i, h, nh, dk):
    cols = slice(i * dk, (i + 1) * dk)
    q = _l2norm(qa[:, cols]) * (dk ** -0.5)
    k = _l2norm(ka[:, cols])
    beta_c = _pick_col(beta_ref[...], h)
    gc_c = _pick_col(gcum_ref[...], nh + h)
    gt_c = _pick_col(gtot_ref[...], nh + h)
    gc_r = rows_ref[pl.ds(nh + h, 1), :]
    return q, k, va[:, cols], beta_c, gc_c, gt_c, gc_r


def _delta_finish(a, u, qs, z, nw_ref):
    o = qs + _bdot(a, u)
    return _rms(o) * nw_ref[...] * _silu(z)


def _delta_prompt_kernel(q_ref, k_ref, v_ref, z_ref, beta_ref, gcum_ref, gtot_ref, rows_ref,
                         cwq_ref, cwk_ref, cwv_ref, nw_ref,
                         o_ref, st_ref,
                         s_sc, pq_sc, pk_sc, pv_sc,
                         *, nh, hb, c, dk):
    p = pl.program_id(1)
    j = pl.program_id(2)
    r = q_ref.shape[0]

    @pl.when(j == 0)
    def _():
        s_sc[...] = jnp.zeros_like(s_sc)
        pq_sc[...] = jnp.zeros_like(pq_sc)
        pk_sc[...] = jnp.zeros_like(pk_sc)
        pv_sc[...] = jnp.zeros_like(pv_sc)

    def conv(x_ref, w_ref, prev_sc):
        x = x_ref[...]
        prev8 = prev_sc[...]
        y = _causal_conv(x, w_ref, lambda s: _shift_rows_carry(x, prev8, s))
        prev_sc[...] = x[r - SUBLANES:, :]
        return _silu(y)

    qa = conv(q_ref, cwq_ref, pq_sc)
    ka = conv(k_ref, cwk_ref, pk_sc)
    va = conv(v_ref, cwv_ref, pv_sc)
    z = z_ref[...]

    heads = [_delta_head_inputs(qa, ka, va, beta_ref, gcum_ref, gtot_ref, rows_ref, i, p * hb + i, nh, dk)
             for i in range(hb)]
    preps = _delta_prep(heads, c)
    nchunk = r // c
    chunk_rows = [slice(n * c, (n + 1) * c) for n in range(nchunk)]
    kws = [[_bdot_tn(pr[3][rows], pr[1][rows]) for rows in chunk_rows] for pr in preps]
    bcs = [[_bdot_tn(pr[3][rows], pr[2][rows]) for rows in chunk_rows] for pr in preps]
    ss = [s_sc[i] for i in range(hb)]
    us = [[] for _ in range(hb)]
    qss = [[] for _ in range(hb)]
    for n, rows in enumerate(chunk_rows):
        for i in range(hb):
            q = heads[i][0]
            _, w, u0, kd, eg, gin = preps[i]
            s = ss[i]
            us[i].append(u0[rows] - _bdot(w[rows], s))
            qss[i].append(gin[rows] * _bdot(q[rows], s))
            ss[i] = (eg[n * c:n * c + 1] * s + bcs[i][n]) - _bdot(kws[i][n], s)
    for i in range(hb):
        s_sc[i] = ss[i]
        st_ref[0, i] = ss[i]
        cols = slice(i * dk, (i + 1) * dk)
        o = _delta_finish(preps[i][0], jnp.concatenate(us[i], axis=0), jnp.concatenate(qss[i], axis=0),
                          z[:, cols], nw_ref)
        o_ref[:, cols] = o.astype(o_ref.dtype)


def _delta_sample_kernel(q_ref, k_ref, v_ref, z_ref, beta_ref, gcum_ref, gtot_ref, rows_ref,
                         cwq_ref, cwk_ref, cwv_ref, nw_ref, bq_ref, bk_ref, bv_ref, sin_ref,
                         o_ref, st_ref,
                         q_s, w_s, u0_s, kd_s, eg_s, gin_s, u_s, qs_s,
                         *, nh, hb, c, dk, hist):
    p = pl.program_id(1)
    r = q_ref.shape[0]

    def conv(x_ref, w_ref, b_ref):
        x = x_ref[...]
        bufx = b_ref[...]
        return _silu(_causal_conv(x, w_ref, lambda s: _shift_rows_seq8(x, bufx, s, hist)))

    qa = conv(q_ref, cwq_ref, bq_ref)
    ka = conv(k_ref, cwk_ref, bk_ref)
    va = conv(v_ref, cwv_ref, bv_ref)
    z = z_ref[...]

    heads = [_delta_head_inputs(qa, ka, va, beta_ref, gcum_ref, gtot_ref, rows_ref, i, p * hb + i, nh, dk)
             for i in range(hb)]
    for i in range(hb):
        q = heads[i][0]
        a, w, u0, kd, eg, gin = _delta_prep(heads[i:i + 1], c)[0]
        q_s[...] = q
        w_s[...] = w
        u0_s[...] = u0
        kd_s[...] = kd
        eg_s[...] = jnp.broadcast_to(eg, eg_s.shape)
        gin_s[...] = jnp.broadcast_to(gin, gin_s.shape)

        def group(gi, carry, i=i):
            seqs = [gi * SAMPLE_GROUP + j for j in range(SAMPLE_GROUP)]
            rows = [pl.ds(pl.multiple_of(n * c, c), c) for n in seqs]
            ss = [sin_ref[n, i] for n in seqs]
            wqs = [_bdot(jnp.concatenate([w_s[rw, :], q_s[rw, :]], axis=0), s) for rw, s in zip(rows, ss)]
            us = [u0_s[rw, :] - wq[:c] for rw, wq in zip(rows, wqs)]
            for rw, u, wq in zip(rows, us, wqs):
                u_s[rw, :] = u
                qs_s[rw, :] = gin_s[rw, :] * wq[c:]
            for n, rw, s, u in zip(seqs, rows, ss, us):
                st_ref[n, i] = eg_s[pl.ds(pl.multiple_of(n * c, c), 1), :] * s + _bdot_tn(kd_s[rw, :], u)
            return carry

        lax.fori_loop(0, r // c // SAMPLE_GROUP, group, 0)
        cols = slice(i * dk, (i + 1) * dk)
        o_ref[:, cols] = _delta_finish(a, u_s[...], qs_s[...], z[:, cols], nw_ref).astype(o_ref.dtype)


def _delta_scratch(r, dk, dv):
    return [pltpu.VMEM((r, dk), F32),
            pltpu.VMEM((r, dk), F32),
            pltpu.VMEM((r, dv), F32),
            pltpu.VMEM((r, dk), F32),
            pltpu.VMEM((r, dv), F32),
            pltpu.VMEM((r, dv), F32),
            pltpu.VMEM((r, dv), F32),
            pltpu.VMEM((r, dv), F32)]


def _delta_common_specs(tiles_per_seg, row_map, local_map, taps, dv, layer, tn, r, nh2):
    def seg(n):
        return pl.BlockSpec((None, r, tn), lambda *g: (n * tiles_per_seg + g[1], row_map(*g), 0))

    def cw(n):
        return pl.BlockSpec((None, taps, tn), lambda *g: (layer, 0, n * tiles_per_seg + g[1]))

    col = pl.BlockSpec((r, LANES), lambda *g: (local_map(*g), 0))
    return [seg(0), seg(1), seg(2), seg(3), col, col, col,
            pl.BlockSpec((nh2, r), lambda *g: (0, local_map(*g))),
            cw(0), cw(1), cw(2),
            pl.BlockSpec((None, 1, dv), lambda *g: (layer, 0, 0))]


def _delta_prompt(p_dn, gates, conv_w, norm_w, layer, *, nb, seq, nh, dk, dv, row0):
    tn = p_dn.shape[-1]
    hb = tn // dk
    r = min(ROW_TILE, seq)
    c = math.gcd(seq, CHUNK)
    npair = nh // hb
    nt = seq // r
    t0 = row0 // r
    kern = functools.partial(_delta_prompt_kernel, nh=nh, hb=hb, c=c, dk=dk)
    specs = _delta_common_specs(npair, lambda b, p, j: t0 + b * nt + j, lambda b, p, j: b * nt + j,
                                conv_w.shape[1], dv, layer, tn, r, 2 * nh)
    return pl.pallas_call(
        kern,
        grid=(nb, npair, nt),
        in_specs=specs,
        out_specs=[pl.BlockSpec((r, tn), lambda b, p, j: (b * nt + j, p)),
                   pl.BlockSpec((1, hb, dk, dv), lambda b, p, j: (b, p, 0, 0))],
        out_shape=[jax.ShapeDtypeStruct((nb * seq, nh * dv), BF16),
                   jax.ShapeDtypeStruct((nb, nh, dk, dv), F32)],
        scratch_shapes=[pltpu.VMEM((hb, dk, dv), F32),
                        pltpu.VMEM((SUBLANES, tn), F32), pltpu.VMEM((SUBLANES, tn), F32),
                        pltpu.VMEM((SUBLANES, tn), F32)],
        compiler_params=_cparams(("arbitrary", "arbitrary", "arbitrary")),
        name="delta_prompt",
    )(p_dn, p_dn, p_dn, p_dn, *gates, conv_w, conv_w, conv_w, norm_w)


def _delta_sample(p_dn, gates, conv_w, norm_w, bufx, state, layer, *, nb, seq, nh, dk, dv, row0, hist):
    tn = p_dn.shape[-1]
    hb = tn // dk
    c = seq
    r = min(ROW_TILE, nb * seq)
    npair = nh // hb
    nt = nb * seq // r
    spt = r // seq
    t0 = row0 // r
    kern = functools.partial(_delta_sample_kernel, nh=nh, hb=hb, c=c, dk=dk, hist=hist)
    specs = _delta_common_specs(npair, lambda i, p: t0 + i, lambda i, p: i,
                                conv_w.shape[1], dv, layer, tn, r, 2 * nh)

    def buf(n):
        return pl.BlockSpec((r, tn), lambda i, p: (i, n * npair + p))

    specs += [buf(0), buf(1), buf(2),
              pl.BlockSpec((None, spt, hb, dk, dv), lambda i, p: (layer, i, p, 0, 0))]
    return pl.pallas_call(
        kern,
        grid=(nt, npair),
        in_specs=specs,
        out_specs=[pl.BlockSpec((r, tn), lambda i, p: (i, p)),
                   pl.BlockSpec((spt, hb, dk, dv), lambda i, p: (i, p, 0, 0))],
        out_shape=[jax.ShapeDtypeStruct((nb * seq, nh * dv), BF16),
                   jax.ShapeDtypeStruct((nb, nh, dk, dv), F32)],
        scratch_shapes=_delta_scratch(r, dk, dv),
        compiler_params=_cparams(("arbitrary", "arbitrary")),
        name="delta_sample",
    )(p_dn, p_dn, p_dn, p_dn, *gates, conv_w, conv_w, conv_w, norm_w, bufx, bufx, bufx, state)


def _rotary(x, cs_ref, sn_ref):
    half = x.shape[1] // 2
    return x * cs_ref[...] + pltpu.roll(x, half, 1) * sn_ref[...]


def _ret_prompt_kernel(q_ref, k_ref, v_ref, g_ref, cs_ref, sn_ref, dec_ref, qd_ref, kd_ref, cd_ref,
                       o_ref, st_ref, s_sc, *, hb, c, dk):
    j = pl.program_id(2)

    @pl.when(j == 0)
    def _():
        s_sc[...] = jnp.zeros_like(s_sc)

    r = q_ref.shape[0]
    qs, kds, vs, inners = [], [], [], []
    for i in range(hb):
        cols = slice(i * dk, (i + 1) * dk)
        q = _rotary(q_ref[:, cols], cs_ref, sn_ref)
        k = _rotary(k_ref[:, cols], cs_ref, sn_ref) * (dk ** -0.5)
        v = v_ref[:, cols]
        inners.append(_bdot(_bdot_nt(q, k) * dec_ref[i], v))
        qs.append(q)
        kds.append(k * kd_ref[i])
        vs.append(v)
    ss = [s_sc[i] for i in range(hb)]
    qrs = [[] for _ in range(hb)]
    for n in range(r // c):
        rows = slice(n * c, (n + 1) * c)
        for i in range(hb):
            qrs[i].append(_bdot(qs[i][rows], ss[i]))
            ss[i] = cd_ref[i, 0:1, :] * ss[i] + _bdot_tn(kds[i][rows], vs[i][rows])
    for i in range(hb):
        cols = slice(i * dk, (i + 1) * dk)
        s_sc[i] = ss[i]
        st_ref[0, i] = ss[i]
        o = inners[i] + qd_ref[i] * jnp.concatenate(qrs[i], axis=0)
        o_ref[:, cols] = (_rms(o) * _silu(g_ref[:, cols])).astype(o_ref.dtype)


def _ret_sample_kernel(q_ref, k_ref, v_ref, g_ref, cs_ref, sn_ref, dec_ref, qd_ref, kd_ref, cd_ref, sin_ref,
                       o_ref, st_ref, q_s, kd_s, v_s, qr_s, *, hb, c, dk):
    r = q_ref.shape[0]
    for i in range(hb):
        cols = slice(i * dk, (i + 1) * dk)
        q = _rotary(q_ref[:, cols], cs_ref, sn_ref)
        k = _rotary(k_ref[:, cols], cs_ref, sn_ref) * (dk ** -0.5)
        v = v_ref[:, cols]
        inner = _bdot(_bdot_nt(q, k) * dec_ref[i], v)
        q_s[...] = q
        kd_s[...] = k * kd_ref[i]
        v_s[...] = v
        cd = cd_ref[i, 0:1, :]

        def group(gi, carry, i=i, cd=cd):
            seqs = [gi * SAMPLE_GROUP + j for j in range(SAMPLE_GROUP)]
            rows = [pl.ds(pl.multiple_of(n * c, c), c) for n in seqs]
            ss = [sin_ref[n, i] for n in seqs]
            for rw, s in zip(rows, ss):
                qr_s[rw, :] = _bdot(q_s[rw, :], s)
            for n, rw, s in zip(seqs, rows, ss):
                st_ref[n, i] = cd * s + _bdot_tn(kd_s[rw, :], v_s[rw, :])
            return carry

        lax.fori_loop(0, r // c // SAMPLE_GROUP, group, 0)
        o = inner + qd_ref[i] * qr_s[...]
        o_ref[:, cols] = (_rms(o) * _silu(g_ref[:, cols])).astype(o_ref.dtype)


def _ret_tables(nh, r, c, dk):
    log_gamma = jnp.log1p(-jnp.exp2(-5.0 - jnp.arange(nh, dtype=F32)))
    ti = jnp.arange(r)
    same = (ti[:, None] // c) == (ti[None, :] // c)
    rel = (ti[:, None] - ti[None, :]).astype(F32)
    dec = jnp.where(same & (rel >= 0), jnp.exp(jnp.maximum(rel, 0.0) * log_gamma[:, None, None]), 0.0)
    pos = (ti % c).astype(F32)
    qd = jnp.exp((pos + 1.0) * log_gamma[:, None])
    kd = jnp.exp((c - 1.0 - pos) * log_gamma[:, None])
    cd = jnp.exp(c * log_gamma)
    bc = lambda a: jnp.broadcast_to(a[..., None], a.shape + (dk,))
    return dec, bc(qd), bc(kd), jnp.broadcast_to(cd[:, None, None], (nh, SUBLANES, dk))


def _rope_tables(pos, dk):
    half = dk // 2
    inv = ROPE_BASE ** (-jnp.arange(half, dtype=F32) / half)
    ang = pos.astype(F32)[:, None] * inv[None, :]
    cos, sin = jnp.cos(ang), jnp.sin(ang)
    return jnp.concatenate([cos, cos], -1), jnp.concatenate([-sin, sin], -1)


def _ret_specs(npair, row_map, pos_map, tn, r, hb, dk):
    def seg(n):
        return pl.BlockSpec((None, r, tn), lambda *g: (n * npair + g[1], row_map(*g), 0))

    def tab(shape):
        return pl.BlockSpec((hb,) + shape, lambda *g: (g[1],) + (0,) * len(shape))

    return [seg(0), seg(1), seg(2), seg(3),
            pl.BlockSpec((r, dk), lambda *g: (pos_map(*g), 0)),
            pl.BlockSpec((r, dk), lambda *g: (pos_map(*g), 0)),
            tab((r, r)), tab((r, dk)), tab((r, dk)), tab((SUBLANES, dk))]


def _ret_scratch(r, dk, dv):
    return [pltpu.VMEM((r, dk), F32), pltpu.VMEM((r, dk), F32), pltpu.VMEM((r, dv), F32),
            pltpu.VMEM((r, dv), F32)]


def _ret_prompt(p_ret, cs, sn, *, nb, seq, nh, dk, dv, row0):
    tn = p_ret.shape[-1]
    hb = tn // dk
    r = min(ROW_TILE, seq)
    c = math.gcd(seq, CHUNK)
    npair = nh // hb
    nt = seq // r
    t0 = row0 // r
    dec, qd, kd, cd = _ret_tables(nh, r, c, dk)
    specs = _ret_specs(npair, lambda b, p, j: t0 + b * nt + j, lambda b, p, j: j, tn, r, hb, dk)
    return pl.pallas_call(
        functools.partial(_ret_prompt_kernel, hb=hb, c=c, dk=dk),
        grid=(nb, npair, nt),
        in_specs=specs,
        out_specs=[pl.BlockSpec((r, tn), lambda b, p, j: (b * nt + j, p)),
                   pl.BlockSpec((1, hb, dk, dv), lambda b, p, j: (b, p, 0, 0))],
        out_shape=[jax.ShapeDtypeStruct((nb * seq, nh * dv), BF16),
                   jax.ShapeDtypeStruct((nb, nh, dk, dv), F32)],
        scratch_shapes=[pltpu.VMEM((hb, dk, dv), F32)],
        compiler_params=_cparams(("arbitrary", "arbitrary", "arbitrary")),
        name="ret_prompt",
    )(p_ret, p_ret, p_ret, p_ret, cs, sn, dec, qd, kd, cd)


def _ret_sample(p_ret, cs, sn, state, layer, *, nb, seq, nh, dk, dv, row0):
    tn = p_ret.shape[-1]
    hb = tn // dk
    c = seq
    r = min(ROW_TILE, nb * seq)
    npair = nh // hb
    nt = nb * seq // r
    spt = r // seq
    t0 = row0 // r
    dec, qd, kd, cd = _ret_tables(nh, r, c, dk)
    specs = _ret_specs(npair, lambda i, p: t0 + i, lambda i, p: 0, tn, r, hb, dk)
    specs += [pl.BlockSpec((None, spt, hb, dk, dv), lambda i, p: (layer, i, p, 0, 0))]
    return pl.pallas_call(
        functools.partial(_ret_sample_kernel, hb=hb, c=c, dk=dk),
        grid=(nt, npair),
        in_specs=specs,
        out_specs=[pl.BlockSpec((r, tn), lambda i, p: (i, p)),
                   pl.BlockSpec((spt, hb, dk, dv), lambda i, p: (i, p, 0, 0))],
        out_shape=[jax.ShapeDtypeStruct((nb * seq, nh * dv), BF16),
                   jax.ShapeDtypeStruct((nb, nh, dk, dv), F32)],
        scratch_shapes=_ret_scratch(r, dk, dv),
        compiler_params=_cparams(("arbitrary", "arbitrary")),
        name="ret_sample",
    )(p_ret, p_ret, p_ret, p_ret, cs, sn, dec, qd, kd, cd, state)


S5_COL_CHUNK = 1024


def _s5_project_in(u_ref, bblk_ref, bur_s, bui_s):
    nblk = bblk_ref.shape[0]
    half = bblk_ref.shape[2] // 2
    for i in range(nblk):
        res = _bdot(u_ref[:, i * LANES:(i + 1) * LANES], bblk_ref[i])
        bur_s[:, i * half:(i + 1) * half] = res[:, :half]
        bui_s[:, i * half:(i + 1) * half] = res[:, half:]


def _s5_scan_rows(bur_s, bui_s, lam_ref, cols, row_start, nrows, hr, hi):
    lr = lam_ref[0:1, cols]
    li = lam_ref[1:2, cols]

    def step(t, carry):
        hr, hi = carry
        row = pl.ds(row_start + t, 1)
        nr = lr * hr - li * hi + bur_s[row, cols]
        ni = lr * hi + li * hr + bui_s[row, cols]
        bur_s[row, cols] = nr
        bui_s[row, cols] = ni
        return nr, ni

    return lax.fori_loop(0, nrows, step, (hr, hi))


def _s5_project_out(u_ref, bur_s, bui_s, cblk_ref, d_ref, wglu_ref, bglu_ref, o_ref):
    nblk = cblk_ref.shape[0]
    half = cblk_ref.shape[1] // 2
    ys = []
    for i in range(nblk):
        cb = cblk_ref[i]
        y = _bdot(bur_s[:, i * half:(i + 1) * half], cb[:half]) + _bdot(bui_s[:, i * half:(i + 1) * half], cb[half:])
        ys.append(y)
    y = jnp.concatenate(ys, axis=1) + d_ref[...] * u_ref[...]
    z = jax.nn.gelu(y)
    z = z * jax.nn.sigmoid(_bdot(z, wglu_ref[...]) + bglu_ref[...])
    o_ref[...] = z.astype(o_ref.dtype)


def _s5_prompt_kernel(u_ref, bblk_ref, cblk_ref, lam_ref, d_ref, wglu_ref, bglu_ref,
                      o_ref, hr_ref, hi_ref, bur_s, bui_s, hc_s):
    j = pl.program_id(1)
    r = u_ref.shape[0]
    ns = bur_s.shape[1]

    @pl.when(j == 0)
    def _():
        hc_s[...] = jnp.zeros_like(hc_s)

    _s5_project_in(u_ref, bblk_ref, bur_s, bui_s)
    cw = min(S5_COL_CHUNK, ns)
    for cc in range(ns // cw):
        cols = slice(cc * cw, (cc + 1) * cw)
        hr, hi = _s5_scan_rows(bur_s, bui_s, lam_ref, cols, 0, r, hc_s[0:1, cols], hc_s[1:2, cols])
        hc_s[0:1, cols] = hr
        hc_s[1:2, cols] = hi
    hr_ref[0] = hc_s[0:1, :]
    hi_ref[0] = hc_s[1:2, :]
    _s5_project_out(u_ref, bur_s, bui_s, cblk_ref, d_ref, wglu_ref, bglu_ref, o_ref)


def _s5_sample_kernel(u_ref, bblk_ref, cblk_ref, lam_ref, d_ref, wglu_ref, bglu_ref, h0r_ref, h0i_ref,
                      o_ref, hr_ref, hi_ref, bur_s, bui_s, *, seq):
    r = u_ref.shape[0]
    ns = bur_s.shape[1]
    _s5_project_in(u_ref, bblk_ref, bur_s, bui_s)
    cw = min(S5_COL_CHUNK, ns)
    for cc in range(ns // cw):
        cols = slice(cc * cw, (cc + 1) * cw)

        def per_seq(b, carry, cols=cols):
            hr, hi = _s5_scan_rows(bur_s, bui_s, lam_ref, cols, b * seq, seq,
                                   h0r_ref[pl.ds(b, 1), cols], h0i_ref[pl.ds(b, 1), cols])
            hr_ref[pl.ds(b, 1), cols] = hr
            hi_ref[pl.ds(b, 1), cols] = hi
            return carry

        lax.fori_loop(0, r // seq, per_seq, 0)
    _s5_project_out(u_ref, bur_s, bui_s, cblk_ref, d_ref, wglu_ref, bglu_ref, o_ref)


def _s5_params(lam_re, lam_im, log_step, b_re, b_im, c_re, c_im):
    g, p = lam_re.shape
    gs = b_re.shape[-1]
    dt = jnp.exp(log_step.astype(F32))[:, None]
    ar, ai = lam_re.astype(F32), lam_im.astype(F32)
    mag = jnp.exp(ar * dt)
    lbr, lbi = mag * jnp.cos(ai * dt), mag * jnp.sin(ai * dt)
    den = ar * ar + ai * ai
    nr, ni = lbr - 1.0, lbi
    cr = (nr * ar + ni * ai) / den
    ci = (ni * ar - nr * ai) / den
    bbr = cr[..., None] * b_re - ci[..., None] * b_im
    bbi = cr[..., None] * b_im + ci[..., None] * b_re
    gb = S5_GROUP_BLOCK
    nblk = g // gb
    eye = jnp.eye(gb, dtype=F32)
    bb = jnp.stack([bbr, bbi]).reshape(2, nblk, gb, p, gs)
    bblk = jnp.einsum('qigpc,gh->igcqhp', bb, eye).reshape(nblk, gb * gs, 2 * gb * p)
    cc = jnp.stack([c_re.astype(F32), -c_im.astype(F32)]).reshape(2, nblk, gb, gs, p)
    cblk = jnp.einsum('qigcp,gh->iqgphc', cc, eye).reshape(nblk, 2 * gb * p, gb * gs)
    lam = jnp.stack([lbr.reshape(-1), lbi.reshape(-1)])
    lam = jnp.concatenate([lam, jnp.zeros((SUBLANES - 2, g * p), F32)], axis=0)
    return bblk, cblk, lam


def _s5_const_specs(bblk, cblk, lam, w, layer, nmap):
    z = lambda n: (lambda *g: (0,) * n)
    return [pl.BlockSpec(bblk.shape, z(3)), pl.BlockSpec(cblk.shape, z(3)), pl.BlockSpec(lam.shape, z(2)),
            pl.BlockSpec((None, 1, w), lambda *g: (layer, 0, 0)),
            pl.BlockSpec((None, w, w), lambda *g: (layer, 0, 0)),
            pl.BlockSpec((None, 1, w), lambda *g: (layer, 0, 0))]


def _s5_prompt(u, bblk, cblk, lam, d, wglu, bglu, layer, *, nb, seq, row0):
    w = u.shape[1]
    ns = lam.shape[1]
    r = min(ROW_TILE, seq)
    nt = seq // r
    t0 = row0 // r
    return pl.pallas_call(
        _s5_prompt_kernel,
        grid=(nb, nt),
        in_specs=[pl.BlockSpec((r, w), lambda b, j: (t0 + b * nt + j, 0))]
        + _s5_const_specs(bblk, cblk, lam, w, layer, 2),
        out_specs=[pl.BlockSpec((r, w), lambda b, j: (b * nt + j, 0)),
                   pl.BlockSpec((1, 1, ns), lambda b, j: (b, 0, 0)),
                   pl.BlockSpec((1, 1, ns), lambda b, j: (b, 0, 0))],
        out_shape=[jax.ShapeDtypeStruct((nb * seq, w), BF16),
                   jax.ShapeDtypeStruct((nb, 1, ns), F32), jax.ShapeDtypeStruct((nb, 1, ns), F32)],
        scratch_shapes=[pltpu.VMEM((r, ns), F32), pltpu.VMEM((r, ns), F32), pltpu.VMEM((SUBLANES, ns), F32)],
        compiler_params=_cparams(("arbitrary", "arbitrary")),
        name="s5_prompt",
    )(u, bblk, cblk, lam, d, wglu, bglu)


def _s5_sample(u, bblk, cblk, lam, d, wglu, bglu, h0r, h0i, layer, *, nb, seq, row0):
    w = u.shape[1]
    ns = lam.shape[1]
    r = min(ROW_TILE, nb * seq)
    nt = nb * seq // r
    spt = r // seq
    t0 = row0 // r
    return pl.pallas_call(
        functools.partial(_s5_sample_kernel, seq=seq),
        grid=(nt,),
        in_specs=[pl.BlockSpec((r, w), lambda i: (t0 + i, 0))]
        + _s5_const_specs(bblk, cblk, lam, w, layer, 1)
        + [pl.BlockSpec((spt, ns), lambda i: (i, 0)), pl.BlockSpec((spt, ns), lambda i: (i, 0))],
        out_specs=[pl.BlockSpec((r, w), lambda i: (i, 0)),
                   pl.BlockSpec((spt, ns), lambda i: (i, 0)),
                   pl.BlockSpec((spt, ns), lambda i: (i, 0))],
        out_shape=[jax.ShapeDtypeStruct((nb * seq, w), BF16),
                   jax.ShapeDtypeStruct((nb, ns), F32), jax.ShapeDtypeStruct((nb, ns), F32)],
        scratch_shapes=[pltpu.VMEM((r, ns), F32), pltpu.VMEM((r, ns), F32)],
        compiler_params=_cparams(("arbitrary",)),
        name="s5_sample",
    )(u, bblk, cblk, lam, d, wglu, bglu, h0r, h0i)


def _top2(x):
    n = x.shape[-1]
    ids = jnp.arange(n, dtype=jnp.int32)
    i1 = jnp.argmax(x, axis=-1).astype(jnp.int32)
    m1 = jnp.max(x, axis=-1)
    rest = jnp.where(ids == i1[..., None], -jnp.inf, x)
    i2 = jnp.argmax(rest, axis=-1).astype(jnp.int32)
    m2 = jnp.max(rest, axis=-1)
    return jnp.stack([m1, m2], -1), jnp.stack([i1, i2], -1)


def _route(probs, n_experts):
    epg = n_experts // N_EXPERT_GROUPS
    grouped = probs.reshape(-1, N_EXPERT_GROUPS, epg)
    group_score = _top2(grouped)[0].sum(-1)
    g_sel = jnp.argmax(group_score, axis=-1).astype(jnp.int32)
    in_group = jnp.sum(jnp.where(jnp.arange(N_EXPERT_GROUPS)[None, :, None] == g_sel[:, None, None], grouped, 0.0),
                       axis=1)
    top_p, top_i = _top2(in_group)
    gate = top_p / jnp.sum(top_p, -1, keepdims=True)
    return g_sel[:, None] * epg + top_i.astype(jnp.int32), gate


def _dispatch(expert_idx, n_experts, bm):
    t = expert_idx.shape[0]
    n_pairs = t * TOP_K
    flat_e = expert_idx.reshape(-1)
    flat_tok = jnp.repeat(jnp.arange(t, dtype=jnp.int32), TOP_K)
    onehot = (flat_e[:, None] == jnp.arange(n_experts, dtype=jnp.int32)[None, :]).astype(jnp.int32)
    csum = jnp.cumsum(onehot, axis=0)
    counts = csum[-1]
    rank = jnp.sum((csum - onehot) * onehot, axis=1)
    padded = (counts + bm - 1) // bm * bm
    pstart = jnp.cumsum(padded) - padded
    pair_slot = jnp.sum(onehot * pstart[None, :], axis=1) + rank
    n_blocks = -(-(n_pairs + n_experts * (bm - 1)) // bm)
    n_slots = n_blocks * bm
    slot_tok = jnp.zeros((n_slots,), jnp.int32).at[pair_slot].set(flat_tok)
    block_exp = jnp.minimum(jnp.searchsorted(jnp.cumsum(padded), jnp.arange(n_blocks, dtype=jnp.int32) * bm,
                                             side='right'), n_experts - 1).astype(jnp.int32)
    n_used = (jnp.sum(padded) // bm).astype(jnp.int32).reshape(1)
    return slot_tok, pair_slot, block_exp, n_used, n_blocks


def _row_copy(src_hbm, dst, src_row, dst_row, sem):
    return pltpu.make_async_copy(src_hbm.at[pl.ds(src_row, 1)], dst.at[pl.ds(dst_row, 1)], sem)


def _gather_rows_kernel(tok_ref, nu_ref, x_hbm, o_ref, buf, sem):
    rows = o_ref.shape[0]
    nseg = x_hbm.shape[1]
    i = pl.program_id(0)
    n_live = nu_ref[0]

    def token_copy(tok, r, slot):
        return pltpu.make_async_copy(x_hbm.at[tok], buf.at[slot, pl.ds(r * nseg, nseg), :], sem.at[slot])

    def issue(step):
        slot = step % 2

        @pl.when(step < n_live)
        def _():
            def body(r, c):
                token_copy(tok_ref[step * rows + r], r, slot).start()
                return c

            lax.fori_loop(0, rows, body, 0)

    @pl.when(i == 0)
    def _():
        issue(i)

    issue(i + 1)

    @pl.when(i < n_live)
    def _():
        slot = i % 2

        def body(r, c):
            token_copy(0, r, slot).wait()
            return c

        lax.fori_loop(0, rows, body, 0)
        for k in range(nseg):
            o_ref[:, k * LANES:(k + 1) * LANES] = buf[slot, pl.ds(k, rows, stride=nseg), :].astype(o_ref.dtype)

    @pl.when(i >= n_live)
    def _():
        o_ref[...] = jnp.zeros_like(o_ref)


def _gather_rows(x, slot_tok, n_used, bm):
    n_slots = slot_tok.shape[0]
    t, d = x.shape
    nseg = d // LANES
    rows = min(bm, GATHER_ROWS)
    n_live = n_used * (bm // rows)
    return pl.pallas_call(
        _gather_rows_kernel,
        grid_spec=pltpu.PrefetchScalarGridSpec(
            num_scalar_prefetch=2,
            grid=(n_slots // rows,),
            in_specs=[pl.BlockSpec(memory_space=pl.ANY)],
            out_specs=pl.BlockSpec((rows, d), lambda i, tok, nu: (i, 0)),
            scratch_shapes=[pltpu.VMEM((2, rows * nseg, LANES), F32), pltpu.SemaphoreType.DMA((2,))]),
        out_shape=jax.ShapeDtypeStruct((n_slots, d), BF16),
        compiler_params=_cparams(("arbitrary",)),
        name="moe_gather",
    )(slot_tok, n_live, x.reshape(t, nseg, LANES))


def _expert_changed(be_ref, blk):
    return jnp.logical_or(blk == 0, be_ref[blk] != be_ref[jnp.maximum(blk - 1, 0)])


def _moe_gate_up_kernel(be_ref, nu_ref, x_ref, wg_ref, wu_ref, o_ref, wg_sc, wu_sc):
    blk = pl.program_id(1)

    @pl.when(_expert_changed(be_ref, blk))
    def _():
        wg_sc[...] = wg_ref[...].astype(BF16)
        wu_sc[...] = wu_ref[...].astype(BF16)

    @pl.when(blk < nu_ref[0])
    def _():
        x = x_ref[...]
        g = jnp.dot(x, wg_sc[...], preferred_element_type=F32)
        u = jnp.dot(x, wu_sc[...], preferred_element_type=F32)
        o_ref[...] = (_silu(g) * u).astype(o_ref.dtype)

    @pl.when(blk >= nu_ref[0])
    def _():
        o_ref[...] = jnp.zeros_like(o_ref)


def _moe_down_kernel(be_ref, nu_ref, h_ref, wd_ref, o_ref, wd_sc):
    blk = pl.program_id(1)

    @pl.when(_expert_changed(be_ref, blk))
    def _():
        wd_sc[...] = wd_ref[...].astype(BF16)

    @pl.when(blk < nu_ref[0])
    def _():
        o_ref[...] = jnp.dot(h_ref[...], wd_sc[...], preferred_element_type=F32)

    @pl.when(blk >= nu_ref[0])
    def _():
        o_ref[...] = jnp.zeros_like(o_ref)


def _moe_experts(xs, block_exp, n_used, w_gate, w_up, w_down, layer, bm):
    n_slots, d = xs.shape
    ff = w_gate.shape[-1]
    nblk = n_slots // bm
    tn = _pick(ff, MOE_TN_UP)
    w_in_spec = pl.BlockSpec((None, None, d, tn), lambda j, b, be, nu: (layer, be[b], 0, j))
    hm = pl.pallas_call(
        _moe_gate_up_kernel,
        grid_spec=pltpu.PrefetchScalarGridSpec(
            num_scalar_prefetch=2,
            grid=(ff // tn, nblk),
            in_specs=[pl.BlockSpec((bm, d), lambda j, b, be, nu: (b, 0)), w_in_spec, w_in_spec],
            out_specs=pl.BlockSpec((bm, tn), lambda j, b, be, nu: (b, j)),
            scratch_shapes=[pltpu.VMEM((d, tn), BF16), pltpu.VMEM((d, tn), BF16)]),
        out_shape=jax.ShapeDtypeStruct((n_slots, ff), BF16),
        compiler_params=_cparams(("arbitrary", "arbitrary")),
        name="moe_gate_up",
    )(block_exp, n_used, xs, w_gate, w_up)
    tn2 = _pick(d, MOE_TN_DOWN)
    return pl.pallas_call(
        _moe_down_kernel,
        grid_spec=pltpu.PrefetchScalarGridSpec(
            num_scalar_prefetch=2,
            grid=(d // tn2, nblk),
            in_specs=[pl.BlockSpec((bm, ff), lambda j, b, be, nu: (b, 0)),
                      pl.BlockSpec((None, None, ff, tn2), lambda j, b, be, nu: (layer, be[b], 0, j))],
            out_specs=pl.BlockSpec((bm, tn2), lambda j, b, be, nu: (b, j)),
            scratch_shapes=[pltpu.VMEM((ff, tn2), BF16)]),
        out_shape=jax.ShapeDtypeStruct((n_slots, d), F32),
        compiler_params=_cparams(("arbitrary", "arbitrary")),
        name="moe_down",
    )(block_exp, n_used, hm, w_down)


def _combine_ln_kernel(slot_ref, x_ref, w_ref, g_ref, b_ref, yb_hbm, of_ref, ob_ref, buf, sem, *, alpha):
    tm = x_ref.shape[0]
    i = pl.program_id(0)

    def issue(step):
        slot = step % 2

        def body(r, c):
            for k in range(TOP_K):
                _row_copy(yb_hbm, buf.at[slot, k], slot_ref[(step * tm + r) * TOP_K + k], r, sem.at[slot]).start()
            return c

        lax.fori_loop(0, tm, body, 0)

    @pl.when(i == 0)
    def _():
        issue(i)

    @pl.when(i + 1 < pl.num_programs(0))
    def _():
        issue(i + 1)

    slot = i % 2

    def wait(r, c):
        for k in range(TOP_K):
            _row_copy(yb_hbm, buf.at[slot, k], 0, r, sem.at[slot]).wait()
        return c

    lax.fori_loop(0, tm, wait, 0)
    w = w_ref[...]
    f = w[:, 0:1] * buf[slot, 0] + w[:, 1:2] * buf[slot, 1]
    z = _ln_math(alpha * x_ref[...] + f, g_ref[...], b_ref[...])
    of_ref[...] = z
    ob_ref[...] = z.astype(BF16)


def _combine_ln(x, yb, pair_slot, gate, g, b, alpha, layer):
    t, d = x.shape
    tm = min(128, t)
    return pl.pallas_call(
        functools.partial(_combine_ln_kernel, alpha=alpha),
        grid_spec=pltpu.PrefetchScalarGridSpec(
            num_scalar_prefetch=1,
            grid=(t // tm,),
            in_specs=[pl.BlockSpec((tm, d), lambda i, s: (i, 0)),
                      pl.BlockSpec((tm, TOP_K), lambda i, s: (i, 0)),
                      pl.BlockSpec((None, 1, d), lambda i, s: (layer, 0, 0)),
                      pl.BlockSpec((None, 1, d), lambda i, s: (layer, 0, 0)),
                      pl.BlockSpec(memory_space=pl.ANY)],
            out_specs=[pl.BlockSpec((tm, d), lambda i, s: (i, 0)),
                       pl.BlockSpec((tm, d), lambda i, s: (i, 0))],
            scratch_shapes=[pltpu.VMEM((2, TOP_K, tm, d), F32), pltpu.SemaphoreType.DMA((2,))]),
        out_shape=[jax.ShapeDtypeStruct((t, d), F32), jax.ShapeDtypeStruct((t, d), BF16)],
        compiler_params=_cparams(("arbitrary",)),
        name="moe_combine_ln2",
    )(pair_slot, x, gate, g, b, yb)


def kernel(x_prompt, x_sample, state_delta, state_conv, state_s5, state_ret, ln_in_g, ln_in_b, w_in, dn_conv_w, dn_a_log, dn_dt_bias, dn_norm_w, s5_lam_re, s5_lam_im, s5_log_step, s5_b_re, s5_b_im, s5_c_re, s5_c_im, s5_d, s5_w_glu, s5_b_glu, w_up_dn, w_up_s5, w_up_ret, w_o, ln1_g, ln1_b, router_w, router_b, w_gate_e, w_up_e, w_down_e, ln2_g, ln2_b):
    bp, lp, d = x_prompt.shape
    bs, ls, _ = x_sample.shape
    depth = w_in.shape[0]
    nh, dk, dv = state_delta.shape[2:]
    qkv = dn_conv_w.shape[2]
    hist = state_conv.shape[2]
    dnw = nh * dv
    s5w = s5_d.shape[1]
    g5, p5 = s5_lam_re.shape[1:]
    rh, rdk, rdv = state_ret.shape[2:]
    rw = rh * rdv
    n_exp = router_w.shape[1]
    tp, ts = bp * lp, bs * ls
    t = tp + ts
    alpha = (2 * depth) ** 0.25
    assert ls == SUBLANES and qkv == 3 * dnw and dk == dv == rdk == rdv == LANES

    sizes = (qkv, dnw, nh, nh, s5w, rh * rdk, rh * rdk, rw, rw, 3 * d)
    offs = [0]
    for s in sizes:
        offs.append(offs[-1] + s)
    o_dn, o_b, o_s5, o_ret, o_gates = offs[0], offs[2], offs[4], offs[5], offs[9]
    assert w_in.shape[2] == offs[-1] and 2 * nh < LANES
    w_in_t = jnp.swapaxes(w_in, 1, 2)

    xf, xb = _ln_in(x_prompt.reshape(tp, d), x_sample.reshape(ts, d), ln_in_g, ln_in_b)

    ep = -(-n_exp // LANES) * LANES
    rw_pad = jnp.pad(router_w.astype(F32), ((0, 0), (0, ep - n_exp)))
    rb_pad = jnp.pad(router_b.astype(F32).reshape(1, n_exp), ((0, 0), (0, ep - n_exp)), constant_values=-1e30)

    cs_p, sn_p = _rope_tables(jnp.arange(lp, dtype=jnp.int32), rdk)
    cs_s, sn_s = _rope_tables(PAST_LEN + jnp.arange(ls, dtype=jnp.int32), rdk)
    rs = min(ROW_TILE, ts)
    cs_s = jnp.tile(cs_s, (rs // ls, 1))
    sn_s = jnp.tile(sn_s, (rs // ls, 1))

    new_delta_p, new_conv_p, new_s5_p, new_ret_p = [], [], [], []
    new_delta_s, new_conv_s, new_s5_s, new_ret_s = [], [], [], []

    for l in range(depth):
        p_dn = _matmul_nt(xb, w_in_t, layer=l, row0=o_dn, n=qkv + dnw, tile_major=True,
                          tn_pref=min(MIXER_TN, dnw), name="proj_dn")
        ba = _matmul_nt(xb, w_in_t, layer=l, row0=o_b, n=LANES, name="proj_ba")
        u5 = _matmul_nt(xb, w_in_t, layer=l, row0=o_s5, n=s5w, name="proj_s5")
        p_ret = _matmul_nt(xb, w_in_t, layer=l, row0=o_ret, n=4 * rw, tile_major=True,
                           tn_pref=min(MIXER_TN, rw), name="proj_ret")
        gates = _matmul_nt(xb, w_in_t, layer=l, row0=o_gates, n=3 * d, tn_pref=512, name="proj_gates")

        tn = p_dn.shape[-1]
        bat = ba[:, :2 * nh].T
        zpad = jnp.zeros((nh,), F32)
        hpr = jnp.pad(jnp.stack([jnp.concatenate([zpad, dn_a_log[l].astype(F32)]),
                                 jnp.concatenate([zpad, dn_dt_bias[l].astype(F32)])]),
                      ((0, 0), (0, LANES - 2 * nh)))
        hpc = hpr[:, :2 * nh].T
        norm_w = dn_norm_w.reshape(depth, 1, dv)

        gates_p = _delta_gate(ba, bat, hpr, hpc, nh=nh, c=math.gcd(lp, CHUNK), row0=0, nrows=tp)
        gates_s = _delta_gate(ba, bat, hpr, hpc, nh=nh, c=ls, row0=tp, nrows=ts)
        odn_p, dlt_p = _delta_prompt(p_dn, gates_p, dn_conv_w, norm_w, l,
                                     nb=bp, seq=lp, nh=nh, dk=dk, dv=dv, row0=0)
        bufx = jnp.pad(state_conv[l], ((0, 0), (0, ls - hist), (0, 0))).reshape(ts, qkv)
        odn_s, dlt_s = _delta_sample(p_dn, gates_s, dn_conv_w, norm_w, bufx, state_delta, l,
                                     nb=bs, seq=ls, nh=nh, dk=dk, dv=dv, row0=tp, hist=hist)
        nq = qkv // tn
        cp = jnp.stack([lax.slice(p_dn, (0, b * lp + lp - hist, 0), (nq, (b + 1) * lp, tn))
                        for b in range(bp)], axis=1)
        cs_ = lax.slice(p_dn, (0, tp, 0), (nq, t, tn)).reshape(nq, bs, ls, tn)[:, :, ls - hist:]
        new_conv_p.append(jnp.moveaxis(cp, 0, 2).reshape(bp, hist, qkv))
        new_conv_s.append(jnp.moveaxis(cs_, 0, 2).reshape(bs, hist, qkv))
        new_delta_p.append(dlt_p)
        new_delta_s.append(dlt_s)

        bblk, cblk, lam = _s5_params(s5_lam_re[l], s5_lam_im[l], s5_log_step[l], s5_b_re[l], s5_b_im[l],
                                     s5_c_re[l], s5_c_im[l])
        d5 = s5_d.reshape(depth, 1, s5w)
        bg5 = s5_b_glu.reshape(depth, 1, s5w)
        os5_p, hr_p, hi_p = _s5_prompt(u5, bblk, cblk, lam, d5, s5_w_glu, bg5, l, nb=bp, seq=lp, row0=0)
        h0 = state_s5[l].astype(F32)
        os5_s, hr_s, hi_s = _s5_sample(u5, bblk, cblk, lam, d5, s5_w_glu, bg5,
                                       h0[..., 0].reshape(bs, g5 * p5), h0[..., 1].reshape(bs, g5 * p5), l,
                                       nb=bs, seq=ls, row0=tp)
        new_s5_p.append(jnp.stack([hr_p.reshape(bp, g5, p5), hi_p.reshape(bp, g5, p5)], -1))
        new_s5_s.append(jnp.stack([hr_s.reshape(bs, g5, p5), hi_s.reshape(bs, g5, p5)], -1))

        oret_p, rt_p = _ret_prompt(p_ret, cs_p, sn_p, nb=bp, seq=lp, nh=rh, dk=rdk, dv=rdv, row0=0)
        oret_s, rt_s = _ret_sample(p_ret, cs_s, sn_s, state_ret, l, nb=bs, seq=ls, nh=rh, dk=rdk, dv=rdv,
                                   row0=tp)
        new_ret_p.append(rt_p)
        new_ret_s.append(rt_s)

        merged = _upmerge((odn_p, os5_p, oret_p), (odn_s, os5_s, oret_s), gates, w_up_dn, w_up_s5, w_up_ret, l)
        mix = _matmul(merged, w_o, layer=l, tn_pref=512, name="w_o")
        xf, xb, probs = _ln1_router(xf, mix, ln1_g.reshape(depth, 1, d), ln1_b.reshape(depth, 1, d),
                                    rw_pad, rb_pad, alpha, l)

        expert_idx, gate = _route(probs[:, :n_exp], n_exp)
        slot_tok, pair_slot, block_exp, n_used, _ = _dispatch(expert_idx, n_exp, MOE_BM)
        xs = _gather_rows(xf, slot_tok, n_used, MOE_BM)
        yb = _moe_experts(xs, block_exp, n_used, w_gate_e, w_up_e, w_down_e, l, MOE_BM)
        xf, xb = _combine_ln(xf, yb, pair_slot, gate, ln2_g.reshape(depth, 1, d), ln2_b.reshape(depth, 1, d),
                             alpha, l)

    y_prompt = xf[:tp].reshape(bp, lp, d)
    y_sample = xf[tp:].reshape(bs, ls, d)
    st = lambda xs_, ref: jnp.stack(xs_).astype(ref.dtype)
    return (y_prompt, y_sample,
            st(new_delta_p, state_delta), st(new_conv_p, state_conv), st(new_s5_p, state_s5),
            st(new_ret_p, state_ret),
            st(new_delta_s, state_delta), st(new_conv_s, state_conv), st(new_s5_s, state_s5),
            st(new_ret_s, state_ret))
```

```python
import functools
import math

import jax
import jax.numpy as jnp
from jax import lax
from jax.experimental import pallas as pl
from jax.experimental.pallas import tpu as pltpu

F32 = jnp.float32
BF16 = jnp.bfloat16
HI = lax.Precision.HIGHEST

LANES = 128
SUBLANES = 8
VMEM_LIMIT = 60 * 1024 * 1024

LN_EPS = 1e-5
RMS_EPS = 1e-6
ROPE_BASE = 10000.0
PAST_LEN = 16384
CHUNK = 64
N_EXPERT_GROUPS = 4
TOP_K = 2
S5_GROUP_BLOCK = 8

ROW_TILE = 256
MM_TM = 1024
MM_TN = 256
MIXER_TN = 512
MOE_BM = 512
MOE_TN_UP = 512
MOE_TN_DOWN = 1024
GATHER_ROWS = 128
SAMPLE_GROUP = 8


def _cparams(sem):
    return pltpu.CompilerParams(dimension_semantics=sem, vmem_limit_bytes=VMEM_LIMIT)


def _bdot(a, b):
    return jnp.dot(a.astype(BF16), b.astype(BF16), preferred_element_type=F32)


def _bdot_nt(a, b):
    return lax.dot_general(a.astype(BF16), b.astype(BF16), (((1,), (1,)), ((), ())),
                           preferred_element_type=F32)


def _bdot_tn(a, b):
    return lax.dot_general(a.astype(BF16), b.astype(BF16), (((0,), (0,)), ((), ())),
                           preferred_element_type=F32)


def _hdot(a, b):
    return jnp.dot(a, b, precision=HI, preferred_element_type=F32)


def _ln_math(x, g, b):
    mu = jnp.mean(x, -1, keepdims=True)
    xc = x - mu
    var = jnp.mean(xc * xc, -1, keepdims=True)
    return xc * lax.rsqrt(var + LN_EPS) * g + b


def _ln_in_kernel(xp_ref, xs_ref, g_ref, b_ref, of_ref, ob_ref, *, n_prompt_tiles):
    i = pl.program_id(0)

    def run(src):
        y = _ln_math(src[...], g_ref[...], b_ref[...])
        of_ref[...] = y
        ob_ref[...] = y.astype(BF16)

    @pl.when(i < n_prompt_tiles)
    def _():
        run(xp_ref)

    @pl.when(i >= n_prompt_tiles)
    def _():
        run(xs_ref)


def _ln_in(xp, xs, g, b):
    tp, d = xp.shape
    ts = xs.shape[0]
    tm = ROW_TILE
    npt, nst = tp // tm, ts // tm
    t = tp + ts
    return pl.pallas_call(
        functools.partial(_ln_in_kernel, n_prompt_tiles=npt),
        grid=(npt + nst,),
        in_specs=[pl.BlockSpec((tm, d), lambda i: (jnp.minimum(i, npt - 1), 0)),
                  pl.BlockSpec((tm, d), lambda i: (jnp.maximum(i - npt, 0), 0)),
                  pl.BlockSpec((1, d), lambda i: (0, 0)),
                  pl.BlockSpec((1, d), lambda i: (0, 0))],
        out_specs=[pl.BlockSpec((tm, d), lambda i: (i, 0)),
                   pl.BlockSpec((tm, d), lambda i: (i, 0))],
        out_shape=[jax.ShapeDtypeStruct((t, d), F32), jax.ShapeDtypeStruct((t, d), BF16)],
        compiler_params=_cparams(("arbitrary",)),
        name="ln_in",
    )(xp, xs, g.reshape(1, d), b.reshape(1, d))


def _ln1_router_kernel(x_ref, y_ref, g_ref, b_ref, rw_ref, rb_ref, of_ref, ob_ref, pr_ref, *, alpha):
    z = _ln_math(alpha * x_ref[...] + y_ref[...], g_ref[...], b_ref[...])
    of_ref[...] = z
    ob_ref[...] = z.astype(BF16)
    logits = _hdot(z, rw_ref[...]) + rb_ref[...]
    m = jnp.max(logits, -1, keepdims=True)
    e = jnp.exp(logits - m)
    pr_ref[...] = e / jnp.sum(e, -1, keepdims=True)


def _ln1_router(x, y, g, b, rw, rb, alpha, layer):
    t, d = x.shape
    tm = ROW_TILE
    ep = rw.shape[1]
    return pl.pallas_call(
        functools.partial(_ln1_router_kernel, alpha=alpha),
        grid=(t // tm,),
        in_specs=[pl.BlockSpec((tm, d), lambda i: (i, 0)),
                  pl.BlockSpec((tm, d), lambda i: (i, 0)),
                  pl.BlockSpec((None, 1, d), lambda i: (layer, 0, 0)),
                  pl.BlockSpec((None, 1, d), lambda i: (layer, 0, 0)),
                  pl.BlockSpec((d, ep), lambda i: (0, 0)),
                  pl.BlockSpec((1, ep), lambda i: (0, 0))],
        out_specs=[pl.BlockSpec((tm, d), lambda i: (i, 0)),
                   pl.BlockSpec((tm, d), lambda i: (i, 0)),
                   pl.BlockSpec((tm, ep), lambda i: (i, 0))],
        out_shape=[jax.ShapeDtypeStruct((t, d), F32), jax.ShapeDtypeStruct((t, d), BF16),
                   jax.ShapeDtypeStruct((t, ep), F32)],
        compiler_params=_cparams(("arbitrary",)),
        name="ln1_router",
    )(x, y, g, b, rw, rb)


def _mm_kernel(a_ref, w_ref, o_ref):
    o_ref[...] = jnp.dot(a_ref[...], w_ref[...].astype(BF16),
                         preferred_element_type=F32).astype(o_ref.dtype)


def _pick(n, pref):
    for c in (pref, 512, 256, 128):
        if c <= pref and n % c == 0:
            return c
    return n


def _mm_nt_kernel(a_ref, wt_ref, o_ref):
    o_ref[...] = lax.dot_general(a_ref[...], wt_ref[0].astype(BF16), (((1,), (1,)), ((), ())),
                                 preferred_element_type=F32).astype(o_ref.dtype)


def _matmul_nt(a, wt, *, layer, row0, n, tile_major=False, tn_pref=MM_TN, name="mm_nt"):
    m, k = a.shape
    tm = _pick(m, MM_TM)
    tn = _pick(n, tn_pref)
    assert row0 % SUBLANES == 0
    w_spec = pl.BlockSpec((pl.Element(1), pl.Element(tn), pl.Element(k)),
                          lambda i, j: (layer, pl.multiple_of(row0 + j * tn, SUBLANES), 0))
    if tile_major:
        out_spec = pl.BlockSpec((None, tm, tn), lambda i, j: (j, i, 0))
        out_shape = jax.ShapeDtypeStruct((n // tn, m, tn), F32)
    else:
        out_spec = pl.BlockSpec((tm, tn), lambda i, j: (i, j))
        out_shape = jax.ShapeDtypeStruct((m, n), F32)
    return pl.pallas_call(
        _mm_nt_kernel,
        grid=(m // tm, n // tn),
        in_specs=[pl.BlockSpec((tm, k), lambda i, j: (i, 0)), w_spec],
        out_specs=out_spec,
        out_shape=out_shape,
        compiler_params=_cparams(("arbitrary", "arbitrary")),
        name=name,
    )(a, wt)


def _matmul(a, w, *, layer=None, col0=0, n=None, tile_major=False, out_dtype=F32, tn_pref=MM_TN, name="mm"):
    m, k = a.shape
    n = w.shape[-1] if n is None else n
    tm = _pick(m, MM_TM)
    tn = _pick(n, tn_pref)
    assert col0 % tn == 0
    j0 = col0 // tn
    if w.ndim == 3:
        w_spec = pl.BlockSpec((None, k, tn), lambda i, j: (layer, 0, j0 + j))
    else:
        w_spec = pl.BlockSpec((k, tn), lambda i, j: (0, j0 + j))
    if tile_major:
        out_spec = pl.BlockSpec((None, tm, tn), lambda i, j: (j, i, 0))
        out_shape = jax.ShapeDtypeStruct((n // tn, m, tn), out_dtype)
    else:
        out_spec = pl.BlockSpec((tm, tn), lambda i, j: (i, j))
        out_shape = jax.ShapeDtypeStruct((m, n), out_dtype)
    return pl.pallas_call(
        _mm_kernel,
        grid=(m // tm, n // tn),
        in_specs=[pl.BlockSpec((tm, k), lambda i, j: (i, 0)), w_spec],
        out_specs=out_spec,
        out_shape=out_shape,
        compiler_params=_cparams(("arbitrary", "arbitrary")),
        name=name,
    )(a, w)


def _upmerge_kernel(odn_p, os5_p, oret_p, odn_s, os5_s, oret_s, gdn, gs5, gret, wdn, ws5, wret, o_ref,
                    *, n_prompt_tiles):
    i = pl.program_id(0)

    def branch(o, w, g):
        return jax.nn.sigmoid(g[...]) * jnp.dot(o[...], w[...].astype(BF16), preferred_element_type=F32)

    def run(odn, os5, oret):
        o_ref[...] = (branch(odn, wdn, gdn) + branch(os5, ws5, gs5) + branch(oret, wret, gret)).astype(o_ref.dtype)

    @pl.when(i < n_prompt_tiles)
    def _():
        run(odn_p, os5_p, oret_p)

    @pl.when(i >= n_prompt_tiles)
    def _():
        run(odn_s, os5_s, oret_s)


def _upmerge(outs_p, outs_s, gates, w_dn, w_s5, w_ret, layer):
    tp, ts = outs_p[0].shape[0], outs_s[0].shape[0]
    t = tp + ts
    d = w_dn.shape[-1]
    tm = _pick(math.gcd(tp, ts), MM_TM)
    tn = _pick(d, MM_TN)
    nj = d // tn
    npt = tp // tm
    nst = ts // tm

    def p_spec(a):
        return pl.BlockSpec((tm, a.shape[1]), lambda i, j: (jnp.minimum(i, npt - 1), 0))

    def s_spec(a):
        return pl.BlockSpec((tm, a.shape[1]), lambda i, j: (jnp.maximum(i - npt, 0), 0))

    def g_spec(off):
        return pl.BlockSpec((tm, tn), lambda i, j: (i, off * nj + j))

    def w_spec(w):
        return pl.BlockSpec((None, w.shape[1], tn), lambda i, j: (layer, 0, j))

    return pl.pallas_call(
        functools.partial(_upmerge_kernel, n_prompt_tiles=npt),
        grid=(npt + nst, nj),
        in_specs=[p_spec(a) for a in outs_p] + [s_spec(a) for a in outs_s]
        + [g_spec(0), g_spec(1), g_spec(2), w_spec(w_dn), w_spec(w_s5), w_spec(w_ret)],
        out_specs=pl.BlockSpec((tm, tn), lambda i, j: (i, j)),
        out_shape=jax.ShapeDtypeStruct((t, d), BF16),
        compiler_params=_cparams(("arbitrary", "arbitrary")),
        name="upmerge",
    )(*outs_p, *outs_s, gates, gates, gates, w_dn, w_s5, w_ret)


def _chunk_masks(r, c):
    ti = lax.broadcasted_iota(jnp.int32, (r, r), 0)
    si = lax.broadcasted_iota(jnp.int32, (r, r), 1)
    same = (ti // c) == (si // c)
    return same, same & (si <= ti), same & (si < ti)


def _shift_rows_carry(x, prev8, s):
    rolled = pltpu.roll(x, s, 0)
    prev_rolled = pltpu.roll(prev8, s, 0)
    row8 = lax.broadcasted_iota(jnp.int32, prev8.shape, 0)
    first = jnp.where(row8 < s, prev_rolled, rolled[0:SUBLANES])
    return jnp.concatenate([first, rolled[SUBLANES:]], axis=0)


def _shift_rows_seq8(x, bufx, s, hist):
    r = x.shape[0]
    rolled = pltpu.roll(x, s, 0)
    brolled = pltpu.roll(bufx, (r - (hist - s)) % r, 0)
    t8 = lax.broadcasted_iota(jnp.int32, x.shape, 0) % SUBLANES
    return jnp.where(t8 < s, brolled, rolled)


def _causal_conv(x, w_ref, shift):
    n = w_ref.shape[0]
    y = w_ref[n - 1:n, :] * x
    for s in range(1, n):
        y = y + w_ref[n - 1 - s:n - s, :] * shift(s)
    return y


def _silu(x):
    return x * jax.nn.sigmoid(x)


def _l2norm(x):
    return x * lax.rsqrt(jnp.sum(x * x, -1, keepdims=True) + RMS_EPS)


def _rms(x):
    return x * lax.rsqrt(jnp.mean(x * x, -1, keepdims=True) + RMS_EPS)


def _split3(x):
    h = x.astype(BF16)
    r1 = x - h.astype(F32)
    m = r1.astype(BF16)
    l = (r1 - m.astype(F32)).astype(BF16)
    return h, m, l


def _delta_gate_kernel(ba_ref, bat_ref, hpr_ref, hpc_ref, beta_ref, gcum_ref, gtot_ref, rows_ref, *, nh, c):
    r = ba_ref.shape[0]
    same, incl, _ = _chunk_masks(r, c)
    ti = lax.broadcasted_iota(jnp.int32, (r, r), 0)
    si = lax.broadcasted_iota(jnp.int32, (r, r), 1)
    incl_t = (same & (ti <= si)).astype(BF16)
    inclb = incl.astype(BF16)
    sameb = same.astype(BF16)
    dot = functools.partial(jnp.dot, preferred_element_type=F32)
    ba = ba_ref[...]
    lane = lax.broadcasted_iota(jnp.int32, ba.shape, 1)
    beta_ref[...] = jax.nn.sigmoid(ba)
    g_cols = -jnp.exp(hpr_ref[0:1, :]) * jax.nn.softplus(ba + hpr_ref[1:2, :])
    g_cols = jnp.where((lane >= nh) & (lane < 2 * nh), g_cols, 0.0)
    h, m, l = _split3(g_cols)
    gcum_ref[...] = dot(inclb, h) + dot(inclb, m) + dot(inclb, l)
    gtot_ref[...] = dot(sameb, h) + dot(sameb, m) + dot(sameb, l)
    g_rows = -jnp.exp(hpc_ref[:, 0:1]) * jax.nn.softplus(bat_ref[...] + hpc_ref[:, 1:2])
    h, m, l = _split3(g_rows)
    rows_ref[...] = dot(h, incl_t) + dot(m, incl_t) + dot(l, incl_t)


def _delta_gate(ba, bat, hpr, hpc, *, nh, c, row0, nrows):
    r = min(ROW_TILE, nrows)
    t0 = row0 // r
    nh2 = 2 * nh
    col = pl.BlockSpec((r, LANES), lambda i: (i, 0))
    return pl.pallas_call(
        functools.partial(_delta_gate_kernel, nh=nh, c=c),
        grid=(nrows // r,),
        in_specs=[pl.BlockSpec((r, LANES), lambda i: (t0 + i, 0)),
                  pl.BlockSpec((nh2, r), lambda i: (0, t0 + i)),
                  pl.BlockSpec((2, LANES), lambda i: (0, 0)),
                  pl.BlockSpec((nh2, 2), lambda i: (0, 0))],
        out_specs=[col, col, col, pl.BlockSpec((nh2, r), lambda i: (0, i))],
        out_shape=[jax.ShapeDtypeStruct((nrows, LANES), F32)] * 3 + [jax.ShapeDtypeStruct((nh2, nrows), F32)],
        compiler_params=_cparams(("arbitrary",)),
        name="delta_gate",
    )(ba, bat, hpr, hpc)


def _pick_col(cols, idx):
    lane = lax.broadcasted_iota(jnp.int32, cols.shape, 1)
    return jnp.sum(jnp.where(lane == idx, cols, 0.0), -1, keepdims=True)


def _delta_prep(heads, c):
    r, dk = heads[0][1].shape
    _, incl, strict = _chunk_masks(r, c)
    ms, as_ = [], []
    for q, k, v, beta_c, gc_c, gt_c, gc_r in heads:
        dec = jnp.exp(jnp.minimum(gc_c - gc_r, 0.0))
        ms.append(jnp.where(strict, beta_c * dec * _bdot_nt(k, k), 0.0))
        as_.append(jnp.where(incl, dec * _bdot_nt(q, k), 0.0))
    eye = (lax.broadcasted_iota(jnp.int32, (r, r), 0) == lax.broadcasted_iota(jnp.int32, (r, r), 1)).astype(F32)
    xs = [eye - m for m in ms]
    pws = ms
    for _ in range(int(math.log2(c)) - 1):
        pws = [_bdot(pw, pw) for pw in pws]
        xs = [x + _bdot(x, pw) for x, pw in zip(xs, pws)]
    out = []
    for (q, k, v, beta_c, gc_c, gt_c, gc_r), a, x in zip(heads, as_, xs):
        gin = jnp.exp(gc_c)
        wu = _bdot(x, jnp.concatenate([beta_c * gin * k, beta_c * v], axis=1))
        out.append((a, wu[:, :dk], wu[:, dk:], k * jnp.exp(gt_c - gc_c), jnp.exp(gt_c), gin))
    return out


def _delta_head_inputs(qa, ka, va, beta_ref, gcum_ref, gtot_ref, rows_ref, i, h, nh, dk):
    cols = slice(i * dk, (i + 1) * dk)
    q = _l2norm(qa[:, cols]) * (dk ** -0.5)
    k = _l2norm(ka[:, cols])
    beta_c = _pick_col(beta_ref[...], h)
    gc_c = _pick_col(gcum_ref[...], nh + h)
    gt_c = _pick_col(gtot_ref[...], nh + h)
    gc_r = rows_ref[pl.ds(nh + h, 1), :]
    return q, k, va[:, cols], beta_c, gc_c, gt_c, gc_r


def _delta_finish(a, u, qs, z, nw_ref):
    o = qs + _bdot(a, u)
    return _rms(o) * nw_ref[...] * _silu(z)


def _delta_prompt_kernel(q_ref, k_ref, v_ref, z_ref, beta_ref, gcum_ref, gtot_ref, rows_ref,
                         cwq_ref, cwk_ref, cwv_ref, nw_ref,
                         o_ref, st_ref,
                         s_sc, pq_sc, pk_sc, pv_sc,
                         *, nh, hb, c, dk):
    p = pl.program_id(1)
    j = pl.program_id(2)
    r = q_ref.shape[0]

    @pl.when(j == 0)
    def _():
        s_sc[...] = jnp.zeros_like(s_sc)
        pq_sc[...] = jnp.zeros_like(pq_sc)
        pk_sc[...] = jnp.zeros_like(pk_sc)
        pv_sc[...] = jnp.zeros_like(pv_sc)

    def conv(x_ref, w_ref, prev_sc):
        x = x_ref[...]
        prev8 = prev_sc[...]
        y = _causal_conv(x, w_ref, lambda s: _shift_rows_carry(x, prev8, s))
        prev_sc[...] = x[r - SUBLANES:, :]
        return _silu(y)

    qa = conv(q_ref, cwq_ref, pq_sc)
    ka = conv(k_ref, cwk_ref, pk_sc)
    va = conv(v_ref, cwv_ref, pv_sc)
    z = z_ref[...]

    heads = [_delta_head_inputs(qa, ka, va, beta_ref, gcum_ref, gtot_ref, rows_ref, i, p * hb + i, nh, dk)
             for i in range(hb)]
    preps = _delta_prep(heads, c)
    nchunk = r // c
    chunk_rows = [slice(n * c, (n + 1) * c) for n in range(nchunk)]
    kws = [[_bdot_tn(pr[3][rows], pr[1][rows]) for rows in chunk_rows] for pr in preps]
    bcs = [[_bdot_tn(pr[3][rows], pr[2][rows]) for rows in chunk_rows] for pr in preps]
    ss = [s_sc[i] for i in range(hb)]
    us = [[] for _ in range(hb)]
    qss = [[] for _ in range(hb)]
    for n, rows in enumerate(chunk_rows):
        for i in range(hb):
            q = heads[i][0]
            _, w, u0, kd, eg, gin = preps[i]
            s = ss[i]
            us[i].append(u0[rows] - _bdot(w[rows], s))
            qss[i].append(gin[rows] * _bdot(q[rows], s))
            ss[i] = (eg[n * c:n * c + 1] * s + bcs[i][n]) - _bdot(kws[i][n], s)
    for i in range(hb):
        s_sc[i] = ss[i]
        st_ref[0, i] = ss[i]
        cols = slice(i * dk, (i + 1) * dk)
        o = _delta_finish(preps[i][0], jnp.concatenate(us[i], axis=0), jnp.concatenate(qss[i], axis=0),
                          z[:, cols], nw_ref)
        o_ref[:, cols] = o.astype(o_ref.dtype)


def _delta_sample_kernel(q_ref, k_ref, v_ref, z_ref, beta_ref, gcum_ref, gtot_ref, rows_ref,
                         cwq_ref, cwk_ref, cwv_ref, nw_ref, bq_ref, bk_ref, bv_ref, sin_ref,
                         o_ref, st_ref,
                         q_s, w_s, u0_s, kd_s, eg_s, gin_s, u_s, qs_s,
                         *, nh, hb, c, dk, hist):
    p = pl.program_id(1)
    r = q_ref.shape[0]

    def conv(x_ref, w_ref, b_ref):
        x = x_ref[...]
        bufx = b_ref[...]
        return _silu(_causal_conv(x, w_ref, lambda s: _shift_rows_seq8(x, bufx, s, hist)))

    qa = conv(q_ref, cwq_ref, bq_ref)
    ka = conv(k_ref, cwk_ref, bk_ref)
    va = conv(v_ref, cwv_ref, bv_ref)
    z = z_ref[...]

    heads = [_delta_head_inputs(qa, ka, va, beta_ref, gcum_ref, gtot_ref, rows_ref, i, p * hb + i, nh, dk)
             for i in range(hb)]
    for i in range(hb):
        q = heads[i][0]
        a, w, u0, kd, eg, gin = _delta_prep(heads[i:i + 1], c)[0]
        q_s[...] = q
        w_s[...] = w
        u0_s[...] = u0
        kd_s[...] = kd
        eg_s[...] = jnp.broadcast_to(eg, eg_s.shape)
        gin_s[...] = jnp.broadcast_to(gin, gin_s.shape)

        def group(gi, carry, i=i):
            seqs = [gi * SAMPLE_GROUP + j for j in range(SAMPLE_GROUP)]
            rows = [pl.ds(pl.multiple_of(n * c, c), c) for n in seqs]
            ss = [sin_ref[n, i] for n in seqs]
            wqs = [_bdot(jnp.concatenate([w_s[rw, :], q_s[rw, :]], axis=0), s) for rw, s in zip(rows, ss)]
            us = [u0_s[rw, :] - wq[:c] for rw, wq in zip(rows, wqs)]
            for rw, u, wq in zip(rows, us, wqs):
                u_s[rw, :] = u
                qs_s[rw, :] = gin_s[rw, :] * wq[c:]
            for n, rw, s, u in zip(seqs, rows, ss, us):
                st_ref[n, i] = eg_s[pl.ds(pl.multiple_of(n * c, c), 1), :] * s + _bdot_tn(kd_s[rw, :], u)
            return carry

        lax.fori_loop(0, r // c // SAMPLE_GROUP, group, 0)
        cols = slice(i * dk, (i + 1) * dk)
        o_ref[:, cols] = _delta_finish(a, u_s[...], qs_s[...], z[:, cols], nw_ref).astype(o_ref.dtype)


def _delta_scratch(r, dk, dv):
    return [pltpu.VMEM((r, dk), F32),
            pltpu.VMEM((r, dk), F32),
            pltpu.VMEM((r, dv), F32),
            pltpu.VMEM((r, dk), F32),
            pltpu.VMEM((r, dv), F32),
            pltpu.VMEM((r, dv), F32),
            pltpu.VMEM((r, dv), F32),
            pltpu.VMEM((r, dv), F32)]


def _delta_common_specs(tiles_per_seg, row_map, local_map, taps, dv, layer, tn, r, nh2):
    def seg(n):
        return pl.BlockSpec((None, r, tn), lambda *g: (n * tiles_per_seg + g[1], row_map(*g), 0))

    def cw(n):
        return pl.BlockSpec((None, taps, tn), lambda *g: (layer, 0, n * tiles_per_seg + g[1]))

    col = pl.BlockSpec((r, LANES), lambda *g: (local_map(*g), 0))
    return [seg(0), seg(1), seg(2), seg(3), col, col, col,
            pl.BlockSpec((nh2, r), lambda *g: (0, local_map(*g))),
            cw(0), cw(1), cw(2),
            pl.BlockSpec((None, 1, dv), lambda *g: (layer, 0, 0))]


def _delta_prompt(p_dn, gates, conv_w, norm_w, layer, *, nb, seq, nh, dk, dv, row0):
    tn = p_dn.shape[-1]
    hb = tn // dk
    r = min(ROW_TILE, seq)
    c = math.gcd(seq, CHUNK)
    npair = nh // hb
    nt = seq // r
    t0 = row0 // r
    kern = functools.partial(_delta_prompt_kernel, nh=nh, hb=hb, c=c, dk=dk)
    specs = _delta_common_specs(npair, lambda b, p, j: t0 + b * nt + j, lambda b, p, j: b * nt + j,
                                conv_w.shape[1], dv, layer, tn, r, 2 * nh)
    return pl.pallas_call(
        kern,
        grid=(nb, npair, nt),
        in_specs=specs,
        out_specs=[pl.BlockSpec((r, tn), lambda b, p, j: (b * nt + j, p)),
                   pl.BlockSpec((1, hb, dk, dv), lambda b, p, j: (b, p, 0, 0))],
        out_shape=[jax.ShapeDtypeStruct((nb * seq, nh * dv), BF16),
                   jax.ShapeDtypeStruct((nb, nh, dk, dv), F32)],
        scratch_shapes=[pltpu.VMEM((hb, dk, dv), F32),
                        pltpu.VMEM((SUBLANES, tn), F32), pltpu.VMEM((SUBLANES, tn), F32),
                        pltpu.VMEM((SUBLANES, tn), F32)],
        compiler_params=_cparams(("arbitrary", "arbitrary", "arbitrary")),
        name="delta_prompt",
    )(p_dn, p_dn, p_dn, p_dn, *gates, conv_w, conv_w, conv_w, norm_w)


def _delta_sample(p_dn, gates, conv_w, norm_w, bufx, state, layer, *, nb, seq, nh, dk, dv, row0, hist):
    tn = p_dn.shape[-1]
    hb = tn // dk
    c = seq
    r = min(ROW_TILE, nb * seq)
    npair = nh // hb
    nt = nb * seq // r
    spt = r // seq
    t0 = row0 // r
    kern = functools.partial(_delta_sample_kernel, nh=nh, hb=hb, c=c, dk=dk, hist=hist)
    specs = _delta_common_specs(npair, lambda i, p: t0 + i, lambda i, p: i,
                                conv_w.shape[1], dv, layer, tn, r, 2 * nh)

    def buf(n):
        return pl.BlockSpec((r, tn), lambda i, p: (i, n * npair + p))

    specs += [buf(0), buf(1), buf(2),
              pl.BlockSpec((None, spt, hb, dk, dv), lambda i, p: (layer, i, p, 0, 0))]
    return pl.pallas_call(
        kern,
        grid=(nt, npair),
        in_specs=specs,
        out_specs=[pl.BlockSpec((r, tn), lambda i, p: (i, p)),
                   pl.BlockSpec((spt, hb, dk, dv), lambda i, p: (i, p, 0, 0))],
        out_shape=[jax.ShapeDtypeStruct((nb * seq, nh * dv), BF16),
                   jax.ShapeDtypeStruct((nb, nh, dk, dv), F32)],
        scratch_shapes=_delta_scratch(r, dk, dv),
        compiler_params=_cparams(("arbitrary", "arbitrary")),
        name="delta_sample",
    )(p_dn, p_dn, p_dn, p_dn, *gates, conv_w, conv_w, conv_w, norm_w, bufx, bufx, bufx, state)


def _rotary(x, cs_ref, sn_ref):
    half = x.shape[1] // 2
    return x * cs_ref[...] + pltpu.roll(x, half, 1) * sn_ref[...]


def _ret_prompt_kernel(q_ref, k_ref, v_ref, g_ref, cs_ref, sn_ref, dec_ref, qd_ref, kd_ref, cd_ref,
                       o_ref, st_ref, s_sc, *, hb, c, dk):
    j = pl.program_id(2)

    @pl.when(j == 0)
    def _():
        s_sc[...] = jnp.zeros_like(s_sc)

    r = q_ref.shape[0]
    qs, kds, vs, inners = [], [], [], []
    for i in range(hb):
        cols = slice(i * dk, (i + 1) * dk)
        q = _rotary(q_ref[:, cols], cs_ref, sn_ref)
        k = _rotary(k_ref[:, cols], cs_ref, sn_ref) * (dk ** -0.5)
        v = v_ref[:, cols]
        inners.append(_bdot(_bdot_nt(q, k) * dec_ref[i], v))
        qs.append(q)
        kds.append(k * kd_ref[i])
        vs.append(v)
    ss = [s_sc[i] for i in range(hb)]
    qrs = [[] for _ in range(hb)]
    for n in range(r // c):
        rows = slice(n * c, (n + 1) * c)
        for i in range(hb):
            qrs[i].append(_bdot(qs[i][rows], ss[i]))
            ss[i] = cd_ref[i, 0:1, :] * ss[i] + _bdot_tn(kds[i][rows], vs[i][rows])
    for i in range(hb):
        cols = slice(i * dk, (i + 1) * dk)
        s_sc[i] = ss[i]
        st_ref[0, i] = ss[i]
        o = inners[i] + qd_ref[i] * jnp.concatenate(qrs[i], axis=0)
        o_ref[:, cols] = (_rms(o) * _silu(g_ref[:, cols])).astype(o_ref.dtype)


def _ret_sample_kernel(q_ref, k_ref, v_ref, g_ref, cs_ref, sn_ref, dec_ref, qd_ref, kd_ref, cd_ref, sin_ref,
                       o_ref, st_ref, q_s, kd_s, v_s, qr_s, *, hb, c, dk):
    r = q_ref.shape[0]
    for i in range(hb):
        cols = slice(i * dk, (i + 1) * dk)
        q = _rotary(q_ref[:, cols], cs_ref, sn_ref)
        k = _rotary(k_ref[:, cols], cs_ref, sn_ref) * (dk ** -0.5)
        v = v_ref[:, cols]
        inner = _bdot(_bdot_nt(q, k) * dec_ref[i], v)
        q_s[...] = q
        kd_s[...] = k * kd_ref[i]
        v_s[...] = v
        cd = cd_ref[i, 0:1, :]

        def group(gi, carry, i=i, cd=cd):
            seqs = [gi * SAMPLE_GROUP + j for j in range(SAMPLE_GROUP)]
            rows = [pl.ds(pl.multiple_of(n * c, c), c) for n in seqs]
            ss = [sin_ref[n, i] for n in seqs]
            for rw, s in zip(rows, ss):
                qr_s[rw, :] = _bdot(q_s[rw, :], s)
            for n, rw, s in zip(seqs, rows, ss):
                st_ref[n, i] = cd * s + _bdot_tn(kd_s[rw, :], v_s[rw, :])
            return carry

        lax.fori_loop(0, r // c // SAMPLE_GROUP, group, 0)
        o = inner + qd_ref[i] * qr_s[...]
        o_ref[:, cols] = (_rms(o) * _silu(g_ref[:, cols])).astype(o_ref.dtype)


def _ret_tables(nh, r, c, dk):
    log_gamma = jnp.log1p(-jnp.exp2(-5.0 - jnp.arange(nh, dtype=F32)))
    ti = jnp.arange(r)
    same = (ti[:, None] // c) == (ti[None, :] // c)
    rel = (ti[:, None] - ti[None, :]).astype(F32)
    dec = jnp.where(same & (rel >= 0), jnp.exp(jnp.maximum(rel, 0.0) * log_gamma[:, None, None]), 0.0)
    pos = (ti % c).astype(F32)
    qd = jnp.exp((pos + 1.0) * log_gamma[:, None])
    kd = jnp.exp((c - 1.0 - pos) * log_gamma[:, None])
    cd = jnp.exp(c * log_gamma)
    bc = lambda a: jnp.broadcast_to(a[..., None], a.shape + (dk,))
    return dec, bc(qd), bc(kd), jnp.broadcast_to(cd[:, None, None], (nh, SUBLANES, dk))


def _rope_tables(pos, dk):
    half = dk // 2
    inv = ROPE_BASE ** (-jnp.arange(half, dtype=F32) / half)
    ang = pos.astype(F32)[:, None] * inv[None, :]
    cos, sin = jnp.cos(ang), jnp.sin(ang)
    return jnp.concatenate([cos, cos], -1), jnp.concatenate([-sin, sin], -1)


def _ret_specs(npair, row_map, pos_map, tn, r, hb, dk):
    def seg(n):
        return pl.BlockSpec((None, r, tn), lambda *g: (n * npair + g[1], row_map(*g), 0))

    def tab(shape):
        return pl.BlockSpec((hb,) + shape, lambda *g: (g[1],) + (0,) * len(shape))

    return [seg(0), seg(1), seg(2), seg(3),
            pl.BlockSpec((r, dk), lambda *g: (pos_map(*g), 0)),
            pl.BlockSpec((r, dk), lambda *g: (pos_map(*g), 0)),
            tab((r, r)), tab((r, dk)), tab((r, dk)), tab((SUBLANES, dk))]


def _ret_scratch(r, dk, dv):
    return [pltpu.VMEM((r, dk), F32), pltpu.VMEM((r, dk), F32), pltpu.VMEM((r, dv), F32),
            pltpu.VMEM((r, dv), F32)]


def _ret_prompt(p_ret, cs, sn, *, nb, seq, nh, dk, dv, row0):
    tn = p_ret.shape[-1]
    hb = tn // dk
    r = min(ROW_TILE, seq)
    c = math.gcd(seq, CHUNK)
    npair = nh // hb
    nt = seq // r
    t0 = row0 // r
    dec, qd, kd, cd = _ret_tables(nh, r, c, dk)
    specs = _ret_specs(npair, lambda b, p, j: t0 + b * nt + j, lambda b, p, j: j, tn, r, hb, dk)
    return pl.pallas_call(
        functools.partial(_ret_prompt_kernel, hb=hb, c=c, dk=dk),
        grid=(nb, npair, nt),
        in_specs=specs,
        out_specs=[pl.BlockSpec((r, tn), lambda b, p, j: (b * nt + j, p)),
                   pl.BlockSpec((1, hb, dk, dv), lambda b, p, j: (b, p, 0, 0))],
        out_shape=[jax.ShapeDtypeStruct((nb * seq, nh * dv), BF16),
                   jax.ShapeDtypeStruct((nb, nh, dk, dv), F32)],
        scratch_shapes=[pltpu.VMEM((hb, dk, dv), F32)],
        compiler_params=_cparams(("arbitrary", "arbitrary", "arbitrary")),
        name="ret_prompt",
    )(p_ret, p_ret, p_ret, p_ret, cs, sn, dec, qd, kd, cd)


def _ret_sample(p_ret, cs, sn, state, layer, *, nb, seq, nh, dk, dv, row0):
    tn = p_ret.shape[-1]
    hb = tn // dk
    c = seq
    r = min(ROW_TILE, nb * seq)
    npair = nh // hb
    nt = nb * seq // r
    spt = r // seq
    t0 = row0 // r
    dec, qd, kd, cd = _ret_tables(nh, r, c, dk)
    specs = _ret_specs(npair, lambda i, p: t0 + i, lambda i, p: 0, tn, r, hb, dk)
    specs += [pl.BlockSpec((None, spt, hb, dk, dv), lambda i, p: (layer, i, p, 0, 0))]
    return pl.pallas_call(
        functools.partial(_ret_sample_kernel, hb=hb, c=c, dk=dk),
        grid=(nt, npair),
        in_specs=specs,
        out_specs=[pl.BlockSpec((r, tn), lambda i, p: (i, p)),
                   pl.BlockSpec((spt, hb, dk, dv), lambda i, p: (i, p, 0, 0))],
        out_shape=[jax.ShapeDtypeStruct((nb * seq, nh * dv), BF16),
                   jax.ShapeDtypeStruct((nb, nh, dk, dv), F32)],
        scratch_shapes=_ret_scratch(r, dk, dv),
        compiler_params=_cparams(("arbitrary", "arbitrary")),
        name="ret_sample",
    )(p_ret, p_ret, p_ret, p_ret, cs, sn, dec, qd, kd, cd, state)


S5_COL_CHUNK = 1024


def _s5_project_in(u_ref, bblk_ref, bur_s, bui_s):
    nblk = bblk_ref.shape[0]
    half = bblk_ref.shape[2] // 2
    for i in range(nblk):
        res = _bdot(u_ref[:, i * LANES:(i + 1) * LANES], bblk_ref[i])
        bur_s[:, i * half:(i + 1) * half] = res[:, :half]
        bui_s[:, i * half:(i + 1) * half] = res[:, half:]


def _s5_scan_rows(bur_s, bui_s, lam_ref, cols, row_start, nrows, hr, hi):
    lr = lam_ref[0:1, cols]
    li = lam_ref[1:2, cols]

    def step(t, carry):
        hr, hi = carry
        row = pl.ds(row_start + t, 1)
        nr = lr * hr - li * hi + bur_s[row, cols]
        ni = lr * hi + li * hr + bui_s[row, cols]
        bur_s[row, cols] = nr
        bui_s[row, cols] = ni
        return nr, ni

    return lax.fori_loop(0, nrows, step, (hr, hi))


def _s5_project_out(u_ref, bur_s, bui_s, cblk_ref, d_ref, wglu_ref, bglu_ref, o_ref):
    nblk = cblk_ref.shape[0]
    half = cblk_ref.shape[1] // 2
    ys = []
    for i in range(nblk):
        cb = cblk_ref[i]
        y = _bdot(bur_s[:, i * half:(i + 1) * half], cb[:half]) + _bdot(bui_s[:, i * half:(i + 1) * half], cb[half:])
        ys.append(y)
    y = jnp.concatenate(ys, axis=1) + d_ref[...] * u_ref[...]
    z = jax.nn.gelu(y)
    z = z * jax.nn.sigmoid(_bdot(z, wglu_ref[...]) + bglu_ref[...])
    o_ref[...] = z.astype(o_ref.dtype)


def _s5_prompt_kernel(u_ref, bblk_ref, cblk_ref, lam_ref, d_ref, wglu_ref, bglu_ref,
                      o_ref, hr_ref, hi_ref, bur_s, bui_s, hc_s):
    j = pl.program_id(1)
    r = u_ref.shape[0]
    ns = bur_s.shape[1]

    @pl.when(j == 0)
    def _():
        hc_s[...] = jnp.zeros_like(hc_s)

    _s5_project_in(u_ref, bblk_ref, bur_s, bui_s)
    cw = min(S5_COL_CHUNK, ns)
    for cc in range(ns // cw):
        cols = slice(cc * cw, (cc + 1) * cw)
        hr, hi = _s5_scan_rows(bur_s, bui_s, lam_ref, cols, 0, r, hc_s[0:1, cols], hc_s[1:2, cols])
        hc_s[0:1, cols] = hr
        hc_s[1:2, cols] = hi
    hr_ref[0] = hc_s[0:1, :]
    hi_ref[0] = hc_s[1:2, :]
    _s5_project_out(u_ref, bur_s, bui_s, cblk_ref, d_ref, wglu_ref, bglu_ref, o_ref)


def _s5_sample_kernel(u_ref, bblk_ref, cblk_ref, lam_ref, d_ref, wglu_ref, bglu_ref, h0r_ref, h0i_ref,
                      o_ref, hr_ref, hi_ref, bur_s, bui_s, *, seq):
    r = u_ref.shape[0]
    ns = bur_s.shape[1]
    _s5_project_in(u_ref, bblk_ref, bur_s, bui_s)
    cw = min(S5_COL_CHUNK, ns)
    for cc in range(ns // cw):
        cols = slice(cc * cw, (cc + 1) * cw)

        def per_seq(b, carry, cols=cols):
            hr, hi = _s5_scan_rows(bur_s, bui_s, lam_ref, cols, b * seq, seq,
                                   h0r_ref[pl.ds(b, 1), cols], h0i_ref[pl.ds(b, 1), cols])
            hr_ref[pl.ds(b, 1), cols] = hr
            hi_ref[pl.ds(b, 1), cols] = hi
            return carry

        lax.fori_loop(0, r // seq, per_seq, 0)
    _s5_project_out(u_ref, bur_s, bui_s, cblk_ref, d_ref, wglu_ref, bglu_ref, o_ref)


def _s5_params(lam_re, lam_im, log_step, b_re, b_im, c_re, c_im):
    g, p = lam_re.shape
    gs = b_re.shape[-1]
    dt = jnp.exp(log_step.astype(F32))[:, None]
    ar, ai = lam_re.astype(F32), lam_im.astype(F32)
    mag = jnp.exp(ar * dt)
    lbr, lbi = mag * jnp.cos(ai * dt), mag * jnp.sin(ai * dt)
    den = ar * ar + ai * ai
    nr, ni = lbr - 1.0, lbi
    cr = (nr * ar + ni * ai) / den
    ci = (ni * ar - nr * ai) / den
    bbr = cr[..., None] * b_re - ci[..., None] * b_im
    bbi = cr[..., None] * b_im + ci[..., None] * b_re
    gb = S5_GROUP_BLOCK
    nblk = g // gb
    eye = jnp.eye(gb, dtype=F32)
    bb = jnp.stack([bbr, bbi]).reshape(2, nblk, gb, p, gs)
    bblk = jnp.einsum('qigpc,gh->igcqhp', bb, eye).reshape(nblk, gb * gs, 2 * gb * p)
    cc = jnp.stack([c_re.astype(F32), -c_im.astype(F32)]).reshape(2, nblk, gb, gs, p)
    cblk = jnp.einsum('qigcp,gh->iqgphc', cc, eye).reshape(nblk, 2 * gb * p, gb * gs)
    lam = jnp.stack([lbr.reshape(-1), lbi.reshape(-1)])
    lam = jnp.concatenate([lam, jnp.zeros((SUBLANES - 2, g * p), F32)], axis=0)
    return bblk, cblk, lam


def _s5_const_specs(bblk, cblk, lam, w, layer, nmap):
    z = lambda n: (lambda *g: (0,) * n)
    return [pl.BlockSpec(bblk.shape, z(3)), pl.BlockSpec(cblk.shape, z(3)), pl.BlockSpec(lam.shape, z(2)),
            pl.BlockSpec((None, 1, w), lambda *g: (layer, 0, 0)),
            pl.BlockSpec((None, w, w), lambda *g: (layer, 0, 0)),
            pl.BlockSpec((None, 1, w), lambda *g: (layer, 0, 0))]


def _s5_prompt(u, bblk, cblk, lam, d, wglu, bglu, layer, *, nb, seq, row0):
    w = u.shape[1]
    ns = lam.shape[1]
    r = min(ROW_TILE, seq)
    nt = seq // r
    t0 = row0 // r
    return pl.pallas_call(
        _s5_prompt_kernel,
        grid=(nb, nt),
        in_specs=[pl.BlockSpec((r, w), lambda b, j: (t0 + b * nt + j, 0))]
        + _s5_const_specs(bblk, cblk, lam, w, layer, 2),
        out_specs=[pl.BlockSpec((r, w), lambda b, j: (b * nt + j, 0)),
                   pl.BlockSpec((1, 1, ns), lambda b, j: (b, 0, 0)),
                   pl.BlockSpec((1, 1, ns), lambda b, j: (b, 0, 0))],
        out_shape=[jax.ShapeDtypeStruct((nb * seq, w), BF16),
                   jax.ShapeDtypeStruct((nb, 1, ns), F32), jax.ShapeDtypeStruct((nb, 1, ns), F32)],
        scratch_shapes=[pltpu.VMEM((r, ns), F32), pltpu.VMEM((r, ns), F32), pltpu.VMEM((SUBLANES, ns), F32)],
        compiler_params=_cparams(("arbitrary", "arbitrary")),
        name="s5_prompt",
    )(u, bblk, cblk, lam, d, wglu, bglu)


def _s5_sample(u, bblk, cblk, lam, d, wglu, bglu, h0r, h0i, layer, *, nb, seq, row0):
    w = u.shape[1]
    ns = lam.shape[1]
    r = min(ROW_TILE, nb * seq)
    nt = nb * seq // r
    spt = r // seq
    t0 = row0 // r
    return pl.pallas_call(
        functools.partial(_s5_sample_kernel, seq=seq),
        grid=(nt,),
        in_specs=[pl.BlockSpec((r, w), lambda i: (t0 + i, 0))]
        + _s5_const_specs(bblk, cblk, lam, w, layer, 1)
        + [pl.BlockSpec((spt, ns), lambda i: (i, 0)), pl.BlockSpec((spt, ns), lambda i: (i, 0))],
        out_specs=[pl.BlockSpec((r, w), lambda i: (i, 0)),
                   pl.BlockSpec((spt, ns), lambda i: (i, 0)),
                   pl.BlockSpec((spt, ns), lambda i: (i, 0))],
        out_shape=[jax.ShapeDtypeStruct((nb * seq, w), BF16),
                   jax.ShapeDtypeStruct((nb, ns), F32), jax.ShapeDtypeStruct((nb, ns), F32)],
        scratch_shapes=[pltpu.VMEM((r, ns), F32), pltpu.VMEM((r, ns), F32)],
        compiler_params=_cparams(("arbitrary",)),
        name="s5_sample",
    )(u, bblk, cblk, lam, d, wglu, bglu, h0r, h0i)


def _top2(x):
    n = x.shape[-1]
    ids = jnp.arange(n, dtype=jnp.int32)
    i1 = jnp.argmax(x, axis=-1).astype(jnp.int32)
    m1 = jnp.max(x, axis=-1)
    rest = jnp.where(ids == i1[..., None], -jnp.inf, x)
    i2 = jnp.argmax(rest, axis=-1).astype(jnp.int32)
    m2 = jnp.max(rest, axis=-1)
    return jnp.stack([m1, m2], -1), jnp.stack([i1, i2], -1)


def _route(probs, n_experts):
    epg = n_experts // N_EXPERT_GROUPS
    grouped = probs.reshape(-1, N_EXPERT_GROUPS, epg)
    group_score = _top2(grouped)[0].sum(-1)
    g_sel = jnp.argmax(group_score, axis=-1).astype(jnp.int32)
    in_group = jnp.sum(jnp.where(jnp.arange(N_EXPERT_GROUPS)[None, :, None] == g_sel[:, None, None], grouped, 0.0),
                       axis=1)
    top_p, top_i = _top2(in_group)
    gate = top_p / jnp.sum(top_p, -1, keepdims=True)
    return g_sel[:, None] * epg + top_i.astype(jnp.int32), gate


def _dispatch(expert_idx, n_experts, bm):
    t = expert_idx.shape[0]
    n_pairs = t * TOP_K
    flat_e = expert_idx.reshape(-1)
    flat_tok = jnp.repeat(jnp.arange(t, dtype=jnp.int32), TOP_K)
    onehot = (flat_e[:, None] == jnp.arange(n_experts, dtype=jnp.int32)[None, :]).astype(jnp.int32)
    csum = jnp.cumsum(onehot, axis=0)
    counts = csum[-1]
    rank = jnp.sum((csum - onehot) * onehot, axis=1)
    padded = (counts + bm - 1) // bm * bm
    pstart = jnp.cumsum(padded) - padded
    pair_slot = jnp.sum(onehot * pstart[None, :], axis=1) + rank
    n_blocks = -(-(n_pairs + n_experts * (bm - 1)) // bm)
    n_slots = n_blocks * bm
    slot_tok = jnp.zeros((n_slots,), jnp.int32).at[pair_slot].set(flat_tok)
    block_exp = jnp.minimum(jnp.searchsorted(jnp.cumsum(padded), jnp.arange(n_blocks, dtype=jnp.int32) * bm,
                                             side='right'), n_experts - 1).astype(jnp.int32)
    n_used = (jnp.sum(padded) // bm).astype(jnp.int32).reshape(1)
    return slot_tok, pair_slot, block_exp, n_used, n_blocks


def _row_copy(src_hbm, dst, src_row, dst_row, sem):
    return pltpu.make_async_copy(src_hbm.at[pl.ds(src_row, 1)], dst.at[pl.ds(dst_row, 1)], sem)


def _gather_rows_kernel(tok_ref, nu_ref, x_hbm, o_ref, buf, sem):
    rows = buf.shape[1]
    i = pl.program_id(0)
    n_live = nu_ref[0]

    def issue(step):
        slot = step % 2

        @pl.when(step < n_live)
        def _():
            def body(h, c):
                for p in range(2):
                    r = 2 * h + p
                    _row_copy(x_hbm, buf.at[slot], tok_ref[step * rows + r], r, sem.at[slot]).start(priority=p)
                return c

            lax.fori_loop(0, rows // 2, body, 0)

    @pl.when(i == 0)
    def _():
        issue(i)

    issue(i + 1)

    @pl.when(i < n_live)
    def _():
        slot = i % 2

        def body(r, c):
            _row_copy(x_hbm, buf.at[slot], 0, r, sem.at[slot]).wait()
            return c

        lax.fori_loop(0, rows, body, 0)
        o_ref[...] = buf[slot].astype(o_ref.dtype)

    @pl.when(i >= n_live)
    def _():
        o_ref[...] = jnp.zeros_like(o_ref)


def _gather_rows(x, slot_tok, n_used, bm):
    n_slots = slot_tok.shape[0]
    d = x.shape[1]
    rows = min(bm, GATHER_ROWS)
    n_live = n_used * (bm // rows)
    return pl.pallas_call(
        _gather_rows_kernel,
        grid_spec=pltpu.PrefetchScalarGridSpec(
            num_scalar_prefetch=2,
            grid=(n_slots // rows,),
            in_specs=[pl.BlockSpec(memory_space=pl.ANY)],
            out_specs=pl.BlockSpec((rows, d), lambda i, tok, nu: (i, 0)),
            scratch_shapes=[pltpu.VMEM((2, rows, d), F32), pltpu.SemaphoreType.DMA((2,))]),
        out_shape=jax.ShapeDtypeStruct((n_slots, d), BF16),
        compiler_params=_cparams(("arbitrary",)),
        name="moe_gather",
    )(slot_tok, n_live, x)


def _expert_changed(be_ref, blk):
    return jnp.logical_or(blk == 0, be_ref[blk] != be_ref[jnp.maximum(blk - 1, 0)])


def _moe_gate_up_kernel(be_ref, nu_ref, x_ref, wg_ref, wu_ref, o_ref, wg_sc, wu_sc):
    blk = pl.program_id(1)

    @pl.when(_expert_changed(be_ref, blk))
    def _():
        wg_sc[...] = wg_ref[...].astype(BF16)
        wu_sc[...] = wu_ref[...].astype(BF16)

    @pl.when(blk < nu_ref[0])
    def _():
        x = x_ref[...]
        g = jnp.dot(x, wg_sc[...], preferred_element_type=F32)
        u = jnp.dot(x, wu_sc[...], preferred_element_type=F32)
        o_ref[...] = (_silu(g) * u).astype(o_ref.dtype)

    @pl.when(blk >= nu_ref[0])
    def _():
        o_ref[...] = jnp.zeros_like(o_ref)


def _moe_down_kernel(be_ref, nu_ref, h_ref, wd_ref, o_ref, wd_sc):
    blk = pl.program_id(1)

    @pl.when(_expert_changed(be_ref, blk))
    def _():
        wd_sc[...] = wd_ref[...].astype(BF16)

    @pl.when(blk < nu_ref[0])
    def _():
        o_ref[...] = jnp.dot(h_ref[...], wd_sc[...], preferred_element_type=F32)

    @pl.when(blk >= nu_ref[0])
    def _():
        o_ref[...] = jnp.zeros_like(o_ref)


def _moe_experts(xs, block_exp, n_used, w_gate, w_up, w_down, layer, bm):
    n_slots, d = xs.shape
    ff = w_gate.shape[-1]
    nblk = n_slots // bm
    tn = _pick(ff, MOE_TN_UP)
    w_in_spec = pl.BlockSpec((None, None, d, tn), lambda j, b, be, nu: (layer, be[b], 0, j))
    hm = pl.pallas_call(
        _moe_gate_up_kernel,
        grid_spec=pltpu.PrefetchScalarGridSpec(
            num_scalar_prefetch=2,
            grid=(ff // tn, nblk),
            in_specs=[pl.BlockSpec((bm, d), lambda j, b, be, nu: (b, 0)), w_in_spec, w_in_spec],
            out_specs=pl.BlockSpec((bm, tn), lambda j, b, be, nu: (b, j)),
            scratch_shapes=[pltpu.VMEM((d, tn), BF16), pltpu.VMEM((d, tn), BF16)]),
        out_shape=jax.ShapeDtypeStruct((n_slots, ff), BF16),
        compiler_params=_cparams(("arbitrary", "arbitrary")),
        name="moe_gate_up",
    )(block_exp, n_used, xs, w_gate, w_up)
    tn2 = _pick(d, MOE_TN_DOWN)
    return pl.pallas_call(
        _moe_down_kernel,
        grid_spec=pltpu.PrefetchScalarGridSpec(
            num_scalar_prefetch=2,
            grid=(d // tn2, nblk),
            in_specs=[pl.BlockSpec((bm, ff), lambda j, b, be, nu: (b, 0)),
                      pl.BlockSpec((None, None, ff, tn2), lambda j, b, be, nu: (layer, be[b], 0, j))],
            out_specs=pl.BlockSpec((bm, tn2), lambda j, b, be, nu: (b, j)),
            scratch_shapes=[pltpu.VMEM((ff, tn2), BF16)]),
        out_shape=jax.ShapeDtypeStruct((n_slots, d), F32),
        compiler_params=_cparams(("arbitrary", "arbitrary")),
        name="moe_down",
    )(block_exp, n_used, hm, w_down)


def _combine_ln_kernel(slot_ref, x_ref, w_ref, g_ref, b_ref, yb_hbm, of_ref, ob_ref, buf, sem, *, alpha):
    tm = x_ref.shape[0]
    i = pl.program_id(0)

    def issue(step):
        slot = step % 2

        def body(r, c):
            for k in range(TOP_K):
                _row_copy(yb_hbm, buf.at[slot, k], slot_ref[(step * tm + r) * TOP_K + k], r,
                          sem.at[slot]).start(priority=k % 2)
            return c

        lax.fori_loop(0, tm, body, 0)

    @pl.when(i == 0)
    def _():
        issue(i)

    @pl.when(i + 1 < pl.num_programs(0))
    def _():
        issue(i + 1)

    slot = i % 2

    def wait(r, c):
        for k in range(TOP_K):
            _row_copy(yb_hbm, buf.at[slot, k], 0, r, sem.at[slot]).wait()
        return c

    lax.fori_loop(0, tm, wait, 0)
    w = w_ref[...]
    f = w[:, 0:1] * buf[slot, 0] + w[:, 1:2] * buf[slot, 1]
    z = _ln_math(alpha * x_ref[...] + f, g_ref[...], b_ref[...])
    of_ref[...] = z
    ob_ref[...] = z.astype(BF16)


def _combine_ln(x, yb, pair_slot, gate, g, b, alpha, layer):
    t, d = x.shape
    tm = min(128, t)
    return pl.pallas_call(
        functools.partial(_combine_ln_kernel, alpha=alpha),
        grid_spec=pltpu.PrefetchScalarGridSpec(
            num_scalar_prefetch=1,
            grid=(t // tm,),
            in_specs=[pl.BlockSpec((tm, d), lambda i, s: (i, 0)),
                      pl.BlockSpec((tm, TOP_K), lambda i, s: (i, 0)),
                      pl.BlockSpec((None, 1, d), lambda i, s: (layer, 0, 0)),
                      pl.BlockSpec((None, 1, d), lambda i, s: (layer, 0, 0)),
                      pl.BlockSpec(memory_space=pl.ANY)],
            out_specs=[pl.BlockSpec((tm, d), lambda i, s: (i, 0)),
                       pl.BlockSpec((tm, d), lambda i, s: (i, 0))],
            scratch_shapes=[pltpu.VMEM((2, TOP_K, tm, d), F32), pltpu.SemaphoreType.DMA((2,))]),
        out_shape=[jax.ShapeDtypeStruct((t, d), F32), jax.ShapeDtypeStruct((t, d), BF16)],
        compiler_params=_cparams(("arbitrary",)),
        name="moe_combine_ln2",
    )(pair_slot, x, gate, g, b, yb)


def kernel(x_prompt, x_sample, state_delta, state_conv, state_s5, state_ret, ln_in_g, ln_in_b, w_in, dn_conv_w, dn_a_log, dn_dt_bias, dn_norm_w, s5_lam_re, s5_lam_im, s5_log_step, s5_b_re, s5_b_im, s5_c_re, s5_c_im, s5_d, s5_w_glu, s5_b_glu, w_up_dn, w_up_s5, w_up_ret, w_o, ln1_g, ln1_b, router_w, router_b, w_gate_e, w_up_e, w_down_e, ln2_g, ln2_b):
    bp, lp, d = x_prompt.shape
    bs, ls, _ = x_sample.shape
    depth = w_in.shape[0]
    nh, dk, dv = state_delta.shape[2:]
    qkv = dn_conv_w.shape[2]
    hist = state_conv.shape[2]
    dnw = nh * dv
    s5w = s5_d.shape[1]
    g5, p5 = s5_lam_re.shape[1:]
    rh, rdk, rdv = state_ret.shape[2:]
    rw = rh * rdv
    n_exp = router_w.shape[1]
    tp, ts = bp * lp, bs * ls
    t = tp + ts
    alpha = (2 * depth) ** 0.25
    assert ls == SUBLANES and qkv == 3 * dnw and dk == dv == rdk == rdv == LANES

    sizes = (qkv, dnw, nh, nh, s5w, rh * rdk, rh * rdk, rw, rw, 3 * d)
    offs = [0]
    for s in sizes:
        offs.append(offs[-1] + s)
    o_dn, o_b, o_s5, o_ret, o_gates = offs[0], offs[2], offs[4], offs[5], offs[9]
    assert w_in.shape[2] == offs[-1] and 2 * nh < LANES
    w_in_t = jnp.swapaxes(w_in, 1, 2)

    xf, xb = _ln_in(x_prompt.reshape(tp, d), x_sample.reshape(ts, d), ln_in_g, ln_in_b)

    ep = -(-n_exp // LANES) * LANES
    rw_pad = jnp.pad(router_w.astype(F32), ((0, 0), (0, ep - n_exp)))
    rb_pad = jnp.pad(router_b.astype(F32).reshape(1, n_exp), ((0, 0), (0, ep - n_exp)), constant_values=-1e30)

    cs_p, sn_p = _rope_tables(jnp.arange(lp, dtype=jnp.int32), rdk)
    cs_s, sn_s = _rope_tables(PAST_LEN + jnp.arange(ls, dtype=jnp.int32), rdk)
    rs = min(ROW_TILE, ts)
    cs_s = jnp.tile(cs_s, (rs // ls, 1))
    sn_s = jnp.tile(sn_s, (rs // ls, 1))

    new_delta_p, new_conv_p, new_s5_p, new_ret_p = [], [], [], []
    new_delta_s, new_conv_s, new_s5_s, new_ret_s = [], [], [], []

    for l in range(depth):
        p_dn = _matmul_nt(xb, w_in_t, layer=l, row0=o_dn, n=qkv + dnw, tile_major=True,
                          tn_pref=min(MIXER_TN, dnw), name="proj_dn")
        ba = _matmul_nt(xb, w_in_t, layer=l, row0=o_b, n=LANES, name="proj_ba")
        u5 = _matmul_nt(xb, w_in_t, layer=l, row0=o_s5, n=s5w, name="proj_s5")
        p_ret = _matmul_nt(xb, w_in_t, layer=l, row0=o_ret, n=4 * rw, tile_major=True,
                           tn_pref=min(MIXER_TN, rw), name="proj_ret")
        gates = _matmul_nt(xb, w_in_t, layer=l, row0=o_gates, n=3 * d, tn_pref=512, name="proj_gates")

        tn = p_dn.shape[-1]
        bat = ba[:, :2 * nh].T
        zpad = jnp.zeros((nh,), F32)
        hpr = jnp.pad(jnp.stack([jnp.concatenate([zpad, dn_a_log[l].astype(F32)]),
                                 jnp.concatenate([zpad, dn_dt_bias[l].astype(F32)])]),
                      ((0, 0), (0, LANES - 2 * nh)))
        hpc = hpr[:, :2 * nh].T
        norm_w = dn_norm_w.reshape(depth, 1, dv)

        gates_p = _delta_gate(ba, bat, hpr, hpc, nh=nh, c=math.gcd(lp, CHUNK), row0=0, nrows=tp)
        gates_s = _delta_gate(ba, bat, hpr, hpc, nh=nh, c=ls, row0=tp, nrows=ts)
        odn_p, dlt_p = _delta_prompt(p_dn, gates_p, dn_conv_w, norm_w, l,
                                     nb=bp, seq=lp, nh=nh, dk=dk, dv=dv, row0=0)
        bufx = jnp.pad(state_conv[l], ((0, 0), (0, ls - hist), (0, 0))).reshape(ts, qkv)
        odn_s, dlt_s = _delta_sample(p_dn, gates_s, dn_conv_w, norm_w, bufx, state_delta, l,
                                     nb=bs, seq=ls, nh=nh, dk=dk, dv=dv, row0=tp, hist=hist)
        nq = qkv // tn
        cp = jnp.stack([lax.slice(p_dn, (0, b * lp + lp - hist, 0), (nq, (b + 1) * lp, tn))
                        for b in range(bp)], axis=1)
        cs_ = lax.slice(p_dn, (0, tp, 0), (nq, t, tn)).reshape(nq, bs, ls, tn)[:, :, ls - hist:]
        new_conv_p.append(jnp.moveaxis(cp, 0, 2).reshape(bp, hist, qkv))
        new_conv_s.append(jnp.moveaxis(cs_, 0, 2).reshape(bs, hist, qkv))
        new_delta_p.append(dlt_p)
        new_delta_s.append(dlt_s)

        bblk, cblk, lam = _s5_params(s5_lam_re[l], s5_lam_im[l], s5_log_step[l], s5_b_re[l], s5_b_im[l],
                                     s5_c_re[l], s5_c_im[l])
        d5 = s5_d.reshape(depth, 1, s5w)
        bg5 = s5_b_glu.reshape(depth, 1, s5w)
        os5_p, hr_p, hi_p = _s5_prompt(u5, bblk, cblk, lam, d5, s5_w_glu, bg5, l, nb=bp, seq=lp, row0=0)
        h0 = state_s5[l].astype(F32)
        os5_s, hr_s, hi_s = _s5_sample(u5, bblk, cblk, lam, d5, s5_w_glu, bg5,
                                       h0[..., 0].reshape(bs, g5 * p5), h0[..., 1].reshape(bs, g5 * p5), l,
                                       nb=bs, seq=ls, row0=tp)
        new_s5_p.append(jnp.stack([hr_p.reshape(bp, g5, p5), hi_p.reshape(bp, g5, p5)], -1))
        new_s5_s.append(jnp.stack([hr_s.reshape(bs, g5, p5), hi_s.reshape(bs, g5, p5)], -1))

        oret_p, rt_p = _ret_prompt(p_ret, cs_p, sn_p, nb=bp, seq=lp, nh=rh, dk=rdk, dv=rdv, row0=0)
        oret_s, rt_s = _ret_sample(p_ret, cs_s, sn_s, state_ret, l, nb=bs, seq=ls, nh=rh, dk=rdk, dv=rdv,
                                   row0=tp)
        new_ret_p.append(rt_p)
        new_ret_s.append(rt_s)

        merged = _upmerge((odn_p, os5_p, oret_p), (odn_s, os5_s, oret_s), gates, w_up_dn, w_up_s5, w_up_ret, l)
        mix = _matmul(merged, w_o, layer=l, tn_pref=512, name="w_o")
        xf, xb, probs = _ln1_router(xf, mix, ln1_g.reshape(depth, 1, d), ln1_b.reshape(depth, 1, d),
                                    rw_pad, rb_pad, alpha, l)

        expert_idx, gate = _route(probs[:, :n_exp], n_exp)
        slot_tok, pair_slot, block_exp, n_used, _ = _dispatch(expert_idx, n_exp, MOE_BM)
        xs = _gather_rows(xf, slot_tok, n_used, MOE_BM)
        yb = _moe_experts(xs, block_exp, n_used, w_gate_e, w_up_e, w_down_e, l, MOE_BM)
        xf, xb = _combine_ln(xf, yb, pair_slot, gate, ln2_g.reshape(depth, 1, d), ln2_b.reshape(depth, 1, d),
                             alpha, l)

    y_prompt = xf[:tp].reshape(bp, lp, d)
    y_sample = xf[tp:].reshape(bs, ls, d)
    st = lambda xs_, ref: jnp.stack(xs_).astype(ref.dtype)
    return (y_prompt, y_sample,
            st(new_delta_p, state_delta), st(new_conv_p, state_conv), st(new_s5_p, state_s5),
            st(new_ret_p, state_ret),
            st(new_delta_s, state_delta), st(new_conv_s, state_conv), st(new_s5_s, state_s5),
            st(new_ret_s, state_ret))
```

```python
import functools
import math

import jax
import jax.numpy as jnp
from jax import lax
from jax.experimental import pallas as pl
from jax.experimental.pallas import tpu as pltpu

F32 = jnp.float32
BF16 = jnp.bfloat16
HI = lax.Precision.HIGHEST

LANES = 128
SUBLANES = 8
VMEM_LIMIT = 60 * 1024 * 1024

LN_EPS = 1e-5
RMS_EPS = 1e-6
ROPE_BASE = 10000.0
PAST_LEN = 16384
CHUNK = 64
N_EXPERT_GROUPS = 4
TOP_K = 2
S5_GROUP_BLOCK = 8

ROW_TILE = 256
MM_TM = 1024
MM_TN = 256
MIXER_TN = 512
MOE_BM = 512
MOE_TN_UP = 512
MOE_TN_DOWN = 1024
GATHER_ROWS = 128
SAMPLE_GROUP = 8


def _cparams(sem):
    return pltpu.CompilerParams(dimension_semantics=sem, vmem_limit_bytes=VMEM_LIMIT)


def _bdot(a, b):
    return jnp.dot(a.astype(BF16), b.astype(BF16), preferred_element_type=F32)


def _bdot_nt(a, b):
    return lax.dot_general(a.astype(BF16), b.astype(BF16), (((1,), (1,)), ((), ())),
                           preferred_element_type=F32)


def _bdot_tn(a, b):
    return lax.dot_general(a.astype(BF16), b.astype(BF16), (((0,), (0,)), ((), ())),
                           preferred_element_type=F32)


def _hdot(a, b):
    return jnp.dot(a, b, precision=HI, preferred_element_type=F32)


def _ln_math(x, g, b):
    mu = jnp.mean(x, -1, keepdims=True)
    xc = x - mu
    var = jnp.mean(xc * xc, -1, keepdims=True)
    return xc * lax.rsqrt(var + LN_EPS) * g + b


def _ln_in_kernel(xp_ref, xs_ref, g_ref, b_ref, of_ref, ob_ref, *, n_prompt_tiles):
    i = pl.program_id(0)

    def run(src):
        y = _ln_math(src[...], g_ref[...], b_ref[...])
        of_ref[...] = y
        ob_ref[...] = y.astype(BF16)

    @pl.when(i < n_prompt_tiles)
    def _():
        run(xp_ref)

    @pl.when(i >= n_prompt_tiles)
    def _():
        run(xs_ref)


def _ln_in(xp, xs, g, b):
    tp, d = xp.shape
    ts = xs.shape[0]
    tm = ROW_TILE
    npt, nst = tp // tm, ts // tm
    t = tp + ts
    return pl.pallas_call(
        functools.partial(_ln_in_kernel, n_prompt_tiles=npt),
        grid=(npt + nst,),
        in_specs=[pl.BlockSpec((tm, d), lambda i: (jnp.minimum(i, npt - 1), 0)),
                  pl.BlockSpec((tm, d), lambda i: (jnp.maximum(i - npt, 0), 0)),
                  pl.BlockSpec((1, d), lambda i: (0, 0)),
                  pl.BlockSpec((1, d), lambda i: (0, 0))],
        out_specs=[pl.BlockSpec((tm, d), lambda i: (i, 0)),
                   pl.BlockSpec((tm, d), lambda i: (i, 0))],
        out_shape=[jax.ShapeDtypeStruct((t, d), F32), jax.ShapeDtypeStruct((t, d), BF16)],
        compiler_params=_cparams(("arbitrary",)),
        name="ln_in",
    )(xp, xs, g.reshape(1, d), b.reshape(1, d))


HI16 = 0xFFFF0000


def _pack_bf16_pairs(z):
    half = z.shape[1] // 2
    bits = lambda v: lax.bitcast_convert_type(v.astype(BF16).astype(F32), jnp.uint32)
    return (bits(z[:, :half]) >> 16) | (bits(z[:, half:]) & jnp.uint32(HI16))


def _unpack_bf16_pairs(p):
    lo = lax.bitcast_convert_type(p << 16, F32).astype(BF16)
    hi = lax.bitcast_convert_type(p & jnp.uint32(HI16), F32).astype(BF16)
    return lo, hi


def _ln1_router_kernel(x_ref, y_ref, g_ref, b_ref, rw_ref, rb_ref, of_ref, pk_ref, pr_ref, *, alpha):
    z = _ln_math(alpha * x_ref[...] + y_ref[...], g_ref[...], b_ref[...])
    of_ref[...] = z
    pk_ref[...] = _pack_bf16_pairs(z)
    logits = _hdot(z, rw_ref[...]) + rb_ref[...]
    m = jnp.max(logits, -1, keepdims=True)
    e = jnp.exp(logits - m)
    pr_ref[...] = e / jnp.sum(e, -1, keepdims=True)


def _ln1_router(x, y, g, b, rw, rb, alpha, layer):
    t, d = x.shape
    tm = ROW_TILE
    ep = rw.shape[1]
    return pl.pallas_call(
        functools.partial(_ln1_router_kernel, alpha=alpha),
        grid=(t // tm,),
        in_specs=[pl.BlockSpec((tm, d), lambda i: (i, 0)),
                  pl.BlockSpec((tm, d), lambda i: (i, 0)),
                  pl.BlockSpec((None, 1, d), lambda i: (layer, 0, 0)),
                  pl.BlockSpec((None, 1, d), lambda i: (layer, 0, 0)),
                  pl.BlockSpec((d, ep), lambda i: (0, 0)),
                  pl.BlockSpec((1, ep), lambda i: (0, 0))],
        out_specs=[pl.BlockSpec((tm, d), lambda i: (i, 0)),
                   pl.BlockSpec((tm, d // 2), lambda i: (i, 0)),
                   pl.BlockSpec((tm, ep), lambda i: (i, 0))],
        out_shape=[jax.ShapeDtypeStruct((t, d), F32), jax.ShapeDtypeStruct((t, d // 2), jnp.uint32),
                   jax.ShapeDtypeStruct((t, ep), F32)],
        compiler_params=_cparams(("arbitrary",)),
        name="ln1_router",
    )(x, y, g, b, rw, rb)


def _mm_kernel(a_ref, w_ref, o_ref):
    o_ref[...] = jnp.dot(a_ref[...], w_ref[...].astype(BF16),
                         preferred_element_type=F32).astype(o_ref.dtype)


def _pick(n, pref):
    for c in (pref, 512, 256, 128):
        if c <= pref and n % c == 0:
            return c
    return n


def _mm_nt_kernel(a_ref, wt_ref, o_ref):
    o_ref[...] = lax.dot_general(a_ref[...], wt_ref[0].astype(BF16), (((1,), (1,)), ((), ())),
                                 preferred_element_type=F32).astype(o_ref.dtype)


def _matmul_nt(a, wt, *, layer, row0, n, tile_major=False, tn_pref=MM_TN, name="mm_nt"):
    m, k = a.shape
    tm = _pick(m, MM_TM)
    tn = _pick(n, tn_pref)
    assert row0 % SUBLANES == 0
    w_spec = pl.BlockSpec((pl.Element(1), pl.Element(tn), pl.Element(k)),
                          lambda i, j: (layer, pl.multiple_of(row0 + j * tn, SUBLANES), 0))
    if tile_major:
        out_spec = pl.BlockSpec((None, tm, tn), lambda i, j: (j, i, 0))
        out_shape = jax.ShapeDtypeStruct((n // tn, m, tn), F32)
    else:
        out_spec = pl.BlockSpec((tm, tn), lambda i, j: (i, j))
        out_shape = jax.ShapeDtypeStruct((m, n), F32)
    return pl.pallas_call(
        _mm_nt_kernel,
        grid=(m // tm, n // tn),
        in_specs=[pl.BlockSpec((tm, k), lambda i, j: (i, 0)), w_spec],
        out_specs=out_spec,
        out_shape=out_shape,
        compiler_params=_cparams(("arbitrary", "arbitrary")),
        name=name,
    )(a, wt)


def _matmul(a, w, *, layer=None, col0=0, n=None, tile_major=False, out_dtype=F32, tn_pref=MM_TN, name="mm"):
    m, k = a.shape
    n = w.shape[-1] if n is None else n
    tm = _pick(m, MM_TM)
    tn = _pick(n, tn_pref)
    assert col0 % tn == 0
    j0 = col0 // tn
    if w.ndim == 3:
        w_spec = pl.BlockSpec((None, k, tn), lambda i, j: (layer, 0, j0 + j))
    else:
        w_spec = pl.BlockSpec((k, tn), lambda i, j: (0, j0 + j))
    if tile_major:
        out_spec = pl.BlockSpec((None, tm, tn), lambda i, j: (j, i, 0))
        out_shape = jax.ShapeDtypeStruct((n // tn, m, tn), out_dtype)
    else:
        out_spec = pl.BlockSpec((tm, tn), lambda i, j: (i, j))
        out_shape = jax.ShapeDtypeStruct((m, n), out_dtype)
    return pl.pallas_call(
        _mm_kernel,
        grid=(m // tm, n // tn),
        in_specs=[pl.BlockSpec((tm, k), lambda i, j: (i, 0)), w_spec],
        out_specs=out_spec,
        out_shape=out_shape,
        compiler_params=_cparams(("arbitrary", "arbitrary")),
        name=name,
    )(a, w)


def _upmerge_kernel(odn_p, os5_p, oret_p, odn_s, os5_s, oret_s, gdn, gs5, gret, wdn, ws5, wret, o_ref,
                    *, n_prompt_tiles):
    i = pl.program_id(0)

    def branch(o, w, g):
        return jax.nn.sigmoid(g[...]) * jnp.dot(o[...], w[...].astype(BF16), preferred_element_type=F32)

    def run(odn, os5, oret):
        o_ref[...] = (branch(odn, wdn, gdn) + branch(os5, ws5, gs5) + branch(oret, wret, gret)).astype(o_ref.dtype)

    @pl.when(i < n_prompt_tiles)
    def _():
        run(odn_p, os5_p, oret_p)

    @pl.when(i >= n_prompt_tiles)
    def _():
        run(odn_s, os5_s, oret_s)


def _upmerge(outs_p, outs_s, gates, w_dn, w_s5, w_ret, layer):
    tp, ts = outs_p[0].shape[0], outs_s[0].shape[0]
    t = tp + ts
    d = w_dn.shape[-1]
    tm = _pick(math.gcd(tp, ts), MM_TM)
    tn = _pick(d, MM_TN)
    nj = d // tn
    npt = tp // tm
    nst = ts // tm

    def p_spec(a):
        return pl.BlockSpec((tm, a.shape[1]), lambda i, j: (jnp.minimum(i, npt - 1), 0))

    def s_spec(a):
        return pl.BlockSpec((tm, a.shape[1]), lambda i, j: (jnp.maximum(i - npt, 0), 0))

    def g_spec(off):
        return pl.BlockSpec((tm, tn), lambda i, j: (i, off * nj + j))

    def w_spec(w):
        return pl.BlockSpec((None, w.shape[1], tn), lambda i, j: (layer, 0, j))

    return pl.pallas_call(
        functools.partial(_upmerge_kernel, n_prompt_tiles=npt),
        grid=(npt + nst, nj),
        in_specs=[p_spec(a) for a in outs_p] + [s_spec(a) for a in outs_s]
        + [g_spec(0), g_spec(1), g_spec(2), w_spec(w_dn), w_spec(w_s5), w_spec(w_ret)],
        out_specs=pl.BlockSpec((tm, tn), lambda i, j: (i, j)),
        out_shape=jax.ShapeDtypeStruct((t, d), BF16),
        compiler_params=_cparams(("arbitrary", "arbitrary")),
        name="upmerge",
    )(*outs_p, *outs_s, gates, gates, gates, w_dn, w_s5, w_ret)


def _chunk_masks(r, c):
    ti = lax.broadcasted_iota(jnp.int32, (r, r), 0)
    si = lax.broadcasted_iota(jnp.int32, (r, r), 1)
    same = (ti // c) == (si // c)
    return same, same & (si <= ti), same & (si < ti)


def _shift_rows_carry(x, prev8, s):
    rolled = pltpu.roll(x, s, 0)
    prev_rolled = pltpu.roll(prev8, s, 0)
    row8 = lax.broadcasted_iota(jnp.int32, prev8.shape, 0)
    first = jnp.where(row8 < s, prev_rolled, rolled[0:SUBLANES])
    return jnp.concatenate([first, rolled[SUBLANES:]], axis=0)


def _shift_rows_seq8(x, bufx, s, hist):
    r = x.shape[0]
    rolled = pltpu.roll(x, s, 0)
    brolled = pltpu.roll(bufx, (r - (hist - s)) % r, 0)
    t8 = lax.broadcasted_iota(jnp.int32, x.shape, 0) % SUBLANES
    return jnp.where(t8 < s, brolled, rolled)


def _causal_conv(x, w_ref, shift):
    n = w_ref.shape[0]
    y = w_ref[n - 1:n, :] * x
    for s in range(1, n):
        y = y + w_ref[n - 1 - s:n - s, :] * shift(s)
    return y


def _silu(x):
    return x * jax.nn.sigmoid(x)


def _l2norm(x):
    return x * lax.rsqrt(jnp.sum(x * x, -1, keepdims=True) + RMS_EPS)


def _rms(x):
    return x * lax.rsqrt(jnp.mean(x * x, -1, keepdims=True) + RMS_EPS)


def _split3(x):
    h = x.astype(BF16)
    r1 = x - h.astype(F32)
    m = r1.astype(BF16)
    l = (r1 - m.astype(F32)).astype(BF16)
    return h, m, l


def _delta_gate_kernel(ba_ref, bat_ref, hpr_ref, hpc_ref, beta_ref, gcum_ref, gtot_ref, rows_ref, *, nh, c):
    r = ba_ref.shape[0]
    same, incl, _ = _chunk_masks(r, c)
    ti = lax.broadcasted_iota(jnp.int32, (r, r), 0)
    si = lax.broadcasted_iota(jnp.int32, (r, r), 1)
    incl_t = (same & (ti <= si)).astype(BF16)
    inclb = incl.astype(BF16)
    sameb = same.astype(BF16)
    dot = functools.partial(jnp.dot, preferred_element_type=F32)
    ba = ba_ref[...]
    lane = lax.broadcasted_iota(jnp.int32, ba.shape, 1)
    beta_ref[...] = jax.nn.sigmoid(ba)
    g_cols = -jnp.exp(hpr_ref[0:1, :]) * jax.nn.softplus(ba + hpr_ref[1:2, :])
    g_cols = jnp.where((lane >= nh) & (lane < 2 * nh), g_cols, 0.0)
    h, m, l = _split3(g_cols)
    gcum_ref[...] = dot(inclb, h) + dot(inclb, m) + dot(inclb, l)
    gtot_ref[...] = dot(sameb, h) + dot(sameb, m) + dot(sameb, l)
    g_rows = -jnp.exp(hpc_ref[:, 0:1]) * jax.nn.softplus(bat_ref[...] + hpc_ref[:, 1:2])
    h, m, l = _split3(g_rows)
    rows_ref[...] = dot(h, incl_t) + dot(m, incl_t) + dot(l, incl_t)


def _delta_gate(ba, bat, hpr, hpc, *, nh, c, row0, nrows):
    r = min(ROW_TILE, nrows)
    t0 = row0 // r
    nh2 = 2 * nh
    col = pl.BlockSpec((r, LANES), lambda i: (i, 0))
    return pl.pallas_call(
        functools.partial(_delta_gate_kernel, nh=nh, c=c),
        grid=(nrows // r,),
        in_specs=[pl.BlockSpec((r, LANES), lambda i: (t0 + i, 0)),
                  pl.BlockSpec((nh2, r), lambda i: (0, t0 + i)),
                  pl.BlockSpec((2, LANES), lambda i: (0, 0)),
                  pl.BlockSpec((nh2, 2), lambda i: (0, 0))],
        out_specs=[col, col, col, pl.BlockSpec((nh2, r), lambda i: (0, i))],
        out_shape=[jax.ShapeDtypeStruct((nrows, LANES), F32)] * 3 + [jax.ShapeDtypeStruct((nh2, nrows), F32)],
        compiler_params=_cparams(("arbitrary",)),
        name="delta_gate",
    )(ba, bat, hpr, hpc)


def _pick_col(cols, idx):
    lane = lax.broadcasted_iota(jnp.int32, cols.shape, 1)
    return jnp.sum(jnp.where(lane == idx, cols, 0.0), -1, keepdims=True)


def _delta_prep(heads, c):
    r, dk = heads[0][1].shape
    _, incl, strict = _chunk_masks(r, c)
    ms, as_ = [], []
    for q, k, v, beta_c, gc_c, gt_c, gc_r in heads:
        dec = jnp.exp(jnp.minimum(gc_c - gc_r, 0.0))
        ms.append(jnp.where(strict, beta_c * dec * _bdot_nt(k, k), 0.0))
        as_.append(jnp.where(incl, dec * _bdot_nt(q, k), 0.0))
    eye = (lax.broadcasted_iota(jnp.int32, (r, r), 0) == lax.broadcasted_iota(jnp.int32, (r, r), 1)).astype(F32)
    xs = [eye - m for m in ms]
    pws = ms
    for _ in range(int(math.log2(c)) - 1):
        pws = [_bdot(pw, pw) for pw in pws]
        xs = [x + _bdot(x, pw) for x, pw in zip(xs, pws)]
    out = []
    for (q, k, v, beta_c, gc_c, gt_c, gc_r), a, x in zip(heads, as_, xs):
        gin = jnp.exp(gc_c)
        wu = _bdot(x, jnp.concatenate([beta_c * gin * k, beta_c * v], axis=1))
        out.append((a, wu[:, :dk], wu[:, dk:], k * jnp.exp(gt_c - gc_c), jnp.exp(gt_c), gin))
    return out


def _delta_head_inputs(qa, ka, va, beta_ref, gcum_ref, gtot_ref, rows_ref, i, h, nh, dk):
    cols = slice(i * dk, (i + 1) * dk)
    q = _l2norm(qa[:, cols]) * (dk ** -0.5)
    k = _l2norm(ka[:, cols])
    beta_c = _pick_col(beta_ref[...], h)
    gc_c = _pick_col(gcum_ref[...], nh + h)
    gt_c = _pick_col(gtot_ref[...], nh + h)
    gc_r = rows_ref[pl.ds(nh + h, 1), :]
    return q, k, va[:, cols], beta_c, gc_c, gt_c, gc_r


def _delta_finish(a, u, qs, z, nw_ref):
    o = qs + _bdot(a, u)
    return _rms(o) * nw_ref[...] * _silu(z)


def _delta_prompt_kernel(q_ref, k_ref, v_ref, z_ref, beta_ref, gcum_ref, gtot_ref, rows_ref,
                         cwq_ref, cwk_ref, cwv_ref, nw_ref,
                         o_ref, st_ref,
                         s_sc, pq_sc, pk_sc, pv_sc,
                         *, nh, hb, c, dk):
    p = pl.program_id(1)
    j = pl.program_id(2)
    r = q_ref.shape[0]

    @pl.when(j == 0)
    def _():
        s_sc[...] = jnp.zeros_like(s_sc)
        pq_sc[...] = jnp.zeros_like(pq_sc)
        pk_sc[...] = jnp.zeros_like(pk_sc)
        pv_sc[...] = jnp.zeros_like(pv_sc)

    def conv(x_ref, w_ref, prev_sc):
        x = x_ref[...]
        prev8 = prev_sc[...]
        y = _causal_conv(x, w_ref, lambda s: _shift_rows_carry(x, prev8, s))
        prev_sc[...] = x[r - SUBLANES:, :]
        return _silu(y)

    qa = conv(q_ref, cwq_ref, pq_sc)
    ka = conv(k_ref, cwk_ref, pk_sc)
    va = conv(v_ref, cwv_ref, pv_sc)
    z = z_ref[...]

    heads = [_delta_head_inputs(qa, ka, va, beta_ref, gcum_ref, gtot_ref, rows_ref, i, p * hb + i, nh, dk)
             for i in range(hb)]
    preps = _delta_prep(heads, c)
    nchunk = r // c
    chunk_rows = [slice(n * c, (n + 1) * c) for n in range(nchunk)]
    kws = [[_bdot_tn(pr[3][rows], pr[1][rows]) for rows in chunk_rows] for pr in preps]
    bcs = [[_bdot_tn(pr[3][rows], pr[2][rows]) for rows in chunk_rows] for pr in preps]
    ss = [s_sc[i] for i in range(hb)]
    us = [[] for _ in range(hb)]
    qss = [[] for _ in range(hb)]
    for n, rows in enumerate(chunk_rows):
        for i in range(hb):
            q = heads[i][0]
            _, w, u0, kd, eg, gin = preps[i]
            s = ss[i]
            us[i].append(u0[rows] - _bdot(w[rows], s))
            qss[i].append(gin[rows] * _bdot(q[rows], s))
            ss[i] = (eg[n * c:n * c + 1] * s + bcs[i][n]) - _bdot(kws[i][n], s)
    for i in range(hb):
        s_sc[i] = ss[i]
        st_ref[0, i] = ss[i]
        cols = slice(i * dk, (i + 1) * dk)
        o = _delta_finish(preps[i][0], jnp.concatenate(us[i], axis=0), jnp.concatenate(qss[i], axis=0),
                          z[:, cols], nw_ref)
        o_ref[:, cols] = o.astype(o_ref.dtype)


def _delta_sample_kernel(q_ref, k_ref, v_ref, z_ref, beta_ref, gcum_ref, gtot_ref, rows_ref,
                         cwq_ref, cwk_ref, cwv_ref, nw_ref, bq_ref, bk_ref, bv_ref, sin_ref,
                         o_ref, st_ref,
                         q_s, w_s, u0_s, kd_s, eg_s, gin_s, u_s, qs_s,
                         *, nh, hb, c, dk, hist):
    p = pl.program_id(1)
    r = q_ref.shape[0]

    def conv(x_ref, w_ref, b_ref):
        x = x_ref[...]
        bufx = b_ref[...]
        return _silu(_causal_conv(x, w_ref, lambda s: _shift_rows_seq8(x, bufx, s, hist)))

    qa = conv(q_ref, cwq_ref, bq_ref)
    ka = conv(k_ref, cwk_ref, bk_ref)
    va = conv(v_ref, cwv_ref, bv_ref)
    z = z_ref[...]

    heads = [_delta_head_inputs(qa, ka, va, beta_ref, gcum_ref, gtot_ref, rows_ref, i, p * hb + i, nh, dk)
             for i in range(hb)]
    for i in range(hb):
        q = heads[i][0]
        a, w, u0, kd, eg, gin = _delta_prep(heads[i:i + 1], c)[0]
        q_s[...] = q
        w_s[...] = w
        u0_s[...] = u0
        kd_s[...] = kd
        eg_s[...] = jnp.broadcast_to(eg, eg_s.shape)
        gin_s[...] = jnp.broadcast_to(gin, gin_s.shape)

        def group(gi, carry, i=i):
            seqs = [gi * SAMPLE_GROUP + j for j in range(SAMPLE_GROUP)]
            rows = [pl.ds(pl.multiple_of(n * c, c), c) for n in seqs]
            ss = [sin_ref[n, i] for n in seqs]
            wqs = [_bdot(jnp.concatenate([w_s[rw, :], q_s[rw, :]], axis=0), s) for rw, s in zip(rows, ss)]
            us = [u0_s[rw, :] - wq[:c] for rw, wq in zip(rows, wqs)]
            for rw, u, wq in zip(rows, us, wqs):
                u_s[rw, :] = u
                qs_s[rw, :] = gin_s[rw, :] * wq[c:]
            for n, rw, s, u in zip(seqs, rows, ss, us):
                st_ref[n, i] = eg_s[pl.ds(pl.multiple_of(n * c, c), 1), :] * s + _bdot_tn(kd_s[rw, :], u)
            return carry

        lax.fori_loop(0, r // c // SAMPLE_GROUP, group, 0)
        cols = slice(i * dk, (i + 1) * dk)
        o_ref[:, cols] = _delta_finish(a, u_s[...], qs_s[...], z[:, cols], nw_ref).astype(o_ref.dtype)


def _delta_scratch(r, dk, dv):
    return [pltpu.VMEM((r, dk), F32),
            pltpu.VMEM((r, dk), F32),
            pltpu.VMEM((r, dv), F32),
            pltpu.VMEM((r, dk), F32),
            pltpu.VMEM((r, dv), F32),
            pltpu.VMEM((r, dv), F32),
            pltpu.VMEM((r, dv), F32),
            pltpu.VMEM((r, dv), F32)]


def _delta_common_specs(tiles_per_seg, row_map, local_map, taps, dv, layer, tn, r, nh2):
    def seg(n):
        return pl.BlockSpec((None, r, tn), lambda *g: (n * tiles_per_seg + g[1], row_map(*g), 0))

    def cw(n):
        return pl.BlockSpec((None, taps, tn), lambda *g: (layer, 0, n * tiles_per_seg + g[1]))

    col = pl.BlockSpec((r, LANES), lambda *g: (local_map(*g), 0))
    return [seg(0), seg(1), seg(2), seg(3), col, col, col,
            pl.BlockSpec((nh2, r), lambda *g: (0, local_map(*g))),
            cw(0), cw(1), cw(2),
            pl.BlockSpec((None, 1, dv), lambda *g: (layer, 0, 0))]


def _delta_prompt(p_dn, gates, conv_w, norm_w, layer, *, nb, seq, nh, dk, dv, row0):
    tn = p_dn.shape[-1]
    hb = tn // dk
    r = min(ROW_TILE, seq)
    c = math.gcd(seq, CHUNK)
    npair = nh // hb
    nt = seq // r
    t0 = row0 // r
    kern = functools.partial(_delta_prompt_kernel, nh=nh, hb=hb, c=c, dk=dk)
    specs = _delta_common_specs(npair, lambda b, p, j: t0 + b * nt + j, lambda b, p, j: b * nt + j,
                                conv_w.shape[1], dv, layer, tn, r, 2 * nh)
    return pl.pallas_call(
        kern,
        grid=(nb, npair, nt),
        in_specs=specs,
        out_specs=[pl.BlockSpec((r, tn), lambda b, p, j: (b * nt + j, p)),
                   pl.BlockSpec((1, hb, dk, dv), lambda b, p, j: (b, p, 0, 0))],
        out_shape=[jax.ShapeDtypeStruct((nb * seq, nh * dv), BF16),
                   jax.ShapeDtypeStruct((nb, nh, dk, dv), F32)],
        scratch_shapes=[pltpu.VMEM((hb, dk, dv), F32),
                        pltpu.VMEM((SUBLANES, tn), F32), pltpu.VMEM((SUBLANES, tn), F32),
                        pltpu.VMEM((SUBLANES, tn), F32)],
        compiler_params=_cparams(("arbitrary", "arbitrary", "arbitrary")),
        name="delta_prompt",
    )(p_dn, p_dn, p_dn, p_dn, *gates, conv_w, conv_w, conv_w, norm_w)


def _delta_sample(p_dn, gates, conv_w, norm_w, bufx, state, layer, *, nb, seq, nh, dk, dv, row0, hist):
    tn = p_dn.shape[-1]
    hb = tn // dk
    c = seq
    r = min(ROW_TILE, nb * seq)
    npair = nh // hb
    nt = nb * seq // r
    spt = r // seq
    t0 = row0 // r
    kern = functools.partial(_delta_sample_kernel, nh=nh, hb=hb, c=c, dk=dk, hist=hist)
    specs = _delta_common_specs(npair, lambda i, p: t0 + i, lambda i, p: i,
                                conv_w.shape[1], dv, layer, tn, r, 2 * nh)

    def buf(n):
        return pl.BlockSpec((r, tn), lambda i, p: (i, n * npair + p))

    specs += [buf(0), buf(1), buf(2),
              pl.BlockSpec((None, spt, hb, dk, dv), lambda i, p: (layer, i, p, 0, 0))]
    return pl.pallas_call(
        kern,
        grid=(nt, npair),
        in_specs=specs,
        out_specs=[pl.BlockSpec((r, tn), lambda i, p: (i, p)),
                   pl.BlockSpec((spt, hb, dk, dv), lambda i, p: (i, p, 0, 0))],
        out_shape=[jax.ShapeDtypeStruct((nb * seq, nh * dv), BF16),
                   jax.ShapeDtypeStruct((nb, nh, dk, dv), F32)],
        scratch_shapes=_delta_scratch(r, dk, dv),
        compiler_params=_cparams(("arbitrary", "arbitrary")),
        name="delta_sample",
    )(p_dn, p_dn, p_dn, p_dn, *gates, conv_w, conv_w, conv_w, norm_w, bufx, bufx, bufx, state)


def _rotary(x, cs_ref, sn_ref):
    half = x.shape[1] // 2
    return x * cs_ref[...] + pltpu.roll(x, half, 1) * sn_ref[...]


def _ret_prompt_kernel(q_ref, k_ref, v_ref, g_ref, cs_ref, sn_ref, dec_ref, qd_ref, kd_ref, cd_ref,
                       o_ref, st_ref, s_sc, *, hb, c, dk):
    j = pl.program_id(2)

    @pl.when(j == 0)
    def _():
        s_sc[...] = jnp.zeros_like(s_sc)

    r = q_ref.shape[0]
    qs, kds, vs, inners = [], [], [], []
    for i in range(hb):
        cols = slice(i * dk, (i + 1) * dk)
        q = _rotary(q_ref[:, cols], cs_ref, sn_ref)
        k = _rotary(k_ref[:, cols], cs_ref, sn_ref) * (dk ** -0.5)
        v = v_ref[:, cols]
        inners.append(_bdot(_bdot_nt(q, k) * dec_ref[i], v))
        qs.append(q)
        kds.append(k * kd_ref[i])
        vs.append(v)
    ss = [s_sc[i] for i in range(hb)]
    qrs = [[] for _ in range(hb)]
    for n in range(r // c):
        rows = slice(n * c, (n + 1) * c)
        for i in range(hb):
            qrs[i].append(_bdot(qs[i][rows], ss[i]))
            ss[i] = cd_ref[i, 0:1, :] * ss[i] + _bdot_tn(kds[i][rows], vs[i][rows])
    for i in range(hb):
        cols = slice(i * dk, (i + 1) * dk)
        s_sc[i] = ss[i]
        st_ref[0, i] = ss[i]
        o = inners[i] + qd_ref[i] * jnp.concatenate(qrs[i], axis=0)
        o_ref[:, cols] = (_rms(o) * _silu(g_ref[:, cols])).astype(o_ref.dtype)


def _ret_sample_kernel(q_ref, k_ref, v_ref, g_ref, cs_ref, sn_ref, dec_ref, qd_ref, kd_ref, cd_ref, sin_ref,
                       o_ref, st_ref, q_s, kd_s, v_s, qr_s, *, hb, c, dk):
    r = q_ref.shape[0]
    for i in range(hb):
        cols = slice(i * dk, (i + 1) * dk)
        q = _rotary(q_ref[:, cols], cs_ref, sn_ref)
        k = _rotary(k_ref[:, cols], cs_ref, sn_ref) * (dk ** -0.5)
        v = v_ref[:, cols]
        inner = _bdot(_bdot_nt(q, k) * dec_ref[i], v)
        q_s[...] = q
        kd_s[...] = k * kd_ref[i]
        v_s[...] = v
        cd = cd_ref[i, 0:1, :]

        def group(gi, carry, i=i, cd=cd):
            seqs = [gi * SAMPLE_GROUP + j for j in range(SAMPLE_GROUP)]
            rows = [pl.ds(pl.multiple_of(n * c, c), c) for n in seqs]
            ss = [sin_ref[n, i] for n in seqs]
            for rw, s in zip(rows, ss):
                qr_s[rw, :] = _bdot(q_s[rw, :], s)
            for n, rw, s in zip(seqs, rows, ss):
                st_ref[n, i] = cd * s + _bdot_tn(kd_s[rw, :], v_s[rw, :])
            return carry

        lax.fori_loop(0, r // c // SAMPLE_GROUP, group, 0)
        o = inner + qd_ref[i] * qr_s[...]
        o_ref[:, cols] = (_rms(o) * _silu(g_ref[:, cols])).astype(o_ref.dtype)


def _ret_tables(nh, r, c, dk):
    log_gamma = jnp.log1p(-jnp.exp2(-5.0 - jnp.arange(nh, dtype=F32)))
    ti = jnp.arange(r)
    same = (ti[:, None] // c) == (ti[None, :] // c)
    rel = (ti[:, None] - ti[None, :]).astype(F32)
    dec = jnp.where(same & (rel >= 0), jnp.exp(jnp.maximum(rel, 0.0) * log_gamma[:, None, None]), 0.0)
    pos = (ti % c).astype(F32)
    qd = jnp.exp((pos + 1.0) * log_gamma[:, None])
    kd = jnp.exp((c - 1.0 - pos) * log_gamma[:, None])
    cd = jnp.exp(c * log_gamma)
    bc = lambda a: jnp.broadcast_to(a[..., None], a.shape + (dk,))
    return dec, bc(qd), bc(kd), jnp.broadcast_to(cd[:, None, None], (nh, SUBLANES, dk))


def _rope_tables(pos, dk):
    half = dk // 2
    inv = ROPE_BASE ** (-jnp.arange(half, dtype=F32) / half)
    ang = pos.astype(F32)[:, None] * inv[None, :]
    cos, sin = jnp.cos(ang), jnp.sin(ang)
    return jnp.concatenate([cos, cos], -1), jnp.concatenate([-sin, sin], -1)


def _ret_specs(npair, row_map, pos_map, tn, r, hb, dk):
    def seg(n):
        return pl.BlockSpec((None, r, tn), lambda *g: (n * npair + g[1], row_map(*g), 0))

    def tab(shape):
        return pl.BlockSpec((hb,) + shape, lambda *g: (g[1],) + (0,) * len(shape))

    return [seg(0), seg(1), seg(2), seg(3),
            pl.BlockSpec((r, dk), lambda *g: (pos_map(*g), 0)),
            pl.BlockSpec((r, dk), lambda *g: (pos_map(*g), 0)),
            tab((r, r)), tab((r, dk)), tab((r, dk)), tab((SUBLANES, dk))]


def _ret_scratch(r, dk, dv):
    return [pltpu.VMEM((r, dk), F32), pltpu.VMEM((r, dk), F32), pltpu.VMEM((r, dv), F32),
            pltpu.VMEM((r, dv), F32)]


def _ret_prompt(p_ret, cs, sn, *, nb, seq, nh, dk, dv, row0):
    tn = p_ret.shape[-1]
    hb = tn // dk
    r = min(ROW_TILE, seq)
    c = math.gcd(seq, CHUNK)
    npair = nh // hb
    nt = seq // r
    t0 = row0 // r
    dec, qd, kd, cd = _ret_tables(nh, r, c, dk)
    specs = _ret_specs(npair, lambda b, p, j: t0 + b * nt + j, lambda b, p, j: j, tn, r, hb, dk)
    return pl.pallas_call(
        functools.partial(_ret_prompt_kernel, hb=hb, c=c, dk=dk),
        grid=(nb, npair, nt),
        in_specs=specs,
        out_specs=[pl.BlockSpec((r, tn), lambda b, p, j: (b * nt + j, p)),
                   pl.BlockSpec((1, hb, dk, dv), lambda b, p, j: (b, p, 0, 0))],
        out_shape=[jax.ShapeDtypeStruct((nb * seq, nh * dv), BF16),
                   jax.ShapeDtypeStruct((nb, nh, dk, dv), F32)],
        scratch_shapes=[pltpu.VMEM((hb, dk, dv), F32)],
        compiler_params=_cparams(("arbitrary", "arbitrary", "arbitrary")),
        name="ret_prompt",
    )(p_ret, p_ret, p_ret, p_ret, cs, sn, dec, qd, kd, cd)


def _ret_sample(p_ret, cs, sn, state, layer, *, nb, seq, nh, dk, dv, row0):
    tn = p_ret.shape[-1]
    hb = tn // dk
    c = seq
    r = min(ROW_TILE, nb * seq)
    npair = nh // hb
    nt = nb * seq // r
    spt = r // seq
    t0 = row0 // r
    dec, qd, kd, cd = _ret_tables(nh, r, c, dk)
    specs = _ret_specs(npair, lambda i, p: t0 + i, lambda i, p: 0, tn, r, hb, dk)
    specs += [pl.BlockSpec((None, spt, hb, dk, dv), lambda i, p: (layer, i, p, 0, 0))]
    return pl.pallas_call(
        functools.partial(_ret_sample_kernel, hb=hb, c=c, dk=dk),
        grid=(nt, npair),
        in_specs=specs,
        out_specs=[pl.BlockSpec((r, tn), lambda i, p: (i, p)),
                   pl.BlockSpec((spt, hb, dk, dv), lambda i, p: (i, p, 0, 0))],
        out_shape=[jax.ShapeDtypeStruct((nb * seq, nh * dv), BF16),
                   jax.ShapeDtypeStruct((nb, nh, dk, dv), F32)],
        scratch_shapes=_ret_scratch(r, dk, dv),
        compiler_params=_cparams(("arbitrary", "arbitrary")),
        name="ret_sample",
    )(p_ret, p_ret, p_ret, p_ret, cs, sn, dec, qd, kd, cd, state)


S5_COL_CHUNK = 1024


def _s5_project_in(u_ref, bblk_ref, bur_s, bui_s):
    nblk = bblk_ref.shape[0]
    half = bblk_ref.shape[2] // 2
    for i in range(nblk):
        res = _bdot(u_ref[:, i * LANES:(i + 1) * LANES], bblk_ref[i])
        bur_s[:, i * half:(i + 1) * half] = res[:, :half]
        bui_s[:, i * half:(i + 1) * half] = res[:, half:]


def _s5_scan_rows(bur_s, bui_s, lam_ref, cols, row_start, nrows, hr, hi):
    lr = lam_ref[0:1, cols]
    li = lam_ref[1:2, cols]

    def step(t, carry):
        hr, hi = carry
        row = pl.ds(row_start + t, 1)
        nr = lr * hr - li * hi + bur_s[row, cols]
        ni = lr * hi + li * hr + bui_s[row, cols]
        bur_s[row, cols] = nr
        bui_s[row, cols] = ni
        return nr, ni

    return lax.fori_loop(0, nrows, step, (hr, hi))


def _s5_project_out(u_ref, bur_s, bui_s, cblk_ref, d_ref, wglu_ref, bglu_ref, o_ref):
    nblk = cblk_ref.shape[0]
    half = cblk_ref.shape[1] // 2
    ys = []
    for i in range(nblk):
        cb = cblk_ref[i]
        y = _bdot(bur_s[:, i * half:(i + 1) * half], cb[:half]) + _bdot(bui_s[:, i * half:(i + 1) * half], cb[half:])
        ys.append(y)
    y = jnp.concatenate(ys, axis=1) + d_ref[...] * u_ref[...]
    z = jax.nn.gelu(y)
    z = z * jax.nn.sigmoid(_bdot(z, wglu_ref[...]) + bglu_ref[...])
    o_ref[...] = z.astype(o_ref.dtype)


def _s5_prompt_kernel(u_ref, bblk_ref, cblk_ref, lam_ref, d_ref, wglu_ref, bglu_ref,
                      o_ref, hr_ref, hi_ref, bur_s, bui_s, hc_s):
    j = pl.program_id(1)
    r = u_ref.shape[0]
    ns = bur_s.shape[1]

    @pl.when(j == 0)
    def _():
        hc_s[...] = jnp.zeros_like(hc_s)

    _s5_project_in(u_ref, bblk_ref, bur_s, bui_s)
    cw = min(S5_COL_CHUNK, ns)
    for cc in range(ns // cw):
        cols = slice(cc * cw, (cc + 1) * cw)
        hr, hi = _s5_scan_rows(bur_s, bui_s, lam_ref, cols, 0, r, hc_s[0:1, cols], hc_s[1:2, cols])
        hc_s[0:1, cols] = hr
        hc_s[1:2, cols] = hi
    hr_ref[0] = hc_s[0:1, :]
    hi_ref[0] = hc_s[1:2, :]
    _s5_project_out(u_ref, bur_s, bui_s, cblk_ref, d_ref, wglu_ref, bglu_ref, o_ref)


def _s5_sample_kernel(u_ref, bblk_ref, cblk_ref, lam_ref, d_ref, wglu_ref, bglu_ref, h0r_ref, h0i_ref,
                      o_ref, hr_ref, hi_ref, bur_s, bui_s, *, seq):
    r = u_ref.shape[0]
    ns = bur_s.shape[1]
    _s5_project_in(u_ref, bblk_ref, bur_s, bui_s)
    cw = min(S5_COL_CHUNK, ns)
    for cc in range(ns // cw):
        cols = slice(cc * cw, (cc + 1) * cw)

        def per_seq(b, carry, cols=cols):
            hr, hi = _s5_scan_rows(bur_s, bui_s, lam_ref, cols, b * seq, seq,
                                   h0r_ref[pl.ds(b, 1), cols], h0i_ref[pl.ds(b, 1), cols])
            hr_ref[pl.ds(b, 1), cols] = hr
            hi_ref[pl.ds(b, 1), cols] = hi
            return carry

        lax.fori_loop(0, r // seq, per_seq, 0)
    _s5_project_out(u_ref, bur_s, bui_s, cblk_ref, d_ref, wglu_ref, bglu_ref, o_ref)


def _s5_params(lam_re, lam_im, log_step, b_re, b_im, c_re, c_im):
    g, p = lam_re.shape
    gs = b_re.shape[-1]
    dt = jnp.exp(log_step.astype(F32))[:, None]
    ar, ai = lam_re.astype(F32), lam_im.astype(F32)
    mag = jnp.exp(ar * dt)
    lbr, lbi = mag * jnp.cos(ai * dt), mag * jnp.sin(ai * dt)
    den = ar * ar + ai * ai
    nr, ni = lbr - 1.0, lbi
    cr = (nr * ar + ni * ai) / den
    ci = (ni * ar - nr * ai) / den
    bbr = cr[..., None] * b_re - ci[..., None] * b_im
    bbi = cr[..., None] * b_im + ci[..., None] * b_re
    gb = S5_GROUP_BLOCK
    nblk = g // gb
    eye = jnp.eye(gb, dtype=F32)
    bb = jnp.stack([bbr, bbi]).reshape(2, nblk, gb, p, gs)
    bblk = jnp.einsum('qigpc,gh->igcqhp', bb, eye).reshape(nblk, gb * gs, 2 * gb * p)
    cc = jnp.stack([c_re.astype(F32), -c_im.astype(F32)]).reshape(2, nblk, gb, gs, p)
    cblk = jnp.einsum('qigcp,gh->iqgphc', cc, eye).reshape(nblk, 2 * gb * p, gb * gs)
    lam = jnp.stack([lbr.reshape(-1), lbi.reshape(-1)])
    lam = jnp.concatenate([lam, jnp.zeros((SUBLANES - 2, g * p), F32)], axis=0)
    return bblk, cblk, lam


def _s5_const_specs(bblk, cblk, lam, w, layer, nmap):
    z = lambda n: (lambda *g: (0,) * n)
    return [pl.BlockSpec(bblk.shape, z(3)), pl.BlockSpec(cblk.shape, z(3)), pl.BlockSpec(lam.shape, z(2)),
            pl.BlockSpec((None, 1, w), lambda *g: (layer, 0, 0)),
            pl.BlockSpec((None, w, w), lambda *g: (layer, 0, 0)),
            pl.BlockSpec((None, 1, w), lambda *g: (layer, 0, 0))]


def _s5_prompt(u, bblk, cblk, lam, d, wglu, bglu, layer, *, nb, seq, row0):
    w = u.shape[1]
    ns = lam.shape[1]
    r = min(ROW_TILE, seq)
    nt = seq // r
    t0 = row0 // r
    return pl.pallas_call(
        _s5_prompt_kernel,
        grid=(nb, nt),
        in_specs=[pl.BlockSpec((r, w), lambda b, j: (t0 + b * nt + j, 0))]
        + _s5_const_specs(bblk, cblk, lam, w, layer, 2),
        out_specs=[pl.BlockSpec((r, w), lambda b, j: (b * nt + j, 0)),
                   pl.BlockSpec((1, 1, ns), lambda b, j: (b, 0, 0)),
                   pl.BlockSpec((1, 1, ns), lambda b, j: (b, 0, 0))],
        out_shape=[jax.ShapeDtypeStruct((nb * seq, w), BF16),
                   jax.ShapeDtypeStruct((nb, 1, ns), F32), jax.ShapeDtypeStruct((nb, 1, ns), F32)],
        scratch_shapes=[pltpu.VMEM((r, ns), F32), pltpu.VMEM((r, ns), F32), pltpu.VMEM((SUBLANES, ns), F32)],
        compiler_params=_cparams(("arbitrary", "arbitrary")),
        name="s5_prompt",
    )(u, bblk, cblk, lam, d, wglu, bglu)


def _s5_sample(u, bblk, cblk, lam, d, wglu, bglu, h0r, h0i, layer, *, nb, seq, row0):
    w = u.shape[1]
    ns = lam.shape[1]
    r = min(ROW_TILE, nb * seq)
    nt = nb * seq // r
    spt = r // seq
    t0 = row0 // r
    return pl.pallas_call(
        functools.partial(_s5_sample_kernel, seq=seq),
        grid=(nt,),
        in_specs=[pl.BlockSpec((r, w), lambda i: (t0 + i, 0))]
        + _s5_const_specs(bblk, cblk, lam, w, layer, 1)
        + [pl.BlockSpec((spt, ns), lambda i: (i, 0)), pl.BlockSpec((spt, ns), lambda i: (i, 0))],
        out_specs=[pl.BlockSpec((r, w), lambda i: (i, 0)),
                   pl.BlockSpec((spt, ns), lambda i: (i, 0)),
                   pl.BlockSpec((spt, ns), lambda i: (i, 0))],
        out_shape=[jax.ShapeDtypeStruct((nb * seq, w), BF16),
                   jax.ShapeDtypeStruct((nb, ns), F32), jax.ShapeDtypeStruct((nb, ns), F32)],
        scratch_shapes=[pltpu.VMEM((r, ns), F32), pltpu.VMEM((r, ns), F32)],
        compiler_params=_cparams(("arbitrary",)),
        name="s5_sample",
    )(u, bblk, cblk, lam, d, wglu, bglu, h0r, h0i)


def _top2(x):
    n = x.shape[-1]
    ids = jnp.arange(n, dtype=jnp.int32)
    i1 = jnp.argmax(x, axis=-1).astype(jnp.int32)
    m1 = jnp.max(x, axis=-1)
    rest = jnp.where(ids == i1[..., None], -jnp.inf, x)
    i2 = jnp.argmax(rest, axis=-1).astype(jnp.int32)
    m2 = jnp.max(rest, axis=-1)
    return jnp.stack([m1, m2], -1), jnp.stack([i1, i2], -1)


def _route(probs, n_experts):
    epg = n_experts // N_EXPERT_GROUPS
    grouped = probs.reshape(-1, N_EXPERT_GROUPS, epg)
    group_score = _top2(grouped)[0].sum(-1)
    g_sel = jnp.argmax(group_score, axis=-1).astype(jnp.int32)
    in_group = jnp.sum(jnp.where(jnp.arange(N_EXPERT_GROUPS)[None, :, None] == g_sel[:, None, None], grouped, 0.0),
                       axis=1)
    top_p, top_i = _top2(in_group)
    gate = top_p / jnp.sum(top_p, -1, keepdims=True)
    return g_sel[:, None] * epg + top_i.astype(jnp.int32), gate


def _dispatch(expert_idx, n_experts, bm):
    t = expert_idx.shape[0]
    n_pairs = t * TOP_K
    flat_e = expert_idx.reshape(-1)
    flat_tok = jnp.repeat(jnp.arange(t, dtype=jnp.int32), TOP_K)
    onehot = (flat_e[:, None] == jnp.arange(n_experts, dtype=jnp.int32)[None, :]).astype(jnp.int32)
    csum = jnp.cumsum(onehot, axis=0)
    counts = csum[-1]
    rank = jnp.sum((csum - onehot) * onehot, axis=1)
    padded = (counts + bm - 1) // bm * bm
    pstart = jnp.cumsum(padded) - padded
    pair_slot = jnp.sum(onehot * pstart[None, :], axis=1) + rank
    n_blocks = -(-(n_pairs + n_experts * (bm - 1)) // bm)
    n_slots = n_blocks * bm
    slot_tok = jnp.zeros((n_slots,), jnp.int32).at[pair_slot].set(flat_tok)
    block_exp = jnp.minimum(jnp.searchsorted(jnp.cumsum(padded), jnp.arange(n_blocks, dtype=jnp.int32) * bm,
                                             side='right'), n_experts - 1).astype(jnp.int32)
    n_used = (jnp.sum(padded) // bm).astype(jnp.int32).reshape(1)
    return slot_tok, pair_slot, block_exp, n_used, n_blocks


def _row_copy(src_hbm, dst, src_row, dst_row, sem):
    return pltpu.make_async_copy(src_hbm.at[pl.ds(src_row, 1)], dst.at[pl.ds(dst_row, 1)], sem)


def _gather_rows_kernel(tok_ref, nu_ref, x_hbm, o_ref, buf, sem):
    rows = buf.shape[1]
    i = pl.program_id(0)
    n_live = nu_ref[0]

    def issue(step):
        slot = step % 2

        @pl.when(step < n_live)
        def _():
            def body(r, c):
                _row_copy(x_hbm, buf.at[slot], tok_ref[step * rows + r], r, sem.at[slot]).start()
                return c

            lax.fori_loop(0, rows, body, 0)

    @pl.when(i == 0)
    def _():
        issue(i)

    issue(i + 1)

    @pl.when(i < n_live)
    def _():
        slot = i % 2

        def body(r, c):
            _row_copy(x_hbm, buf.at[slot], 0, r, sem.at[slot]).wait()
            return c

        lax.fori_loop(0, rows, body, 0)
        half = buf.shape[2]
        lo, hi = _unpack_bf16_pairs(buf[slot])
        o_ref[:, :half] = lo
        o_ref[:, half:] = hi

    @pl.when(i >= n_live)
    def _():
        o_ref[...] = jnp.zeros_like(o_ref)


def _gather_rows(x_packed, slot_tok, n_used, bm):
    n_slots = slot_tok.shape[0]
    x = x_packed
    d = 2 * x.shape[1]
    rows = min(bm, GATHER_ROWS)
    n_live = n_used * (bm // rows)
    return pl.pallas_call(
        _gather_rows_kernel,
        grid_spec=pltpu.PrefetchScalarGridSpec(
            num_scalar_prefetch=2,
            grid=(n_slots // rows,),
            in_specs=[pl.BlockSpec(memory_space=pl.ANY)],
            out_specs=pl.BlockSpec((rows, d), lambda i, tok, nu: (i, 0)),
            scratch_shapes=[pltpu.VMEM((2, rows, d // 2), jnp.uint32), pltpu.SemaphoreType.DMA((2,))]),
        out_shape=jax.ShapeDtypeStruct((n_slots, d), BF16),
        compiler_params=_cparams(("arbitrary",)),
        name="moe_gather",
    )(slot_tok, n_live, x)


def _expert_changed(be_ref, blk):
    return jnp.logical_or(blk == 0, be_ref[blk] != be_ref[jnp.maximum(blk - 1, 0)])


def _moe_gate_up_kernel(be_ref, nu_ref, x_ref, wg_ref, wu_ref, o_ref, wg_sc, wu_sc):
    blk = pl.program_id(1)

    @pl.when(_expert_changed(be_ref, blk))
    def _():
        wg_sc[...] = wg_ref[...].astype(BF16)
        wu_sc[...] = wu_ref[...].astype(BF16)

    @pl.when(blk < nu_ref[0])
    def _():
        x = x_ref[...]
        g = jnp.dot(x, wg_sc[...], preferred_element_type=F32)
        u = jnp.dot(x, wu_sc[...], preferred_element_type=F32)
        o_ref[...] = (_silu(g) * u).astype(o_ref.dtype)

    @pl.when(blk >= nu_ref[0])
    def _():
        o_ref[...] = jnp.zeros_like(o_ref)


def _moe_down_kernel(be_ref, nu_ref, h_ref, wd_ref, o_ref, wd_sc):
    blk = pl.program_id(1)

    @pl.when(_expert_changed(be_ref, blk))
    def _():
        wd_sc[...] = wd_ref[...].astype(BF16)

    @pl.when(blk < nu_ref[0])
    def _():
        o_ref[...] = jnp.dot(h_ref[...], wd_sc[...], preferred_element_type=F32)

    @pl.when(blk >= nu_ref[0])
    def _():
        o_ref[...] = jnp.zeros_like(o_ref)


def _moe_experts(xs, block_exp, n_used, w_gate, w_up, w_down, layer, bm):
    n_slots, d = xs.shape
    ff = w_gate.shape[-1]
    nblk = n_slots // bm
    tn = _pick(ff, MOE_TN_UP)
    w_in_spec = pl.BlockSpec((None, None, d, tn), lambda j, b, be, nu: (layer, be[b], 0, j))
    hm = pl.pallas_call(
        _moe_gate_up_kernel,
        grid_spec=pltpu.PrefetchScalarGridSpec(
            num_scalar_prefetch=2,
            grid=(ff // tn, nblk),
            in_specs=[pl.BlockSpec((bm, d), lambda j, b, be, nu: (b, 0)), w_in_spec, w_in_spec],
            out_specs=pl.BlockSpec((bm, tn), lambda j, b, be, nu: (b, j)),
            scratch_shapes=[pltpu.VMEM((d, tn), BF16), pltpu.VMEM((d, tn), BF16)]),
        out_shape=jax.ShapeDtypeStruct((n_slots, ff), BF16),
        compiler_params=_cparams(("arbitrary", "arbitrary")),
        name="moe_gate_up",
    )(block_exp, n_used, xs, w_gate, w_up)
    tn2 = _pick(d, MOE_TN_DOWN)
    return pl.pallas_call(
        _moe_down_kernel,
        grid_spec=pltpu.PrefetchScalarGridSpec(
            num_scalar_prefetch=2,
            grid=(d // tn2, nblk),
            in_specs=[pl.BlockSpec((bm, ff), lambda j, b, be, nu: (b, 0)),
                      pl.BlockSpec((None, None, ff, tn2), lambda j, b, be, nu: (layer, be[b], 0, j))],
            out_specs=pl.BlockSpec((bm, tn2), lambda j, b, be, nu: (b, j)),
            scratch_shapes=[pltpu.VMEM((ff, tn2), BF16)]),
        out_shape=jax.ShapeDtypeStruct((n_slots, d), F32),
        compiler_params=_cparams(("arbitrary", "arbitrary")),
        name="moe_down",
    )(block_exp, n_used, hm, w_down)


def _combine_ln_kernel(slot_ref, x_ref, w_ref, g_ref, b_ref, yb_hbm, of_ref, ob_ref, buf, sem, *, alpha):
    tm = x_ref.shape[0]
    i = pl.program_id(0)

    def issue(step):
        slot = step % 2

        def body(r, c):
            for k in range(TOP_K):
                _row_copy(yb_hbm, buf.at[slot, k], slot_ref[(step * tm + r) * TOP_K + k], r, sem.at[slot]).start()
            return c

        lax.fori_loop(0, tm, body, 0)

    @pl.when(i == 0)
    def _():
        issue(i)

    @pl.when(i + 1 < pl.num_programs(0))
    def _():
        issue(i + 1)

    slot = i % 2

    def wait(r, c):
        for k in range(TOP_K):
            _row_copy(yb_hbm, buf.at[slot, k], 0, r, sem.at[slot]).wait()
        return c

    lax.fori_loop(0, tm, wait, 0)
    w = w_ref[...]
    f = w[:, 0:1] * buf[slot, 0] + w[:, 1:2] * buf[slot, 1]
    z = _ln_math(alpha * x_ref[...] + f, g_ref[...], b_ref[...])
    of_ref[...] = z
    ob_ref[...] = z.astype(BF16)


def _combine_ln(x, yb, pair_slot, gate, g, b, alpha, layer):
    t, d = x.shape
    tm = min(128, t)
    return pl.pallas_call(
        functools.partial(_combine_ln_kernel, alpha=alpha),
        grid_spec=pltpu.PrefetchScalarGridSpec(
            num_scalar_prefetch=1,
            grid=(t // tm,),
            in_specs=[pl.BlockSpec((tm, d), lambda i, s: (i, 0)),
                      pl.BlockSpec((tm, TOP_K), lambda i, s: (i, 0)),
                      pl.BlockSpec((None, 1, d), lambda i, s: (layer, 0, 0)),
                      pl.BlockSpec((None, 1, d), lambda i, s: (layer, 0, 0)),
                      pl.BlockSpec(memory_space=pl.ANY)],
            out_specs=[pl.BlockSpec((tm, d), lambda i, s: (i, 0)),
                       pl.BlockSpec((tm, d), lambda i, s: (i, 0))],
            scratch_shapes=[pltpu.VMEM((2, TOP_K, tm, d), F32), pltpu.SemaphoreType.DMA((2,))]),
        out_shape=[jax.ShapeDtypeStruct((t, d), F32), jax.ShapeDtypeStruct((t, d), BF16)],
        compiler_params=_cparams(("arbitrary",)),
        name="moe_combine_ln2",
    )(pair_slot, x, gate, g, b, yb)


def kernel(x_prompt, x_sample, state_delta, state_conv, state_s5, state_ret, ln_in_g, ln_in_b, w_in, dn_conv_w, dn_a_log, dn_dt_bias, dn_norm_w, s5_lam_re, s5_lam_im, s5_log_step, s5_b_re, s5_b_im, s5_c_re, s5_c_im, s5_d, s5_w_glu, s5_b_glu, w_up_dn, w_up_s5, w_up_ret, w_o, ln1_g, ln1_b, router_w, router_b, w_gate_e, w_up_e, w_down_e, ln2_g, ln2_b):
    bp, lp, d = x_prompt.shape
    bs, ls, _ = x_sample.shape
    depth = w_in.shape[0]
    nh, dk, dv = state_delta.shape[2:]
    qkv = dn_conv_w.shape[2]
    hist = state_conv.shape[2]
    dnw = nh * dv
    s5w = s5_d.shape[1]
    g5, p5 = s5_lam_re.shape[1:]
    rh, rdk, rdv = state_ret.shape[2:]
    rw = rh * rdv
    n_exp = router_w.shape[1]
    tp, ts = bp * lp, bs * ls
    t = tp + ts
    alpha = (2 * depth) ** 0.25
    assert ls == SUBLANES and qkv == 3 * dnw and dk == dv == rdk == rdv == LANES

    sizes = (qkv, dnw, nh, nh, s5w, rh * rdk, rh * rdk, rw, rw, 3 * d)
    offs = [0]
    for s in sizes:
        offs.append(offs[-1] + s)
    o_dn, o_b, o_s5, o_ret, o_gates = offs[0], offs[2], offs[4], offs[5], offs[9]
    assert w_in.shape[2] == offs[-1] and 2 * nh < LANES
    w_in_t = jnp.swapaxes(w_in, 1, 2)

    xf, xb = _ln_in(x_prompt.reshape(tp, d), x_sample.reshape(ts, d), ln_in_g, ln_in_b)

    ep = -(-n_exp // LANES) * LANES
    rw_pad = jnp.pad(router_w.astype(F32), ((0, 0), (0, ep - n_exp)))
    rb_pad = jnp.pad(router_b.astype(F32).reshape(1, n_exp), ((0, 0), (0, ep - n_exp)), constant_values=-1e30)

    cs_p, sn_p = _rope_tables(jnp.arange(lp, dtype=jnp.int32), rdk)
    cs_s, sn_s = _rope_tables(PAST_LEN + jnp.arange(ls, dtype=jnp.int32), rdk)
    rs = min(ROW_TILE, ts)
    cs_s = jnp.tile(cs_s, (rs // ls, 1))
    sn_s = jnp.tile(sn_s, (rs // ls, 1))

    new_delta_p, new_conv_p, new_s5_p, new_ret_p = [], [], [], []
    new_delta_s, new_conv_s, new_s5_s, new_ret_s = [], [], [], []

    for l in range(depth):
        p_dn = _matmul_nt(xb, w_in_t, layer=l, row0=o_dn, n=qkv + dnw, tile_major=True,
                          tn_pref=min(MIXER_TN, dnw), name="proj_dn")
        ba = _matmul_nt(xb, w_in_t, layer=l, row0=o_b, n=LANES, name="proj_ba")
        u5 = _matmul_nt(xb, w_in_t, layer=l, row0=o_s5, n=s5w, name="proj_s5")
        p_ret = _matmul_nt(xb, w_in_t, layer=l, row0=o_ret, n=4 * rw, tile_major=True,
                           tn_pref=min(MIXER_TN, rw), name="proj_ret")
        gates = _matmul_nt(xb, w_in_t, layer=l, row0=o_gates, n=3 * d, tn_pref=512, name="proj_gates")

        tn = p_dn.shape[-1]
        bat = ba[:, :2 * nh].T
        zpad = jnp.zeros((nh,), F32)
        hpr = jnp.pad(jnp.stack([jnp.concatenate([zpad, dn_a_log[l].astype(F32)]),
                                 jnp.concatenate([zpad, dn_dt_bias[l].astype(F32)])]),
                      ((0, 0), (0, LANES - 2 * nh)))
        hpc = hpr[:, :2 * nh].T
        norm_w = dn_norm_w.reshape(depth, 1, dv)

        gates_p = _delta_gate(ba, bat, hpr, hpc, nh=nh, c=math.gcd(lp, CHUNK), row0=0, nrows=tp)
        gates_s = _delta_gate(ba, bat, hpr, hpc, nh=nh, c=ls, row0=tp, nrows=ts)
        odn_p, dlt_p = _delta_prompt(p_dn, gates_p, dn_conv_w, norm_w, l,
                                     nb=bp, seq=lp, nh=nh, dk=dk, dv=dv, row0=0)
        bufx = jnp.pad(state_conv[l], ((0, 0), (0, ls - hist), (0, 0))).reshape(ts, qkv)
        odn_s, dlt_s = _delta_sample(p_dn, gates_s, dn_conv_w, norm_w, bufx, state_delta, l,
                                     nb=bs, seq=ls, nh=nh, dk=dk, dv=dv, row0=tp, hist=hist)
        nq = qkv // tn
        cp = jnp.stack([lax.slice(p_dn, (0, b * lp + lp - hist, 0), (nq, (b + 1) * lp, tn))
                        for b in range(bp)], axis=1)
        cs_ = lax.slice(p_dn, (0, tp, 0), (nq, t, tn)).reshape(nq, bs, ls, tn)[:, :, ls - hist:]
        new_conv_p.append(jnp.moveaxis(cp, 0, 2).reshape(bp, hist, qkv))
        new_conv_s.append(jnp.moveaxis(cs_, 0, 2).reshape(bs, hist, qkv))
        new_delta_p.append(dlt_p)
        new_delta_s.append(dlt_s)

        bblk, cblk, lam = _s5_params(s5_lam_re[l], s5_lam_im[l], s5_log_step[l], s5_b_re[l], s5_b_im[l],
                                     s5_c_re[l], s5_c_im[l])
        d5 = s5_d.reshape(depth, 1, s5w)
        bg5 = s5_b_glu.reshape(depth, 1, s5w)
        os5_p, hr_p, hi_p = _s5_prompt(u5, bblk, cblk, lam, d5, s5_w_glu, bg5, l, nb=bp, seq=lp, row0=0)
        h0 = state_s5[l].astype(F32)
        os5_s, hr_s, hi_s = _s5_sample(u5, bblk, cblk, lam, d5, s5_w_glu, bg5,
                                       h0[..., 0].reshape(bs, g5 * p5), h0[..., 1].reshape(bs, g5 * p5), l,
                                       nb=bs, seq=ls, row0=tp)
        new_s5_p.append(jnp.stack([hr_p.reshape(bp, g5, p5), hi_p.reshape(bp, g5, p5)], -1))
        new_s5_s.append(jnp.stack([hr_s.reshape(bs, g5, p5), hi_s.reshape(bs, g5, p5)], -1))

        oret_p, rt_p = _ret_prompt(p_ret, cs_p, sn_p, nb=bp, seq=lp, nh=rh, dk=rdk, dv=rdv, row0=0)
        oret_s, rt_s = _ret_sample(p_ret, cs_s, sn_s, state_ret, l, nb=bs, seq=ls, nh=rh, dk=rdk, dv=rdv,
                                   row0=tp)
        new_ret_p.append(rt_p)
        new_ret_s.append(rt_s)

        merged = _upmerge((odn_p, os5_p, oret_p), (odn_s, os5_s, oret_s), gates, w_up_dn, w_up_s5, w_up_ret, l)
        mix = _matmul(merged, w_o, layer=l, tn_pref=512, name="w_o")
        xf, x_packed, probs = _ln1_router(xf, mix, ln1_g.reshape(depth, 1, d), ln1_b.reshape(depth, 1, d),
                                    rw_pad, rb_pad, alpha, l)

        expert_idx, gate = _route(probs[:, :n_exp], n_exp)
        slot_tok, pair_slot, block_exp, n_used, _ = _dispatch(expert_idx, n_exp, MOE_BM)
        xs = _gather_rows(x_packed, slot_tok, n_used, MOE_BM)
        yb = _moe_experts(xs, block_exp, n_used, w_gate_e, w_up_e, w_down_e, l, MOE_BM)
        xf, xb = _combine_ln(xf, yb, pair_slot, gate, ln2_g.reshape(depth, 1, d), ln2_b.reshape(depth, 1, d),
                             alpha, l)

    y_prompt = xf[:tp].reshape(bp, lp, d)
    y_sample = xf[tp:].reshape(bs, ls, d)
    st = lambda xs_, ref: jnp.stack(xs_).astype(ref.dtype)
    return (y_prompt, y_sample,
            st(new_delta_p, state_delta), st(new_conv_p, state_conv), st(new_s5_p, state_s5),
            st(new_ret_p, state_ret),
            st(new_delta_s, state_delta), st(new_conv_s, state_conv), st(new_s5_s, state_s5),
            st(new_ret_s, state_ret))
```

```python
import functools
import math

import jax
import jax.numpy as jnp
from jax import lax
from jax.experimental import pallas as pl
from jax.experimental.pallas import tpu as pltpu

F32 = jnp.float32
BF16 = jnp.bfloat16
HI = lax.Precision.HIGHEST

LANES = 128
SUBLANES = 8
VMEM_LIMIT = 60 * 1024 * 1024

LN_EPS = 1e-5
RMS_EPS = 1e-6
ROPE_BASE = 10000.0
PAST_LEN = 16384
CHUNK = 64
N_EXPERT_GROUPS = 4
TOP_K = 2
S5_GROUP_BLOCK = 8

ROW_TILE = 256
MM_TM = 1024
MM_TN = 256
MIXER_TN = 512
MOE_BM = 512
MOE_TN_UP = 512
MOE_TN_DOWN = 1024
GATHER_ROWS = 128
SAMPLE_GROUP = 8


def _cparams(sem):
    return pltpu.CompilerParams(dimension_semantics=sem, vmem_limit_bytes=VMEM_LIMIT)


def _bdot(a, b):
    return jnp.dot(a.astype(BF16), b.astype(BF16), preferred_element_type=F32)


def _bdot_nt(a, b):
    return lax.dot_general(a.astype(BF16), b.astype(BF16), (((1,), (1,)), ((), ())),
                           preferred_element_type=F32)


def _bdot_tn(a, b):
    return lax.dot_general(a.astype(BF16), b.astype(BF16), (((0,), (0,)), ((), ())),
                           preferred_element_type=F32)


def _hdot(a, b):
    return jnp.dot(a, b, precision=HI, preferred_element_type=F32)


def _ln_math(x, g, b):
    mu = jnp.mean(x, -1, keepdims=True)
    xc = x - mu
    var = jnp.mean(xc * xc, -1, keepdims=True)
    return xc * lax.rsqrt(var + LN_EPS) * g + b


def _ln_in_kernel(xp_ref, xs_ref, g_ref, b_ref, of_ref, ob_ref, *, n_prompt_tiles):
    i = pl.program_id(0)

    def run(src):
        y = _ln_math(src[...], g_ref[...], b_ref[...])
        of_ref[...] = y
        ob_ref[...] = y.astype(BF16)

    @pl.when(i < n_prompt_tiles)
    def _():
        run(xp_ref)

    @pl.when(i >= n_prompt_tiles)
    def _():
        run(xs_ref)


def _ln_in(xp, xs, g, b):
    tp, d = xp.shape
    ts = xs.shape[0]
    tm = ROW_TILE
    npt, nst = tp // tm, ts // tm
    t = tp + ts
    return pl.pallas_call(
        functools.partial(_ln_in_kernel, n_prompt_tiles=npt),
        grid=(npt + nst,),
        in_specs=[pl.BlockSpec((tm, d), lambda i: (jnp.minimum(i, npt - 1), 0)),
                  pl.BlockSpec((tm, d), lambda i: (jnp.maximum(i - npt, 0), 0)),
                  pl.BlockSpec((1, d), lambda i: (0, 0)),
                  pl.BlockSpec((1, d), lambda i: (0, 0))],
        out_specs=[pl.BlockSpec((tm, d), lambda i: (i, 0)),
                   pl.BlockSpec((tm, d), lambda i: (i, 0))],
        out_shape=[jax.ShapeDtypeStruct((t, d), F32), jax.ShapeDtypeStruct((t, d), BF16)],
        compiler_params=_cparams(("arbitrary",)),
        name="ln_in",
    )(xp, xs, g.reshape(1, d), b.reshape(1, d))


HI16 = 0xFFFF0000


def _pack_bf16_pairs(z):
    half = z.shape[1] // 2
    bits = lambda v: lax.bitcast_convert_type(v.astype(BF16).astype(F32), jnp.uint32)
    return (bits(z[:, :half]) >> 16) | (bits(z[:, half:]) & jnp.uint32(HI16))


def _unpack_bf16_pairs(p):
    lo = lax.bitcast_convert_type(p << 16, F32).astype(BF16)
    hi = lax.bitcast_convert_type(p & jnp.uint32(HI16), F32).astype(BF16)
    return lo, hi


def _ln1_router_kernel(x_ref, y_ref, g_ref, b_ref, rw_ref, rb_ref, of_ref, pk_ref, pr_ref, *, alpha):
    z = _ln_math(alpha * x_ref[...] + y_ref[...], g_ref[...], b_ref[...])
    of_ref[...] = z
    pk_ref[...] = _pack_bf16_pairs(z)
    logits = _hdot(z, rw_ref[...]) + rb_ref[...]
    m = jnp.max(logits, -1, keepdims=True)
    e = jnp.exp(logits - m)
    pr_ref[...] = e / jnp.sum(e, -1, keepdims=True)


def _ln1_router(x, y, g, b, rw, rb, alpha, layer):
    t, d = x.shape
    tm = ROW_TILE
    ep = rw.shape[1]
    return pl.pallas_call(
        functools.partial(_ln1_router_kernel, alpha=alpha),
        grid=(t // tm,),
        in_specs=[pl.BlockSpec((tm, d), lambda i: (i, 0)),
                  pl.BlockSpec((tm, d), lambda i: (i, 0)),
                  pl.BlockSpec((None, 1, d), lambda i: (layer, 0, 0)),
                  pl.BlockSpec((None, 1, d), lambda i: (layer, 0, 0)),
                  pl.BlockSpec((d, ep), lambda i: (0, 0)),
                  pl.BlockSpec((1, ep), lambda i: (0, 0))],
        out_specs=[pl.BlockSpec((tm, d), lambda i: (i, 0)),
                   pl.BlockSpec((tm, d // 2), lambda i: (i, 0)),
                   pl.BlockSpec((tm, ep), lambda i: (i, 0))],
        out_shape=[jax.ShapeDtypeStruct((t, d), F32), jax.ShapeDtypeStruct((t, d // 2), jnp.uint32),
                   jax.ShapeDtypeStruct((t, ep), F32)],
        compiler_params=_cparams(("arbitrary",)),
        name="ln1_router",
    )(x, y, g, b, rw, rb)


def _mm_kernel(a_ref, w_ref, o_ref):
    o_ref[...] = jnp.dot(a_ref[...], w_ref[...].astype(BF16),
                         preferred_element_type=F32).astype(o_ref.dtype)


def _pick(n, pref):
    for c in (pref, 512, 256, 128):
        if c <= pref and n % c == 0:
            return c
    return n


W_RING = 3


def _mm_nt_kernel(a_ref, wt_hbm, o_ref, wbuf, sem, *, layer, row0, tn, nj, nsteps):
    s = pl.program_id(0) * nj + pl.program_id(1)

    def tile_copy(step, slot):
        r = pl.multiple_of(row0 + (step % nj) * tn, SUBLANES)
        return pltpu.make_async_copy(wt_hbm.at[layer, pl.ds(r, tn), :], wbuf.at[slot], sem.at[slot])

    @pl.when(s == 0)
    def _():
        for first in range(min(W_RING - 1, nsteps)):
            tile_copy(first, first).start()

    @pl.when(s + W_RING - 1 < nsteps)
    def _():
        tile_copy(s + W_RING - 1, (s + W_RING - 1) % W_RING).start()

    slot = s % W_RING
    tile_copy(s, slot).wait()
    o_ref[...] = lax.dot_general(a_ref[...], wbuf[slot].astype(BF16), (((1,), (1,)), ((), ())),
                                 preferred_element_type=F32).astype(o_ref.dtype)


def _matmul_nt(a, wt, *, layer, row0, n, tile_major=False, tn_pref=MM_TN, name="mm_nt"):
    m, k = a.shape
    tm = _pick(m, MM_TM)
    tn = _pick(n, tn_pref)
    assert row0 % SUBLANES == 0
    ni, nj = m // tm, n // tn
    w_spec = pl.BlockSpec(memory_space=pl.ANY)
    if tile_major:
        out_spec = pl.BlockSpec((None, tm, tn), lambda i, j: (j, i, 0))
        out_shape = jax.ShapeDtypeStruct((n // tn, m, tn), F32)
    else:
        out_spec = pl.BlockSpec((tm, tn), lambda i, j: (i, j))
        out_shape = jax.ShapeDtypeStruct((m, n), F32)
    return pl.pallas_call(
        functools.partial(_mm_nt_kernel, layer=layer, row0=row0, tn=tn, nj=nj, nsteps=ni * nj),
        grid=(ni, nj),
        in_specs=[pl.BlockSpec((tm, k), lambda i, j: (i, 0)), w_spec],
        out_specs=out_spec,
        out_shape=out_shape,
        scratch_shapes=[pltpu.VMEM((W_RING, tn, k), F32), pltpu.SemaphoreType.DMA((W_RING,))],
        compiler_params=_cparams(("arbitrary", "arbitrary")),
        name=name,
    )(a, wt)


def _matmul(a, w, *, layer=None, col0=0, n=None, tile_major=False, out_dtype=F32, tn_pref=MM_TN, name="mm"):
    m, k = a.shape
    n = w.shape[-1] if n is None else n
    tm = _pick(m, MM_TM)
    tn = _pick(n, tn_pref)
    assert col0 % tn == 0
    j0 = col0 // tn
    if w.ndim == 3:
        w_spec = pl.BlockSpec((None, k, tn), lambda i, j: (layer, 0, j0 + j))
    else:
        w_spec = pl.BlockSpec((k, tn), lambda i, j: (0, j0 + j))
    if tile_major:
        out_spec = pl.BlockSpec((None, tm, tn), lambda i, j: (j, i, 0))
        out_shape = jax.ShapeDtypeStruct((n // tn, m, tn), out_dtype)
    else:
        out_spec = pl.BlockSpec((tm, tn), lambda i, j: (i, j))
        out_shape = jax.ShapeDtypeStruct((m, n), out_dtype)
    return pl.pallas_call(
        _mm_kernel,
        grid=(m // tm, n // tn),
        in_specs=[pl.BlockSpec((tm, k), lambda i, j: (i, 0)), w_spec],
        out_specs=out_spec,
        out_shape=out_shape,
        compiler_params=_cparams(("arbitrary", "arbitrary")),
        name=name,
    )(a, w)


def _upmerge_kernel(odn_p, os5_p, oret_p, odn_s, os5_s, oret_s, gdn, gs5, gret, wdn, ws5, wret, o_ref,
                    *, n_prompt_tiles):
    i = pl.program_id(0)

    def branch(o, w, g):
        return jax.nn.sigmoid(g[...]) * jnp.dot(o[...], w[...].astype(BF16), preferred_element_type=F32)

    def run(odn, os5, oret):
        o_ref[...] = (branch(odn, wdn, gdn) + branch(os5, ws5, gs5) + branch(oret, wret, gret)).astype(o_ref.dtype)

    @pl.when(i < n_prompt_tiles)
    def _():
        run(odn_p, os5_p, oret_p)

    @pl.when(i >= n_prompt_tiles)
    def _():
        run(odn_s, os5_s, oret_s)


def _upmerge(outs_p, outs_s, gates, w_dn, w_s5, w_ret, layer):
    tp, ts = outs_p[0].shape[0], outs_s[0].shape[0]
    t = tp + ts
    d = w_dn.shape[-1]
    tm = _pick(math.gcd(tp, ts), MM_TM)
    tn = _pick(d, MM_TN)
    nj = d // tn
    npt = tp // tm
    nst = ts // tm

    def p_spec(a):
        return pl.BlockSpec((tm, a.shape[1]), lambda i, j: (jnp.minimum(i, npt - 1), 0))

    def s_spec(a):
        return pl.BlockSpec((tm, a.shape[1]), lambda i, j: (jnp.maximum(i - npt, 0), 0))

    def g_spec(off):
        return pl.BlockSpec((tm, tn), lambda i, j: (i, off * nj + j))

    def w_spec(w):
        return pl.BlockSpec((None, w.shape[1], tn), lambda i, j: (layer, 0, j))

    return pl.pallas_call(
        functools.partial(_upmerge_kernel, n_prompt_tiles=npt),
        grid=(npt + nst, nj),
        in_specs=[p_spec(a) for a in outs_p] + [s_spec(a) for a in outs_s]
        + [g_spec(0), g_spec(1), g_spec(2), w_spec(w_dn), w_spec(w_s5), w_spec(w_ret)],
        out_specs=pl.BlockSpec((tm, tn), lambda i, j: (i, j)),
        out_shape=jax.ShapeDtypeStruct((t, d), BF16),
        compiler_params=_cparams(("arbitrary", "arbitrary")),
        name="upmerge",
    )(*outs_p, *outs_s, gates, gates, gates, w_dn, w_s5, w_ret)


def _chunk_masks(r, c):
    ti = lax.broadcasted_iota(jnp.int32, (r, r), 0)
    si = lax.broadcasted_iota(jnp.int32, (r, r), 1)
    same = (ti // c) == (si // c)
    return same, same & (si <= ti), same & (si < ti)


def _shift_rows_carry(x, prev8, s):
    rolled = pltpu.roll(x, s, 0)
    prev_rolled = pltpu.roll(prev8, s, 0)
    row8 = lax.broadcasted_iota(jnp.int32, prev8.shape, 0)
    first = jnp.where(row8 < s, prev_rolled, rolled[0:SUBLANES])
    return jnp.concatenate([first, rolled[SUBLANES:]], axis=0)


def _shift_rows_seq8(x, bufx, s, hist):
    r = x.shape[0]
    rolled = pltpu.roll(x, s, 0)
    brolled = pltpu.roll(bufx, (r - (hist - s)) % r, 0)
    t8 = lax.broadcasted_iota(jnp.int32, x.shape, 0) % SUBLANES
    return jnp.where(t8 < s, brolled, rolled)


def _causal_conv(x, w_ref, shift):
    n = w_ref.shape[0]
    y = w_ref[n - 1:n, :] * x
    for s in range(1, n):
        y = y + w_ref[n - 1 - s:n - s, :] * shift(s)
    return y


def _silu(x):
    return x * jax.nn.sigmoid(x)


def _l2norm(x):
    return x * lax.rsqrt(jnp.sum(x * x, -1, keepdims=True) + RMS_EPS)


def _rms(x):
    return x * lax.rsqrt(jnp.mean(x * x, -1, keepdims=True) + RMS_EPS)


def _split3(x):
    h = x.astype(BF16)
    r1 = x - h.astype(F32)
    m = r1.astype(BF16)
    l = (r1 - m.astype(F32)).astype(BF16)
    return h, m, l


def _delta_gate_kernel(ba_ref, bat_ref, hpr_ref, hpc_ref, beta_ref, gcum_ref, gtot_ref, rows_ref, *, nh, c):
    r = ba_ref.shape[0]
    same, incl, _ = _chunk_masks(r, c)
    ti = lax.broadcasted_iota(jnp.int32, (r, r), 0)
    si = lax.broadcasted_iota(jnp.int32, (r, r), 1)
    incl_t = (same & (ti <= si)).astype(BF16)
    inclb = incl.astype(BF16)
    sameb = same.astype(BF16)
    dot = functools.partial(jnp.dot, preferred_element_type=F32)
    ba = ba_ref[...]
    lane = lax.broadcasted_iota(jnp.int32, ba.shape, 1)
    beta_ref[...] = jax.nn.sigmoid(ba)
    g_cols = -jnp.exp(hpr_ref[0:1, :]) * jax.nn.softplus(ba + hpr_ref[1:2, :])
    g_cols = jnp.where((lane >= nh) & (lane < 2 * nh), g_cols, 0.0)
    h, m, l = _split3(g_cols)
    gcum_ref[...] = dot(inclb, h) + dot(inclb, m) + dot(inclb, l)
    gtot_ref[...] = dot(sameb, h) + dot(sameb, m) + dot(sameb, l)
    g_rows = -jnp.exp(hpc_ref[:, 0:1]) * jax.nn.softplus(bat_ref[...] + hpc_ref[:, 1:2])
    h, m, l = _split3(g_rows)
    rows_ref[...] = dot(h, incl_t) + dot(m, incl_t) + dot(l, incl_t)


def _delta_gate(ba, bat, hpr, hpc, *, nh, c, row0, nrows):
    r = min(ROW_TILE, nrows)
    t0 = row0 // r
    nh2 = 2 * nh
    col = pl.BlockSpec((r, LANES), lambda i: (i, 0))
    return pl.pallas_call(
        functools.partial(_delta_gate_kernel, nh=nh, c=c),
        grid=(nrows // r,),
        in_specs=[pl.BlockSpec((r, LANES), lambda i: (t0 + i, 0)),
                  pl.BlockSpec((nh2, r), lambda i: (0, t0 + i)),
                  pl.BlockSpec((2, LANES), lambda i: (0, 0)),
                  pl.BlockSpec((nh2, 2), lambda i: (0, 0))],
        out_specs=[col, col, col, pl.BlockSpec((nh2, r), lambda i: (0, i))],
        out_shape=[jax.ShapeDtypeStruct((nrows, LANES), F32)] * 3 + [jax.ShapeDtypeStruct((nh2, nrows), F32)],
        compiler_params=_cparams(("arbitrary",)),
        name="delta_gate",
    )(ba, bat, hpr, hpc)


def _pick_col(cols, idx):
    lane = lax.broadcasted_iota(jnp.int32, cols.shape, 1)
    return jnp.sum(jnp.where(lane == idx, cols, 0.0), -1, keepdims=True)


def _delta_prep(heads, c):
    r, dk = heads[0][1].shape
    _, incl, strict = _chunk_masks(r, c)
    ms, as_ = [], []
    for q, k, v, beta_c, gc_c, gt_c, gc_r in heads:
        dec = jnp.exp(jnp.minimum(gc_c - gc_r, 0.0))
        ms.append(jnp.where(strict, beta_c * dec * _bdot_nt(k, k), 0.0))
        as_.append(jnp.where(incl, dec * _bdot_nt(q, k), 0.0))
    eye = (lax.broadcasted_iota(jnp.int32, (r, r), 0) == lax.broadcasted_iota(jnp.int32, (r, r), 1)).astype(F32)
    xs = [eye - m for m in ms]
    pws = ms
    for _ in range(int(math.log2(c)) - 1):
        pws = [_bdot(pw, pw) for pw in pws]
        xs = [x + _bdot(x, pw) for x, pw in zip(xs, pws)]
    out = []
    for (q, k, v, beta_c, gc_c, gt_c, gc_r), a, x in zip(heads, as_, xs):
        gin = jnp.exp(gc_c)
        wu = _bdot(x, jnp.concatenate([beta_c * gin * k, beta_c * v], axis=1))
        out.append((a, wu[:, :dk], wu[:, dk:], k * jnp.exp(gt_c - gc_c), jnp.exp(gt_c), gin))
    return out


def _delta_head_inputs(qa, ka, va, beta_ref, gcum_ref, gtot_ref, rows_ref, i, h, nh, dk):
    cols = slice(i * dk, (i + 1) * dk)
    q = _l2norm(qa[:, cols]) * (dk ** -0.5)
    k = _l2norm(ka[:, cols])
    beta_c = _pick_col(beta_ref[...], h)
    gc_c = _pick_col(gcum_ref[...], nh + h)
    gt_c = _pick_col(gtot_ref[...], nh + h)
    gc_r = rows_ref[pl.ds(nh + h, 1), :]
    return q, k, va[:, cols], beta_c, gc_c, gt_c, gc_r


def _delta_finish(a, u, qs, z, nw_ref):
    o = qs + _bdot(a, u)
    return _rms(o) * nw_ref[...] * _silu(z)


def _delta_prompt_kernel(q_ref, k_ref, v_ref, z_ref, beta_ref, gcum_ref, gtot_ref, rows_ref,
                         cwq_ref, cwk_ref, cwv_ref, nw_ref,
                         o_ref, st_ref,
                         s_sc, pq_sc, pk_sc, pv_sc,
                         *, nh, hb, c, dk):
    p = pl.program_id(1)
    j = pl.program_id(2)
    r = q_ref.shape[0]

    @pl.when(j == 0)
    def _():
        s_sc[...] = jnp.zeros_like(s_sc)
        pq_sc[...] = jnp.zeros_like(pq_sc)
        pk_sc[...] = jnp.zeros_like(pk_sc)
        pv_sc[...] = jnp.zeros_like(pv_sc)

    def conv(x_ref, w_ref, prev_sc):
        x = x_ref[...]
        prev8 = prev_sc[...]
        y = _causal_conv(x, w_ref, lambda s: _shift_rows_carry(x, prev8, s))
        prev_sc[...] = x[r - SUBLANES:, :]
        return _silu(y)

    qa = conv(q_ref, cwq_ref, pq_sc)
    ka = conv(k_ref, cwk_ref, pk_sc)
    va = conv(v_ref, cwv_ref, pv_sc)
    z = z_ref[...]

    heads = [_delta_head_inputs(qa, ka, va, beta_ref, gcum_ref, gtot_ref, rows_ref, i, p * hb + i, nh, dk)
             for i in range(hb)]
    preps = _delta_prep(heads, c)
    nchunk = r // c
    chunk_rows = [slice(n * c, (n + 1) * c) for n in range(nchunk)]
    kws = [[_bdot_tn(pr[3][rows], pr[1][rows]) for rows in chunk_rows] for pr in preps]
    bcs = [[_bdot_tn(pr[3][rows], pr[2][rows]) for rows in chunk_rows] for pr in preps]
    ss = [s_sc[i] for i in range(hb)]
    us = [[] for _ in range(hb)]
    qss = [[] for _ in range(hb)]
    for n, rows in enumerate(chunk_rows):
        for i in range(hb):
            q = heads[i][0]
            _, w, u0, kd, eg, gin = preps[i]
            s = ss[i]
            us[i].append(u0[rows] - _bdot(w[rows], s))
            qss[i].append(gin[rows] * _bdot(q[rows], s))
            ss[i] = (eg[n * c:n * c + 1] * s + bcs[i][n]) - _bdot(kws[i][n], s)
    for i in range(hb):
        s_sc[i] = ss[i]
        st_ref[0, i] = ss[i]
        cols = slice(i * dk, (i + 1) * dk)
        o = _delta_finish(preps[i][0], jnp.concatenate(us[i], axis=0), jnp.concatenate(qss[i], axis=0),
                          z[:, cols], nw_ref)
        o_ref[:, cols] = o.astype(o_ref.dtype)


def _delta_sample_kernel(q_ref, k_ref, v_ref, z_ref, beta_ref, gcum_ref, gtot_ref, rows_ref,
                         cwq_ref, cwk_ref, cwv_ref, nw_ref, bq_ref, bk_ref, bv_ref, sin_ref,
                         o_ref, st_ref,
                         q_s, w_s, u0_s, kd_s, eg_s, gin_s, u_s, qs_s,
                         *, nh, hb, c, dk, hist):
    p = pl.program_id(1)
    r = q_ref.shape[0]

    def conv(x_ref, w_ref, b_ref):
        x = x_ref[...]
        bufx = b_ref[...]
        return _silu(_causal_conv(x, w_ref, lambda s: _shift_rows_seq8(x, bufx, s, hist)))

    qa = conv(q_ref, cwq_ref, bq_ref)
    ka = conv(k_ref, cwk_ref, bk_ref)
    va = conv(v_ref, cwv_ref, bv_ref)
    z = z_ref[...]

    heads = [_delta_head_inputs(qa, ka, va, beta_ref, gcum_ref, gtot_ref, rows_ref, i, p * hb + i, nh, dk)
             for i in range(hb)]
    for i in range(hb):
        q = heads[i][0]
        a, w, u0, kd, eg, gin = _delta_prep(heads[i:i + 1], c)[0]
        q_s[...] = q
        w_s[...] = w
        u0_s[...] = u0
        kd_s[...] = kd
        eg_s[...] = jnp.broadcast_to(eg, eg_s.shape)
        gin_s[...] = jnp.broadcast_to(gin, gin_s.shape)

        def group(gi, carry, i=i):
            seqs = [gi * SAMPLE_GROUP + j for j in range(SAMPLE_GROUP)]
            rows = [pl.ds(pl.multiple_of(n * c, c), c) for n in seqs]
            ss = [sin_ref[n, i] for n in seqs]
            wqs = [_bdot(jnp.concatenate([w_s[rw, :], q_s[rw, :]], axis=0), s) for rw, s in zip(rows, ss)]
            us = [u0_s[rw, :] - wq[:c] for rw, wq in zip(rows, wqs)]
            for rw, u, wq in zip(rows, us, wqs):
                u_s[rw, :] = u
                qs_s[rw, :] = gin_s[rw, :] * wq[c:]
            for n, rw, s, u in zip(seqs, rows, ss, us):
                st_ref[n, i] = eg_s[pl.ds(pl.multiple_of(n * c, c), 1), :] * s + _bdot_tn(kd_s[rw, :], u)
            return carry

        lax.fori_loop(0, r // c // SAMPLE_GROUP, group, 0)
        cols = slice(i * dk, (i + 1) * dk)
        o_ref[:, cols] = _delta_finish(a, u_s[...], qs_s[...], z[:, cols], nw_ref).astype(o_ref.dtype)


def _delta_scratch(r, dk, dv):
    return [pltpu.VMEM((r, dk), F32),
            pltpu.VMEM((r, dk), F32),
            pltpu.VMEM((r, dv), F32),
            pltpu.VMEM((r, dk), F32),
            pltpu.VMEM((r, dv), F32),
            pltpu.VMEM((r, dv), F32),
            pltpu.VMEM((r, dv), F32),
            pltpu.VMEM((r, dv), F32)]


def _delta_common_specs(tiles_per_seg, row_map, local_map, taps, dv, layer, tn, r, nh2):
    def seg(n):
        return pl.BlockSpec((None, r, tn), lambda *g: (n * tiles_per_seg + g[1], row_map(*g), 0))

    def cw(n):
        return pl.BlockSpec((None, taps, tn), lambda *g: (layer, 0, n * tiles_per_seg + g[1]))

    col = pl.BlockSpec((r, LANES), lambda *g: (local_map(*g), 0))
    return [seg(0), seg(1), seg(2), seg(3), col, col, col,
            pl.BlockSpec((nh2, r), lambda *g: (0, local_map(*g))),
            cw(0), cw(1), cw(2),
            pl.BlockSpec((None, 1, dv), lambda *g: (layer, 0, 0))]


def _delta_prompt(p_dn, gates, conv_w, norm_w, layer, *, nb, seq, nh, dk, dv, row0):
    tn = p_dn.shape[-1]
    hb = tn // dk
    r = min(ROW_TILE, seq)
    c = math.gcd(seq, CHUNK)
    npair = nh // hb
    nt = seq // r
    t0 = row0 // r
    kern = functools.partial(_delta_prompt_kernel, nh=nh, hb=hb, c=c, dk=dk)
    specs = _delta_common_specs(npair, lambda b, p, j: t0 + b * nt + j, lambda b, p, j: b * nt + j,
                                conv_w.shape[1], dv, layer, tn, r, 2 * nh)
    return pl.pallas_call(
        kern,
        grid=(nb, npair, nt),
        in_specs=specs,
        out_specs=[pl.BlockSpec((r, tn), lambda b, p, j: (b * nt + j, p)),
                   pl.BlockSpec((1, hb, dk, dv), lambda b, p, j: (b, p, 0, 0))],
        out_shape=[jax.ShapeDtypeStruct((nb * seq, nh * dv), BF16),
                   jax.ShapeDtypeStruct((nb, nh, dk, dv), F32)],
        scratch_shapes=[pltpu.VMEM((hb, dk, dv), F32),
                        pltpu.VMEM((SUBLANES, tn), F32), pltpu.VMEM((SUBLANES, tn), F32),
                        pltpu.VMEM((SUBLANES, tn), F32)],
        compiler_params=_cparams(("arbitrary", "arbitrary", "arbitrary")),
        name="delta_prompt",
    )(p_dn, p_dn, p_dn, p_dn, *gates, conv_w, conv_w, conv_w, norm_w)


def _delta_sample(p_dn, gates, conv_w, norm_w, bufx, state, layer, *, nb, seq, nh, dk, dv, row0, hist):
    tn = p_dn.shape[-1]
    hb = tn // dk
    c = seq
    r = min(ROW_TILE, nb * seq)
    npair = nh // hb
    nt = nb * seq // r
    spt = r // seq
    t0 = row0 // r
    kern = functools.partial(_delta_sample_kernel, nh=nh, hb=hb, c=c, dk=dk, hist=hist)
    specs = _delta_common_specs(npair, lambda i, p: t0 + i, lambda i, p: i,
                                conv_w.shape[1], dv, layer, tn, r, 2 * nh)

    def buf(n):
        return pl.BlockSpec((r, tn), lambda i, p: (i, n * npair + p))

    specs += [buf(0), buf(1), buf(2),
              pl.BlockSpec((None, spt, hb, dk, dv), lambda i, p: (layer, i, p, 0, 0))]
    return pl.pallas_call(
        kern,
        grid=(nt, npair),
        in_specs=specs,
        out_specs=[pl.BlockSpec((r, tn), lambda i, p: (i, p)),
                   pl.BlockSpec((spt, hb, dk, dv), lambda i, p: (i, p, 0, 0))],
        out_shape=[jax.ShapeDtypeStruct((nb * seq, nh * dv), BF16),
                   jax.ShapeDtypeStruct((nb, nh, dk, dv), F32)],
        scratch_shapes=_delta_scratch(r, dk, dv),
        compiler_params=_cparams(("arbitrary", "arbitrary")),
        name="delta_sample",
    )(p_dn, p_dn, p_dn, p_dn, *gates, conv_w, conv_w, conv_w, norm_w, bufx, bufx, bufx, state)


def _rotary(x, cs_ref, sn_ref):
    half = x.shape[1] // 2
    return x * cs_ref[...] + pltpu.roll(x, half, 1) * sn_ref[...]


def _ret_prompt_kernel(q_ref, k_ref, v_ref, g_ref, cs_ref, sn_ref, dec_ref, qd_ref, kd_ref, cd_ref,
                       o_ref, st_ref, s_sc, *, hb, c, dk):
    j = pl.program_id(2)

    @pl.when(j == 0)
    def _():
        s_sc[...] = jnp.zeros_like(s_sc)

    r = q_ref.shape[0]
    qs, kds, vs, inners = [], [], [], []
    for i in range(hb):
        cols = slice(i * dk, (i + 1) * dk)
        q = _rotary(q_ref[:, cols], cs_ref, sn_ref)
        k = _rotary(k_ref[:, cols], cs_ref, sn_ref) * (dk ** -0.5)
        v = v_ref[:, cols]
        inners.append(_bdot(_bdot_nt(q, k) * dec_ref[i], v))
        qs.append(q)
        kds.append(k * kd_ref[i])
        vs.append(v)
    ss = [s_sc[i] for i in range(hb)]
    qrs = [[] for _ in range(hb)]
    for n in range(r // c):
        rows = slice(n * c, (n + 1) * c)
        for i in range(hb):
            qrs[i].append(_bdot(qs[i][rows], ss[i]))
            ss[i] = cd_ref[i, 0:1, :] * ss[i] + _bdot_tn(kds[i][rows], vs[i][rows])
    for i in range(hb):
        cols = slice(i * dk, (i + 1) * dk)
        s_sc[i] = ss[i]
        st_ref[0, i] = ss[i]
        o = inners[i] + qd_ref[i] * jnp.concatenate(qrs[i], axis=0)
        o_ref[:, cols] = (_rms(o) * _silu(g_ref[:, cols])).astype(o_ref.dtype)


def _ret_sample_kernel(q_ref, k_ref, v_ref, g_ref, cs_ref, sn_ref, dec_ref, qd_ref, kd_ref, cd_ref, sin_ref,
                       o_ref, st_ref, q_s, kd_s, v_s, qr_s, *, hb, c, dk):
    r = q_ref.shape[0]
    for i in range(hb):
        cols = slice(i * dk, (i + 1) * dk)
        q = _rotary(q_ref[:, cols], cs_ref, sn_ref)
        k = _rotary(k_ref[:, cols], cs_ref, sn_ref) * (dk ** -0.5)
        v = v_ref[:, cols]
        inner = _bdot(_bdot_nt(q, k) * dec_ref[i], v)
        q_s[...] = q
        kd_s[...] = k * kd_ref[i]
        v_s[...] = v
        cd = cd_ref[i, 0:1, :]

        def group(gi, carry, i=i, cd=cd):
            seqs = [gi * SAMPLE_GROUP + j for j in range(SAMPLE_GROUP)]
            rows = [pl.ds(pl.multiple_of(n * c, c), c) for n in seqs]
            ss = [sin_ref[n, i] for n in seqs]
            for rw, s in zip(rows, ss):
                qr_s[rw, :] = _bdot(q_s[rw, :], s)
            for n, rw, s in zip(seqs, rows, ss):
                st_ref[n, i] = cd * s + _bdot_tn(kd_s[rw, :], v_s[rw, :])
            return carry

        lax.fori_loop(0, r // c // SAMPLE_GROUP, group, 0)
        o = inner + qd_ref[i] * qr_s[...]
        o_ref[:, cols] = (_rms(o) * _silu(g_ref[:, cols])).astype(o_ref.dtype)


def _ret_tables(nh, r, c, dk):
    log_gamma = jnp.log1p(-jnp.exp2(-5.0 - jnp.arange(nh, dtype=F32)))
    ti = jnp.arange(r)
    same = (ti[:, None] // c) == (ti[None, :] // c)
    rel = (ti[:, None] - ti[None, :]).astype(F32)
    dec = jnp.where(same & (rel >= 0), jnp.exp(jnp.maximum(rel, 0.0) * log_gamma[:, None, None]), 0.0)
    pos = (ti % c).astype(F32)
    qd = jnp.exp((pos + 1.0) * log_gamma[:, None])
    kd = jnp.exp((c - 1.0 - pos) * log_gamma[:, None])
    cd = jnp.exp(c * log_gamma)
    bc = lambda a: jnp.broadcast_to(a[..., None], a.shape + (dk,))
    return dec, bc(qd), bc(kd), jnp.broadcast_to(cd[:, None, None], (nh, SUBLANES, dk))


def _rope_tables(pos, dk):
    half = dk // 2
    inv = ROPE_BASE ** (-jnp.arange(half, dtype=F32) / half)
    ang = pos.astype(F32)[:, None] * inv[None, :]
    cos, sin = jnp.cos(ang), jnp.sin(ang)
    return jnp.concatenate([cos, cos], -1), jnp.concatenate([-sin, sin], -1)


def _ret_specs(npair, row_map, pos_map, tn, r, hb, dk):
    def seg(n):
        return pl.BlockSpec((None, r, tn), lambda *g: (n * npair + g[1], row_map(*g), 0))

    def tab(shape):
        return pl.BlockSpec((hb,) + shape, lambda *g: (g[1],) + (0,) * len(shape))

    return [seg(0), seg(1), seg(2), seg(3),
            pl.BlockSpec((r, dk), lambda *g: (pos_map(*g), 0)),
            pl.BlockSpec((r, dk), lambda *g: (pos_map(*g), 0)),
            tab((r, r)), tab((r, dk)), tab((r, dk)), tab((SUBLANES, dk))]


def _ret_scratch(r, dk, dv):
    return [pltpu.VMEM((r, dk), F32), pltpu.VMEM((r, dk), F32), pltpu.VMEM((r, dv), F32),
            pltpu.VMEM((r, dv), F32)]


def _ret_prompt(p_ret, cs, sn, *, nb, seq, nh, dk, dv, row0):
    tn = p_ret.shape[-1]
    hb = tn // dk
    r = min(ROW_TILE, seq)
    c = math.gcd(seq, CHUNK)
    npair = nh // hb
    nt = seq // r
    t0 = row0 // r
    dec, qd, kd, cd = _ret_tables(nh, r, c, dk)
    specs = _ret_specs(npair, lambda b, p, j: t0 + b * nt + j, lambda b, p, j: j, tn, r, hb, dk)
    return pl.pallas_call(
        functools.partial(_ret_prompt_kernel, hb=hb, c=c, dk=dk),
        grid=(nb, npair, nt),
        in_specs=specs,
        out_specs=[pl.BlockSpec((r, tn), lambda b, p, j: (b * nt + j, p)),
                   pl.BlockSpec((1, hb, dk, dv), lambda b, p, j: (b, p, 0, 0))],
        out_shape=[jax.ShapeDtypeStruct((nb * seq, nh * dv), BF16),
                   jax.ShapeDtypeStruct((nb, nh, dk, dv), F32)],
        scratch_shapes=[pltpu.VMEM((hb, dk, dv), F32)],
        compiler_params=_cparams(("arbitrary", "arbitrary", "arbitrary")),
        name="ret_prompt",
    )(p_ret, p_ret, p_ret, p_ret, cs, sn, dec, qd, kd, cd)


def _ret_sample(p_ret, cs, sn, state, layer, *, nb, seq, nh, dk, dv, row0):
    tn = p_ret.shape[-1]
    hb = tn // dk
    c = seq
    r = min(ROW_TILE, nb * seq)
    npair = nh // hb
    nt = nb * seq // r
    spt = r // seq
    t0 = row0 // r
    dec, qd, kd, cd = _ret_tables(nh, r, c, dk)
    specs = _ret_specs(npair, lambda i, p: t0 + i, lambda i, p: 0, tn, r, hb, dk)
    specs += [pl.BlockSpec((None, spt, hb, dk, dv), lambda i, p: (layer, i, p, 0, 0))]
    return pl.pallas_call(
        functools.partial(_ret_sample_kernel, hb=hb, c=c, dk=dk),
        grid=(nt, npair),
        in_specs=specs,
        out_specs=[pl.BlockSpec((r, tn), lambda i, p: (i, p)),
                   pl.BlockSpec((spt, hb, dk, dv), lambda i, p: (i, p, 0, 0))],
        out_shape=[jax.ShapeDtypeStruct((nb * seq, nh * dv), BF16),
                   jax.ShapeDtypeStruct((nb, nh, dk, dv), F32)],
        scratch_shapes=_ret_scratch(r, dk, dv),
        compiler_params=_cparams(("arbitrary", "arbitrary")),
        name="ret_sample",
    )(p_ret, p_ret, p_ret, p_ret, cs, sn, dec, qd, kd, cd, state)


S5_COL_CHUNK = 1024


def _s5_project_in(u_ref, bblk_ref, bur_s, bui_s):
    nblk = bblk_ref.shape[0]
    half = bblk_ref.shape[2] // 2
    for i in range(nblk):
        res = _bdot(u_ref[:, i * LANES:(i + 1) * LANES], bblk_ref[i])
        bur_s[:, i * half:(i + 1) * half] = res[:, :half]
        bui_s[:, i * half:(i + 1) * half] = res[:, half:]


def _s5_scan_rows(bur_s, bui_s, lam_ref, cols, row_start, nrows, hr, hi):
    lr = lam_ref[0:1, cols]
    li = lam_ref[1:2, cols]

    def step(t, carry):
        hr, hi = carry
        row = pl.ds(row_start + t, 1)
        nr = lr * hr - li * hi + bur_s[row, cols]
        ni = lr * hi + li * hr + bui_s[row, cols]
        bur_s[row, cols] = nr
        bui_s[row, cols] = ni
        return nr, ni

    return lax.fori_loop(0, nrows, step, (hr, hi))


def _s5_project_out(u_ref, bur_s, bui_s, cblk_ref, d_ref, wglu_ref, bglu_ref, o_ref):
    nblk = cblk_ref.shape[0]
    half = cblk_ref.shape[1] // 2
    ys = []
    for i in range(nblk):
        cb = cblk_ref[i]
        y = _bdot(bur_s[:, i * half:(i + 1) * half], cb[:half]) + _bdot(bui_s[:, i * half:(i + 1) * half], cb[half:])
        ys.append(y)
    y = jnp.concatenate(ys, axis=1) + d_ref[...] * u_ref[...]
    z = jax.nn.gelu(y)
    z = z * jax.nn.sigmoid(_bdot(z, wglu_ref[...]) + bglu_ref[...])
    o_ref[...] = z.astype(o_ref.dtype)


def _s5_prompt_kernel(u_ref, bblk_ref, cblk_ref, lam_ref, d_ref, wglu_ref, bglu_ref,
                      o_ref, hr_ref, hi_ref, bur_s, bui_s, hc_s):
    j = pl.program_id(1)
    r = u_ref.shape[0]
    ns = bur_s.shape[1]

    @pl.when(j == 0)
    def _():
        hc_s[...] = jnp.zeros_like(hc_s)

    _s5_project_in(u_ref, bblk_ref, bur_s, bui_s)
    cw = min(S5_COL_CHUNK, ns)
    for cc in range(ns // cw):
        cols = slice(cc * cw, (cc + 1) * cw)
        hr, hi = _s5_scan_rows(bur_s, bui_s, lam_ref, cols, 0, r, hc_s[0:1, cols], hc_s[1:2, cols])
        hc_s[0:1, cols] = hr
        hc_s[1:2, cols] = hi
    hr_ref[0] = hc_s[0:1, :]
    hi_ref[0] = hc_s[1:2, :]
    _s5_project_out(u_ref, bur_s, bui_s, cblk_ref, d_ref, wglu_ref, bglu_ref, o_ref)


def _s5_sample_kernel(u_ref, bblk_ref, cblk_ref, lam_ref, d_ref, wglu_ref, bglu_ref, h0r_ref, h0i_ref,
                      o_ref, hr_ref, hi_ref, bur_s, bui_s, *, seq):
    r = u_ref.shape[0]
    ns = bur_s.shape[1]
    _s5_project_in(u_ref, bblk_ref, bur_s, bui_s)
    cw = min(S5_COL_CHUNK, ns)
    for cc in range(ns // cw):
        cols = slice(cc * cw, (cc + 1) * cw)

        def per_seq(b, carry, cols=cols):
            hr, hi = _s5_scan_rows(bur_s, bui_s, lam_ref, cols, b * seq, seq,
                                   h0r_ref[pl.ds(b, 1), cols], h0i_ref[pl.ds(b, 1), cols])
            hr_ref[pl.ds(b, 1), cols] = hr
            hi_ref[pl.ds(b, 1), cols] = hi
            return carry

        lax.fori_loop(0, r // seq, per_seq, 0)
    _s5_project_out(u_ref, bur_s, bui_s, cblk_ref, d_ref, wglu_ref, bglu_ref, o_ref)


def _s5_params(lam_re, lam_im, log_step, b_re, b_im, c_re, c_im):
    g, p = lam_re.shape
    gs = b_re.shape[-1]
    dt = jnp.exp(log_step.astype(F32))[:, None]
    ar, ai = lam_re.astype(F32), lam_im.astype(F32)
    mag = jnp.exp(ar * dt)
    lbr, lbi = mag * jnp.cos(ai * dt), mag * jnp.sin(ai * dt)
    den = ar * ar + ai * ai
    nr, ni = lbr - 1.0, lbi
    cr = (nr * ar + ni * ai) / den
    ci = (ni * ar - nr * ai) / den
    bbr = cr[..., None] * b_re - ci[..., None] * b_im
    bbi = cr[..., None] * b_im + ci[..., None] * b_re
    gb = S5_GROUP_BLOCK
    nblk = g // gb
    eye = jnp.eye(gb, dtype=F32)
    bb = jnp.stack([bbr, bbi]).reshape(2, nblk, gb, p, gs)
    bblk = jnp.einsum('qigpc,gh->igcqhp', bb, eye).reshape(nblk, gb * gs, 2 * gb * p)
    cc = jnp.stack([c_re.astype(F32), -c_im.astype(F32)]).reshape(2, nblk, gb, gs, p)
    cblk = jnp.einsum('qigcp,gh->iqgphc', cc, eye).reshape(nblk, 2 * gb * p, gb * gs)
    lam = jnp.stack([lbr.reshape(-1), lbi.reshape(-1)])
    lam = jnp.concatenate([lam, jnp.zeros((SUBLANES - 2, g * p), F32)], axis=0)
    return bblk, cblk, lam


def _s5_const_specs(bblk, cblk, lam, w, layer, nmap):
    z = lambda n: (lambda *g: (0,) * n)
    return [pl.BlockSpec(bblk.shape, z(3)), pl.BlockSpec(cblk.shape, z(3)), pl.BlockSpec(lam.shape, z(2)),
            pl.BlockSpec((None, 1, w), lambda *g: (layer, 0, 0)),
            pl.BlockSpec((None, w, w), lambda *g: (layer, 0, 0)),
            pl.BlockSpec((None, 1, w), lambda *g: (layer, 0, 0))]


def _s5_prompt(u, bblk, cblk, lam, d, wglu, bglu, layer, *, nb, seq, row0):
    w = u.shape[1]
    ns = lam.shape[1]
    r = min(ROW_TILE, seq)
    nt = seq // r
    t0 = row0 // r
    return pl.pallas_call(
        _s5_prompt_kernel,
        grid=(nb, nt),
        in_specs=[pl.BlockSpec((r, w), lambda b, j: (t0 + b * nt + j, 0))]
        + _s5_const_specs(bblk, cblk, lam, w, layer, 2),
        out_specs=[pl.BlockSpec((r, w), lambda b, j: (b * nt + j, 0)),
                   pl.BlockSpec((1, 1, ns), lambda b, j: (b, 0, 0)),
                   pl.BlockSpec((1, 1, ns), lambda b, j: (b, 0, 0))],
        out_shape=[jax.ShapeDtypeStruct((nb * seq, w), BF16),
                   jax.ShapeDtypeStruct((nb, 1, ns), F32), jax.ShapeDtypeStruct((nb, 1, ns), F32)],
        scratch_shapes=[pltpu.VMEM((r, ns), F32), pltpu.VMEM((r, ns), F32), pltpu.VMEM((SUBLANES, ns), F32)],
        compiler_params=_cparams(("arbitrary", "arbitrary")),
        name="s5_prompt",
    )(u, bblk, cblk, lam, d, wglu, bglu)


def _s5_sample(u, bblk, cblk, lam, d, wglu, bglu, h0r, h0i, layer, *, nb, seq, row0):
    w = u.shape[1]
    ns = lam.shape[1]
    r = min(ROW_TILE, nb * seq)
    nt = nb * seq // r
    spt = r // seq
    t0 = row0 // r
    return pl.pallas_call(
        functools.partial(_s5_sample_kernel, seq=seq),
        grid=(nt,),
        in_specs=[pl.BlockSpec((r, w), lambda i: (t0 + i, 0))]
        + _s5_const_specs(bblk, cblk, lam, w, layer, 1)
        + [pl.BlockSpec((spt, ns), lambda i: (i, 0)), pl.BlockSpec((spt, ns), lambda i: (i, 0))],
        out_specs=[pl.BlockSpec((r, w), lambda i: (i, 0)),
                   pl.BlockSpec((spt, ns), lambda i: (i, 0)),
                   pl.BlockSpec((spt, ns), lambda i: (i, 0))],
        out_shape=[jax.ShapeDtypeStruct((nb * seq, w), BF16),
                   jax.ShapeDtypeStruct((nb, ns), F32), jax.ShapeDtypeStruct((nb, ns), F32)],
        scratch_shapes=[pltpu.VMEM((r, ns), F32), pltpu.VMEM((r, ns), F32)],
        compiler_params=_cparams(("arbitrary",)),
        name="s5_sample",
    )(u, bblk, cblk, lam, d, wglu, bglu, h0r, h0i)


def _top2(x):
    n = x.shape[-1]
    ids = jnp.arange(n, dtype=jnp.int32)
    i1 = jnp.argmax(x, axis=-1).astype(jnp.int32)
    m1 = jnp.max(x, axis=-1)
    rest = jnp.where(ids == i1[..., None], -jnp.inf, x)
    i2 = jnp.argmax(rest, axis=-1).astype(jnp.int32)
    m2 = jnp.max(rest, axis=-1)
    return jnp.stack([m1, m2], -1), jnp.stack([i1, i2], -1)


def _route(probs, n_experts):
    epg = n_experts // N_EXPERT_GROUPS
    grouped = probs.reshape(-1, N_EXPERT_GROUPS, epg)
    group_score = _top2(grouped)[0].sum(-1)
    g_sel = jnp.argmax(group_score, axis=-1).astype(jnp.int32)
    in_group = jnp.sum(jnp.where(jnp.arange(N_EXPERT_GROUPS)[None, :, None] == g_sel[:, None, None], grouped, 0.0),
                       axis=1)
    top_p, top_i = _top2(in_group)
    gate = top_p / jnp.sum(top_p, -1, keepdims=True)
    return g_sel[:, None] * epg + top_i.astype(jnp.int32), gate


def _dispatch(expert_idx, n_experts, bm):
    t = expert_idx.shape[0]
    n_pairs = t * TOP_K
    flat_e = expert_idx.reshape(-1)
    flat_tok = jnp.repeat(jnp.arange(t, dtype=jnp.int32), TOP_K)
    onehot = (flat_e[:, None] == jnp.arange(n_experts, dtype=jnp.int32)[None, :]).astype(jnp.int32)
    csum = jnp.cumsum(onehot, axis=0)
    counts = csum[-1]
    rank = jnp.sum((csum - onehot) * onehot, axis=1)
    padded = (counts + bm - 1) // bm * bm
    pstart = jnp.cumsum(padded) - padded
    pair_slot = jnp.sum(onehot * pstart[None, :], axis=1) + rank
    n_blocks = -(-(n_pairs + n_experts * (bm - 1)) // bm)
    n_slots = n_blocks * bm
    slot_tok = jnp.zeros((n_slots,), jnp.int32).at[pair_slot].set(flat_tok)
    block_exp = jnp.minimum(jnp.searchsorted(jnp.cumsum(padded), jnp.arange(n_blocks, dtype=jnp.int32) * bm,
                                             side='right'), n_experts - 1).astype(jnp.int32)
    n_used = (jnp.sum(padded) // bm).astype(jnp.int32).reshape(1)
    return slot_tok, pair_slot, block_exp, n_used, n_blocks


def _row_copy(src_hbm, dst, src_row, dst_row, sem):
    return pltpu.make_async_copy(src_hbm.at[pl.ds(src_row, 1)], dst.at[pl.ds(dst_row, 1)], sem)


def _gather_rows_kernel(tok_ref, nu_ref, x_hbm, o_ref, buf, sem):
    rows = buf.shape[1]
    i = pl.program_id(0)
    n_live = nu_ref[0]

    def issue(step):
        slot = step % 2

        @pl.when(step < n_live)
        def _():
            def body(r, c):
                _row_copy(x_hbm, buf.at[slot], tok_ref[step * rows + r], r, sem.at[slot]).start()
                return c

            lax.fori_loop(0, rows, body, 0)

    @pl.when(i == 0)
    def _():
        issue(i)

    issue(i + 1)

    @pl.when(i < n_live)
    def _():
        slot = i % 2

        def body(r, c):
            _row_copy(x_hbm, buf.at[slot], 0, r, sem.at[slot]).wait()
            return c

        lax.fori_loop(0, rows, body, 0)
        half = buf.shape[2]
        lo, hi = _unpack_bf16_pairs(buf[slot])
        o_ref[:, :half] = lo
        o_ref[:, half:] = hi

    @pl.when(i >= n_live)
    def _():
        o_ref[...] = jnp.zeros_like(o_ref)


def _gather_rows(x_packed, slot_tok, n_used, bm):
    n_slots = slot_tok.shape[0]
    x = x_packed
    d = 2 * x.shape[1]
    rows = min(bm, GATHER_ROWS)
    n_live = n_used * (bm // rows)
    return pl.pallas_call(
        _gather_rows_kernel,
        grid_spec=pltpu.PrefetchScalarGridSpec(
            num_scalar_prefetch=2,
            grid=(n_slots // rows,),
            in_specs=[pl.BlockSpec(memory_space=pl.ANY)],
            out_specs=pl.BlockSpec((rows, d), lambda i, tok, nu: (i, 0)),
            scratch_shapes=[pltpu.VMEM((2, rows, d // 2), jnp.uint32), pltpu.SemaphoreType.DMA((2,))]),
        out_shape=jax.ShapeDtypeStruct((n_slots, d), BF16),
        compiler_params=_cparams(("arbitrary",)),
        name="moe_gather",
    )(slot_tok, n_live, x)


def _expert_changed(be_ref, blk):
    return jnp.logical_or(blk == 0, be_ref[blk] != be_ref[jnp.maximum(blk - 1, 0)])


def _moe_gate_up_kernel(be_ref, nu_ref, x_ref, wg_ref, wu_ref, o_ref, wg_sc, wu_sc):
    blk = pl.program_id(1)

    @pl.when(_expert_changed(be_ref, blk))
    def _():
        wg_sc[...] = wg_ref[...].astype(BF16)
        wu_sc[...] = wu_ref[...].astype(BF16)

    @pl.when(blk < nu_ref[0])
    def _():
        x = x_ref[...]
        g = jnp.dot(x, wg_sc[...], preferred_element_type=F32)
        u = jnp.dot(x, wu_sc[...], preferred_element_type=F32)
        o_ref[...] = (_silu(g) * u).astype(o_ref.dtype)

    @pl.when(blk >= nu_ref[0])
    def _():
        o_ref[...] = jnp.zeros_like(o_ref)


def _moe_down_kernel(be_ref, nu_ref, h_ref, wd_ref, o_ref, wd_sc):
    blk = pl.program_id(1)

    @pl.when(_expert_changed(be_ref, blk))
    def _():
        wd_sc[...] = wd_ref[...].astype(BF16)

    @pl.when(blk < nu_ref[0])
    def _():
        o_ref[...] = jnp.dot(h_ref[...], wd_sc[...], preferred_element_type=F32)

    @pl.when(blk >= nu_ref[0])
    def _():
        o_ref[...] = jnp.zeros_like(o_ref)


def _moe_experts(xs, block_exp, n_used, w_gate, w_up, w_down, layer, bm):
    n_slots, d = xs.shape
    ff = w_gate.shape[-1]
    nblk = n_slots // bm
    tn = _pick(ff, MOE_TN_UP)
    w_in_spec = pl.BlockSpec((None, None, d, tn), lambda j, b, be, nu: (layer, be[b], 0, j))
    hm = pl.pallas_call(
        _moe_gate_up_kernel,
        grid_spec=pltpu.PrefetchScalarGridSpec(
            num_scalar_prefetch=2,
            grid=(ff // tn, nblk),
            in_specs=[pl.BlockSpec((bm, d), lambda j, b, be, nu: (b, 0)), w_in_spec, w_in_spec],
            out_specs=pl.BlockSpec((bm, tn), lambda j, b, be, nu: (b, j)),
            scratch_shapes=[pltpu.VMEM((d, tn), BF16), pltpu.VMEM((d, tn), BF16)]),
        out_shape=jax.ShapeDtypeStruct((n_slots, ff), BF16),
        compiler_params=_cparams(("arbitrary", "arbitrary")),
        name="moe_gate_up",
    )(block_exp, n_used, xs, w_gate, w_up)
    tn2 = _pick(d, MOE_TN_DOWN)
    return pl.pallas_call(
        _moe_down_kernel,
        grid_spec=pltpu.PrefetchScalarGridSpec(
            num_scalar_prefetch=2,
            grid=(d // tn2, nblk),
            in_specs=[pl.BlockSpec((bm, ff), lambda j, b, be, nu: (b, 0)),
                      pl.BlockSpec((None, None, ff, tn2), lambda j, b, be, nu: (layer, be[b], 0, j))],
            out_specs=pl.BlockSpec((bm, tn2), lambda j, b, be, nu: (b, j)),
            scratch_shapes=[pltpu.VMEM((ff, tn2), BF16)]),
        out_shape=jax.ShapeDtypeStruct((n_slots, d), F32),
        compiler_params=_cparams(("arbitrary", "arbitrary")),
        name="moe_down",
    )(block_exp, n_used, hm, w_down)


def _combine_ln_kernel(slot_ref, x_ref, w_ref, g_ref, b_ref, yb_hbm, of_ref, ob_ref, buf, sem, *, alpha):
    tm = x_ref.shape[0]
    i = pl.program_id(0)

    def issue(step):
        slot = step % 2

        def body(r, c):
            for k in range(TOP_K):
                _row_copy(yb_hbm, buf.at[slot, k], slot_ref[(step * tm + r) * TOP_K + k], r, sem.at[slot]).start()
            return c

        lax.fori_loop(0, tm, body, 0)

    @pl.when(i == 0)
    def _():
        issue(i)

    @pl.when(i + 1 < pl.num_programs(0))
    def _():
        issue(i + 1)

    slot = i % 2

    def wait(r, c):
        for k in range(TOP_K):
            _row_copy(yb_hbm, buf.at[slot, k], 0, r, sem.at[slot]).wait()
        return c

    lax.fori_loop(0, tm, wait, 0)
    w = w_ref[...]
    f = w[:, 0:1] * buf[slot, 0] + w[:, 1:2] * buf[slot, 1]
    z = _ln_math(alpha * x_ref[...] + f, g_ref[...], b_ref[...])
    of_ref[...] = z
    ob_ref[...] = z.astype(BF16)


def _combine_ln(x, yb, pair_slot, gate, g, b, alpha, layer):
    t, d = x.shape
    tm = min(128, t)
    return pl.pallas_call(
        functools.partial(_combine_ln_kernel, alpha=alpha),
        grid_spec=pltpu.PrefetchScalarGridSpec(
            num_scalar_prefetch=1,
            grid=(t // tm,),
            in_specs=[pl.BlockSpec((tm, d), lambda i, s: (i, 0)),
                      pl.BlockSpec((tm, TOP_K), lambda i, s: (i, 0)),
                      pl.BlockSpec((None, 1, d), lambda i, s: (layer, 0, 0)),
                      pl.BlockSpec((None, 1, d), lambda i, s: (layer, 0, 0)),
                      pl.BlockSpec(memory_space=pl.ANY)],
            out_specs=[pl.BlockSpec((tm, d), lambda i, s: (i, 0)),
                       pl.BlockSpec((tm, d), lambda i, s: (i, 0))],
            scratch_shapes=[pltpu.VMEM((2, TOP_K, tm, d), F32), pltpu.SemaphoreType.DMA((2,))]),
        out_shape=[jax.ShapeDtypeStruct((t, d), F32), jax.ShapeDtypeStruct((t, d), BF16)],
        compiler_params=_cparams(("arbitrary",)),
        name="moe_combine_ln2",
    )(pair_slot, x, gate, g, b, yb)


def kernel(x_prompt, x_sample, state_delta, state_conv, state_s5, state_ret, ln_in_g, ln_in_b, w_in, dn_conv_w, dn_a_log, dn_dt_bias, dn_norm_w, s5_lam_re, s5_lam_im, s5_log_step, s5_b_re, s5_b_im, s5_c_re, s5_c_im, s5_d, s5_w_glu, s5_b_glu, w_up_dn, w_up_s5, w_up_ret, w_o, ln1_g, ln1_b, router_w, router_b, w_gate_e, w_up_e, w_down_e, ln2_g, ln2_b):
    bp, lp, d = x_prompt.shape
    bs, ls, _ = x_sample.shape
    depth = w_in.shape[0]
    nh, dk, dv = state_delta.shape[2:]
    qkv = dn_conv_w.shape[2]
    hist = state_conv.shape[2]
    dnw = nh * dv
    s5w = s5_d.shape[1]
    g5, p5 = s5_lam_re.shape[1:]
    rh, rdk, rdv = state_ret.shape[2:]
    rw = rh * rdv
    n_exp = router_w.shape[1]
    tp, ts = bp * lp, bs * ls
    t = tp + ts
    alpha = (2 * depth) ** 0.25
    assert ls == SUBLANES and qkv == 3 * dnw and dk == dv == rdk == rdv == LANES

    sizes = (qkv, dnw, nh, nh, s5w, rh * rdk, rh * rdk, rw, rw, 3 * d)
    offs = [0]
    for s in sizes:
        offs.append(offs[-1] + s)
    o_dn, o_b, o_s5, o_ret, o_gates = offs[0], offs[2], offs[4], offs[5], offs[9]
    assert w_in.shape[2] == offs[-1] and 2 * nh < LANES
    w_in_t = jnp.swapaxes(w_in, 1, 2)

    xf, xb = _ln_in(x_prompt.reshape(tp, d), x_sample.reshape(ts, d), ln_in_g, ln_in_b)

    ep = -(-n_exp // LANES) * LANES
    rw_pad = jnp.pad(router_w.astype(F32), ((0, 0), (0, ep - n_exp)))
    rb_pad = jnp.pad(router_b.astype(F32).reshape(1, n_exp), ((0, 0), (0, ep - n_exp)), constant_values=-1e30)

    cs_p, sn_p = _rope_tables(jnp.arange(lp, dtype=jnp.int32), rdk)
    cs_s, sn_s = _rope_tables(PAST_LEN + jnp.arange(ls, dtype=jnp.int32), rdk)
    rs = min(ROW_TILE, ts)
    cs_s = jnp.tile(cs_s, (rs // ls, 1))
    sn_s = jnp.tile(sn_s, (rs // ls, 1))

    new_delta_p, new_conv_p, new_s5_p, new_ret_p = [], [], [], []
    new_delta_s, new_conv_s, new_s5_s, new_ret_s = [], [], [], []

    for l in range(depth):
        p_dn = _matmul_nt(xb, w_in_t, layer=l, row0=o_dn, n=qkv + dnw, tile_major=True,
                          tn_pref=min(MIXER_TN, dnw), name="proj_dn")
        ba = _matmul_nt(xb, w_in_t, layer=l, row0=o_b, n=LANES, name="proj_ba")
        u5 = _matmul_nt(xb, w_in_t, layer=l, row0=o_s5, n=s5w, name="proj_s5")
        p_ret = _matmul_nt(xb, w_in_t, layer=l, row0=o_ret, n=4 * rw, tile_major=True,
                           tn_pref=min(MIXER_TN, rw), name="proj_ret")
        gates = _matmul_nt(xb, w_in_t, layer=l, row0=o_gates, n=3 * d, tn_pref=512, name="proj_gates")

        tn = p_dn.shape[-1]
        bat = ba[:, :2 * nh].T
        zpad = jnp.zeros((nh,), F32)
        hpr = jnp.pad(jnp.stack([jnp.concatenate([zpad, dn_a_log[l].astype(F32)]),
                                 jnp.concatenate([zpad, dn_dt_bias[l].astype(F32)])]),
                      ((0, 0), (0, LANES - 2 * nh)))
        hpc = hpr[:, :2 * nh].T
        norm_w = dn_norm_w.reshape(depth, 1, dv)

        gates_p = _delta_gate(ba, bat, hpr, hpc, nh=nh, c=math.gcd(lp, CHUNK), row0=0, nrows=tp)
        gates_s = _delta_gate(ba, bat, hpr, hpc, nh=nh, c=ls, row0=tp, nrows=ts)
        odn_p, dlt_p = _delta_prompt(p_dn, gates_p, dn_conv_w, norm_w, l,
                                     nb=bp, seq=lp, nh=nh, dk=dk, dv=dv, row0=0)
        bufx = jnp.pad(state_conv[l], ((0, 0), (0, ls - hist), (0, 0))).reshape(ts, qkv)
        odn_s, dlt_s = _delta_sample(p_dn, gates_s, dn_conv_w, norm_w, bufx, state_delta, l,
                                     nb=bs, seq=ls, nh=nh, dk=dk, dv=dv, row0=tp, hist=hist)
        nq = qkv // tn
        cp = jnp.stack([lax.slice(p_dn, (0, b * lp + lp - hist, 0), (nq, (b + 1) * lp, tn))
                        for b in range(bp)], axis=1)
        cs_ = lax.slice(p_dn, (0, tp, 0), (nq, t, tn)).reshape(nq, bs, ls, tn)[:, :, ls - hist:]
        new_conv_p.append(jnp.moveaxis(cp, 0, 2).reshape(bp, hist, qkv))
        new_conv_s.append(jnp.moveaxis(cs_, 0, 2).reshape(bs, hist, qkv))
        new_delta_p.append(dlt_p)
        new_delta_s.append(dlt_s)

        bblk, cblk, lam = _s5_params(s5_lam_re[l], s5_lam_im[l], s5_log_step[l], s5_b_re[l], s5_b_im[l],
                                     s5_c_re[l], s5_c_im[l])
        d5 = s5_d.reshape(depth, 1, s5w)
        bg5 = s5_b_glu.reshape(depth, 1, s5w)
        os5_p, hr_p, hi_p = _s5_prompt(u5, bblk, cblk, lam, d5, s5_w_glu, bg5, l, nb=bp, seq=lp, row0=0)
        h0 = state_s5[l].astype(F32)
        os5_s, hr_s, hi_s = _s5_sample(u5, bblk, cblk, lam, d5, s5_w_glu, bg5,
                                       h0[..., 0].reshape(bs, g5 * p5), h0[..., 1].reshape(bs, g5 * p5), l,
                                       nb=bs, seq=ls, row0=tp)
        new_s5_p.append(jnp.stack([hr_p.reshape(bp, g5, p5), hi_p.reshape(bp, g5, p5)], -1))
        new_s5_s.append(jnp.stack([hr_s.reshape(bs, g5, p5), hi_s.reshape(bs, g5, p5)], -1))

        oret_p, rt_p = _ret_prompt(p_ret, cs_p, sn_p, nb=bp, seq=lp, nh=rh, dk=rdk, dv=rdv, row0=0)
        oret_s, rt_s = _ret_sample(p_ret, cs_s, sn_s, state_ret, l, nb=bs, seq=ls, nh=rh, dk=rdk, dv=rdv,
                                   row0=tp)
        new_ret_p.append(rt_p)
        new_ret_s.append(rt_s)

        merged = _upmerge((odn_p, os5_p, oret_p), (odn_s, os5_s, oret_s), gates, w_up_dn, w_up_s5, w_up_ret, l)
        mix = _matmul(merged, w_o, layer=l, tn_pref=512, name="w_o")
        xf, x_packed, probs = _ln1_router(xf, mix, ln1_g.reshape(depth, 1, d), ln1_b.reshape(depth, 1, d),
                                    rw_pad, rb_pad, alpha, l)

        expert_idx, gate = _route(probs[:, :n_exp], n_exp)
        slot_tok, pair_slot, block_exp, n_used, _ = _dispatch(expert_idx, n_exp, MOE_BM)
        xs = _gather_rows(x_packed, slot_tok, n_used, MOE_BM)
        yb = _moe_experts(xs, block_exp, n_used, w_gate_e, w_up_e, w_down_e, l, MOE_BM)
        xf, xb = _combine_ln(xf, yb, pair_slot, gate, ln2_g.reshape(depth, 1, d), ln2_b.reshape(depth, 1, d),
                             alpha, l)

    y_prompt = xf[:tp].reshape(bp, lp, d)
    y_sample = xf[tp:].reshape(bs, ls, d)
    st = lambda xs_, ref: jnp.stack(xs_).astype(ref.dtype)
    return (y_prompt, y_sample,
            st(new_delta_p, state_delta), st(new_conv_p, state_conv), st(new_s5_p, state_s5),
            st(new_ret_p, state_ret),
            st(new_delta_s, state_delta), st(new_conv_s, state_conv), st(new_s5_s, state_s5),
            st(new_ret_s, state_ret))
```
